```python
import math
import jax
import jax.numpy as jnp
from jax import lax
import numpy as np

D_MODEL = 1024
BATCH = 8
SEQ = 4096
DEPTH = 2
DEC_BATCH = 32
DEC_SEQ = 4
PAST_LEN = 16384
PAGE_SIZE = 128

N_EVEN = (DEPTH + 1) // 2
N_ODD = DEPTH // 2
ML_HEADS = 4
ML_DH = 128
ML_WIDTH = ML_HEADS * ML_DH
ML_CONV = 4
ML_CHUNK = 128
SB_HEADS = 8
SB_DH = 64
SB_WIDTH = SB_HEADS * SB_DH
SB_BLOCK = 128
SB_SCALE = SB_DH ** -0.5
SB_BIAS_HI = -3.0
SB_BIAS_LO = -10.0
EVEN_IN = 2 * ML_WIDTH + ML_WIDTH + ML_WIDTH + 2 * ML_HEADS + 3 * SB_WIDTH
EVEN_OUT = ML_WIDTH + SB_WIDTH
CM_WIDTH = 2 * D_MODEL
CM_GROUPS = 8
CM_GW = CM_WIDTH // CM_GROUPS
CM_CHUNK = 128
D_FF = 2816
FFN_CONV = 3
EPS = 1e-6

kernel_name = 'mlstm_stickbreak_chunkgmlp_convffn_step'


def rmsnorm(x, g):
    xf = x.astype(jnp.float32)
    y = xf * lax.rsqrt(jnp.mean(xf * xf, axis=-1, keepdims=True) + EPS)
    return (y * g.astype(jnp.float32)).astype(x.dtype)


def layernorm(x, g, b):
    xf = x.astype(jnp.float32)
    xc = xf - jnp.mean(xf, axis=-1, keepdims=True)
    y = xc * lax.rsqrt(jnp.mean(xc * xc, axis=-1, keepdims=True) + EPS)
    return (y * g.astype(jnp.float32) + b.astype(jnp.float32)).astype(x.dtype)


def causal_dwconv(x, w, b, buf):
    width = w.shape[0]
    length = x.shape[1]
    xp = jnp.concatenate([buf.astype(x.dtype), x], axis=1)
    y = b + xp[:, 0:length] * w[0]
    for i in range(1, width):
        y = y + xp[:, i:i + length] * w[i]
    return y.astype(x.dtype), xp[:, length:]


def split_even(a):
    sizes = [2 * ML_WIDTH, ML_WIDTH, ML_WIDTH, 2 * ML_HEADS, SB_WIDTH, SB_WIDTH]
    idx = [int(s) for s in np.cumsum(sizes)]
    return jnp.split(a, idx, axis=-1)


def mlstm_chunk(carry, xs):
    c, n, m = carry
    q, k, v, log_i, log_f = xs
    length = q.shape[2]
    b = jnp.cumsum(log_f, axis=-1)
    causal = jnp.tril(jnp.ones((length, length), bool))
    dmat = jnp.where(causal, b[..., :, None] - b[..., None, :] + log_i[..., None, :], -jnp.inf)
    inter = b + m[..., None]
    m_t = jnp.maximum(inter, jnp.max(dmat, axis=-1))
    w_inter = jnp.exp(inter - m_t)
    s = jnp.einsum('bhtd,bhsd->bhts', q, k) * jnp.exp(dmat - m_t[..., None])
    num = w_inter[..., None] * jnp.einsum('bhvd,bhtd->bhtv', c, q) + jnp.einsum('bhts,bhsv->bhtv', s, v)
    den = w_inter * jnp.einsum('bhd,bhtd->bht', n, q) + jnp.sum(s, axis=-1)
    h = num / jnp.maximum(jnp.abs(den), jnp.exp(-m_t))[..., None]
    b_last = b[..., -1]
    g = b_last[..., None] - b + log_i
    m_new = jnp.maximum(b_last + m, jnp.max(g, axis=-1))
    a_s = jnp.exp(g - m_new[..., None])
    a_c = jnp.exp(b_last + m - m_new)
    c_new = a_c[..., None, None] * c + jnp.einsum('bhs,bhsv,bhsd->bhvd', a_s, v, k)
    n_new = a_c[..., None] * n + jnp.einsum('bhs,bhsd->bhd', a_s, k)
    return (c_new, n_new, m_new), h


def mlstm_mixer(qk_pre, v_m, o_pre, if_pre, conv_w, conv_b, gate_b, head_g, conv_buf, c0, n0, m0):
    bsz, length, _ = qk_pre.shape
    qk, new_buf = causal_dwconv(qk_pre, conv_w, conv_b, conv_buf)
    qk = jax.nn.silu(qk.astype(jnp.float32))

    def heads(t):
        return t.reshape(bsz, length, ML_HEADS, ML_DH).transpose(0, 2, 1, 3).astype(jnp.float32)

    q = heads(qk[..., :ML_WIDTH])
    k = heads(qk[..., ML_WIDTH:]) * (ML_DH ** -0.5)
    v = heads(v_m)
    gates = (if_pre.astype(jnp.float32) + gate_b.astype(jnp.float32)).transpose(0, 2, 1)
    log_i = gates[:, :ML_HEADS]
    log_f = jax.nn.log_sigmoid(gates[:, ML_HEADS:])
    chunk = ML_CHUNK if length % ML_CHUNK == 0 else length
    nc = length // chunk

    def to_chunks(t):
        return jnp.moveaxis(t.reshape(bsz, ML_HEADS, nc, chunk, *t.shape[3:]), 2, 0)

    xs = (to_chunks(q), to_chunks(k), to_chunks(v), to_chunks(log_i), to_chunks(log_f))
    init = (c0.astype(jnp.float32), n0.astype(jnp.float32), m0.astype(jnp.float32))
    (c, n, m), h = lax.scan(mlstm_chunk, init, xs)
    h = jnp.moveaxis(h, 0, 2).reshape(bsz, ML_HEADS, length, ML_DH).transpose(0, 2, 1, 3)
    h = jax.nn.sigmoid(o_pre.astype(jnp.float32)).reshape(bsz, length, ML_HEADS, ML_DH) * h
    h = rmsnorm(h, head_g.reshape(ML_HEADS, ML_DH)).reshape(bsz, length, ML_WIDTH)
    return h.astype(qk_pre.dtype), c, n, m, new_buf


def sb_weights(z, mask):
    log_1m = jnp.where(mask, jax.nn.log_sigmoid(-z), 0.0)
    rest = lax.cumsum(log_1m, axis=z.ndim - 1, reverse=True) - log_1m
    return jnp.where(mask, jnp.exp(jax.nn.log_sigmoid(z) + rest), 0.0)


def sb_prompt(q, k, v, bias):
    bsz, length = q.shape[:2]
    blk = SB_BLOCK if length % SB_BLOCK == 0 else length
    nb = length // blk
    kf = k.astype(jnp.float32)
    vf = v.astype(jnp.float32)
    bf = bias.astype(jnp.float32)[None, :, None, None]
    k_pos = jnp.arange(length)
    qb = jnp.moveaxis(q.astype(jnp.float32).reshape(bsz, nb, blk, SB_HEADS, SB_DH), 1, 0)
    q_pos = jnp.arange(length).reshape(nb, blk)

    def one_block(args):
        qi, pi = args
        z = jnp.einsum('bqhd,bkhd->bhqk', qi, kf) * SB_SCALE + bf
        a = sb_weights(z, k_pos[None, :] < pi[:, None])
        return jnp.einsum('bhqk,bkhd->bqhd', a, vf)

    o = lax.map(one_block, (qb, q_pos))
    return jnp.moveaxis(o, 0, 1).reshape(bsz, length, SB_WIDTH).astype(q.dtype)


def sb_sample(q, k, v, past_k, past_v, bias):
    bsz, n_new = q.shape[:2]
    past_len = past_k.shape[1]
    qf = q.astype(jnp.float32)
    z = jnp.concatenate([jnp.einsum('bqhd,bkhd->bhqk', qf, past_k.astype(jnp.float32)),
                         jnp.einsum('bqhd,bkhd->bhqk', qf, k.astype(jnp.float32))], axis=-1) * SB_SCALE
    z = z + bias.astype(jnp.float32)[None, :, None, None]
    q_pos = past_len + jnp.arange(n_new)
    k_pos = jnp.arange(past_len + n_new)
    a = sb_weights(z, k_pos[None, :] < q_pos[:, None])
    o = (jnp.einsum('bhqk,bkhd->bqhd', a[..., :past_len], past_v.astype(jnp.float32))
         + jnp.einsum('bhqk,bkhd->bqhd', a[..., past_len:], v.astype(jnp.float32)))
    return o.reshape(bsz, n_new, SB_WIDTH).astype(q.dtype)


def chunk_mlp(xn, w_in, ln_g, ln_b, ws, bs, chunk_len):
    a = jax.nn.gelu(xn @ w_in)
    u, v = jnp.split(a, 2, axis=-1)
    v = layernorm(v, ln_g, ln_b)
    bsz, length, _ = v.shape
    nc = length // chunk_len
    vg = v.reshape(bsz, nc, chunk_len, CM_GROUPS, CM_GW)
    w = jnp.tril(ws[:, :chunk_len, :chunk_len])
    mixed = jnp.einsum('gts,bcsgd->bctgd', w, vg) + bs[:, :chunk_len].T[None, None, :, :, None]
    return u * mixed.reshape(bsz, length, CM_WIDTH).astype(u.dtype), v


def conv_ffn(xn, w_in, cw, cb, w_out, buf):
    a = xn @ w_in
    g, u = jnp.split(a, 2, axis=-1)
    g, new_buf = causal_dwconv(g, cw, cb, buf)
    return (jax.nn.gelu(g) * u) @ w_out, new_buf


def setup_inputs(seed: int = 0) -> dict:
    key = jax.random.key(seed)
    ks = jax.random.split(key, 32)
    n_pages = PAST_LEN // PAGE_SIZE
    n_used = DEC_BATCH * n_pages
    n_pool = n_used + n_used // 4
    nrm = jax.random.normal
    f32 = jnp.float32
    page_table = jax.random.permutation(ks[4], n_pool)[:n_used].reshape(DEC_BATCH, n_pages).astype(jnp.int32)
    gate_b = jnp.concatenate([-1.0 + 0.1 * nrm(ks[15], (N_EVEN, ML_HEADS), f32),
                              3.0 + 0.5 * nrm(ks[16], (N_EVEN, ML_HEADS), f32)], axis=-1)
    sb_b = jnp.linspace(SB_BIAS_HI, SB_BIAS_LO, SB_HEADS, dtype=f32)[None, :] + 0.1 * nrm(ks[30], (N_EVEN, SB_HEADS), f32)
    return {
        'x_prompt': nrm(ks[0], (BATCH, SEQ, D_MODEL), f32),
        'x_sample': nrm(ks[1], (DEC_BATCH, DEC_SEQ, D_MODEL), f32),
        'cache_k': nrm(ks[2], (N_EVEN, n_pool, PAGE_SIZE, SB_HEADS, SB_DH), f32),
        'cache_v': nrm(ks[3], (N_EVEN, n_pool, PAGE_SIZE, SB_HEADS, SB_DH), f32),
        'page_table': page_table,
        'state_mlstm_c': 0.5 * nrm(ks[5], (N_EVEN, DEC_BATCH, ML_HEADS, ML_DH, ML_DH), f32),
        'state_mlstm_n': 0.5 * nrm(ks[6], (N_EVEN, DEC_BATCH, ML_HEADS, ML_DH), f32),
        'state_mlstm_m': nrm(ks[7], (N_EVEN, DEC_BATCH, ML_HEADS), f32),
        'state_mlstm_conv': nrm(ks[8], (N_EVEN, DEC_BATCH, ML_CONV - 1, 2 * ML_WIDTH), f32),
        'state_ffn_conv': nrm(ks[9], (DEPTH, DEC_BATCH, FFN_CONV - 1, D_FF), f32),
        'norm_mix_g': 1.0 + 0.01 * nrm(ks[10], (DEPTH, D_MODEL), f32),
        'norm_ffn_g': 1.0 + 0.01 * nrm(ks[11], (DEPTH, D_MODEL), f32),
        'norm_final_g': 1.0 + 0.01 * nrm(ks[12], (D_MODEL,), f32),
        'w_in_even': nrm(ks[13], (N_EVEN, D_MODEL, EVEN_IN), f32) * D_MODEL ** -0.5,
        'ml_conv_w': nrm(ks[14], (N_EVEN, ML_CONV, 2 * ML_WIDTH), f32) * ML_CONV ** -0.5,
        'ml_conv_b': 0.01 * nrm(ks[17], (N_EVEN, 2 * ML_WIDTH), f32),
        'ml_gate_b': gate_b,
        'ml_head_g': 1.0 + 0.01 * nrm(ks[18], (N_EVEN, ML_WIDTH), f32),
        'sb_logit_b': sb_b,
        'w_out_even': nrm(ks[19], (N_EVEN, EVEN_OUT, D_MODEL), f32) * 0.5 * EVEN_OUT ** -0.5,
        'w_in_odd': nrm(ks[20], (N_ODD, D_MODEL, 2 * CM_WIDTH), f32) * D_MODEL ** -0.5,
        'cm_ln_g': 1.0 + 0.01 * nrm(ks[21], (N_ODD, CM_WIDTH), f32),
        'cm_ln_b': 0.01 * nrm(ks[22], (N_ODD, CM_WIDTH), f32),
        'cm_spatial_w': nrm(ks[23], (N_ODD, CM_GROUPS, CM_CHUNK, CM_CHUNK), f32) * 0.5 * CM_CHUNK ** -0.5,
        'cm_spatial_b': 1.0 + 0.1 * nrm(ks[24], (N_ODD, CM_GROUPS, CM_CHUNK), f32),
        'w_out_odd': nrm(ks[25], (N_ODD, CM_WIDTH, D_MODEL), f32) * 0.5 * CM_WIDTH ** -0.5,
        'ffn_w_in': nrm(ks[26], (DEPTH, D_MODEL, 2 * D_FF), f32) * D_MODEL ** -0.5,
        'ffn_conv_w': nrm(ks[27], (DEPTH, FFN_CONV, D_FF), f32) * FFN_CONV ** -0.5,
        'ffn_conv_b': 0.01 * nrm(ks[28], (DEPTH, D_FF), f32),
        'ffn_w_out': nrm(ks[29], (DEPTH, D_FF, D_MODEL), f32) * 0.5 * D_FF ** -0.5,
    }


def reference(x_prompt, x_sample, cache_k, cache_v, page_table, state_mlstm_c, state_mlstm_n,
              state_mlstm_m, state_mlstm_conv, state_ffn_conv, norm_mix_g, norm_ffn_g, norm_final_g,
              w_in_even, ml_conv_w, ml_conv_b, ml_gate_b, ml_head_g, sb_logit_b, w_out_even, w_in_odd,
              cm_ln_g, cm_ln_b, cm_spatial_w, cm_spatial_b, w_out_odd, ffn_w_in, ffn_conv_w, ffn_conv_b,
              ffn_w_out):
    n_pb, seq_p = x_prompt.shape[:2]
    n_sb, seq_s = x_sample.shape[:2]
    past_len = page_table.shape[1] * cache_k.shape[2]
    hp, hs = x_prompt, x_sample
    kp_l, vp_l, ks_l, vs_l = [], [], [], []
    cp_l, np_l, mp_l, bp_l = [], [], [], []
    cs_l, ns_l, ms_l, bs_l = [], [], [], []
    cmv_l, fp_l, fs_l = [], [], []
    for layer in range(DEPTH):
        if layer % 2 == 0:
            e = layer // 2
            qk_p, vm_p, o_p, if_p, sq_p, sk_p, sv_p = split_even(rmsnorm(hp, norm_mix_g[layer]) @ w_in_even[e])
            h_ml, c_p, n_p, m_p, buf_p = mlstm_mixer(
                qk_p, vm_p, o_p, if_p, ml_conv_w[e], ml_conv_b[e], ml_gate_b[e], ml_head_g[e],
                jnp.zeros((n_pb, ML_CONV - 1, 2 * ML_WIDTH), hp.dtype),
                jnp.zeros((n_pb, ML_HEADS, ML_DH, ML_DH), jnp.float32),
                jnp.zeros((n_pb, ML_HEADS, ML_DH), jnp.float32),
                jnp.zeros((n_pb, ML_HEADS), jnp.float32))
            k_p = sk_p.reshape(n_pb, seq_p, SB_HEADS, SB_DH)
            v_p = sv_p.reshape(n_pb, seq_p, SB_HEADS, SB_DH)
            h_sb = sb_prompt(sq_p.reshape(n_pb, seq_p, SB_HEADS, SB_DH), k_p, v_p, sb_logit_b[e])
            hp = hp + jnp.concatenate([h_ml, h_sb], axis=-1) @ w_out_even[e]
            kp_l.append(k_p)
            vp_l.append(v_p)
            cp_l.append(c_p)
            np_l.append(n_p)
            mp_l.append(m_p)
            bp_l.append(buf_p)
            qk_s, vm_s, o_s, if_s, sq_s, sk_s, sv_s = split_even(rmsnorm(hs, norm_mix_g[layer]) @ w_in_even[e])
            h_ml, c_s, n_s, m_s, buf_s = mlstm_mixer(
                qk_s, vm_s, o_s, if_s, ml_conv_w[e], ml_conv_b[e], ml_gate_b[e], ml_head_g[e],
                state_mlstm_conv[e], state_mlstm_c[e], state_mlstm_n[e], state_mlstm_m[e])
            k_s = sk_s.reshape(n_sb, seq_s, SB_HEADS, SB_DH)
            v_s = sv_s.reshape(n_sb, seq_s, SB_HEADS, SB_DH)
            past_k = cache_k[e][page_table].reshape(n_sb, past_len, SB_HEADS, SB_DH)
            past_v = cache_v[e][page_table].reshape(n_sb, past_len, SB_HEADS, SB_DH)
            h_sb = sb_sample(sq_s.reshape(n_sb, seq_s, SB_HEADS, SB_DH), k_s, v_s, past_k, past_v, sb_logit_b[e])
            hs = hs + jnp.concatenate([h_ml, h_sb], axis=-1) @ w_out_even[e]
            ks_l.append(k_s)
            vs_l.append(v_s)
            cs_l.append(c_s)
            ns_l.append(n_s)
            ms_l.append(m_s)
            bs_l.append(buf_s)
        else:
            o = layer // 2
            h_cm, _ = chunk_mlp(rmsnorm(hp, norm_mix_g[layer]), w_in_odd[o], cm_ln_g[o], cm_ln_b[o],
                                cm_spatial_w[o], cm_spatial_b[o], CM_CHUNK)
            hp = hp + h_cm @ w_out_odd[o]
            h_cm, v_rows = chunk_mlp(rmsnorm(hs, norm_mix_g[layer]), w_in_odd[o], cm_ln_g[o], cm_ln_b[o],
                                     cm_spatial_w[o], cm_spatial_b[o], seq_s)
            hs = hs + h_cm @ w_out_odd[o]
            cmv_l.append(v_rows)
        f_p, fbuf_p = conv_ffn(rmsnorm(hp, norm_ffn_g[layer]), ffn_w_in[layer], ffn_conv_w[layer],
                               ffn_conv_b[layer], ffn_w_out[layer],
                               jnp.zeros((n_pb, FFN_CONV - 1, D_FF), hp.dtype))
        hp = hp + f_p
        f_s, fbuf_s = conv_ffn(rmsnorm(hs, norm_ffn_g[layer]), ffn_w_in[layer], ffn_conv_w[layer],
                               ffn_conv_b[layer], ffn_w_out[layer], state_ffn_conv[layer])
        hs = hs + f_s
        fp_l.append(fbuf_p)
        fs_l.append(fbuf_s)
    y_prompt = rmsnorm(hp, norm_final_g)
    y_sample = rmsnorm(hs, norm_final_g)
    new_k_prompt = jnp.stack(kp_l)
    new_v_prompt = jnp.stack(vp_l)
    new_k_sample = jnp.stack(ks_l)
    new_v_sample = jnp.stack(vs_l)
    new_c_prompt = jnp.stack(cp_l)
    new_n_prompt = jnp.stack(np_l)
    new_m_prompt = jnp.stack(mp_l)
    new_mconv_prompt = jnp.stack(bp_l)
    new_c_sample = jnp.stack(cs_l)
    new_n_sample = jnp.stack(ns_l)
    new_m_sample = jnp.stack(ms_l)
    new_mconv_sample = jnp.stack(bs_l)
    new_cm_v_sample = jnp.stack(cmv_l)
    new_ffn_conv_prompt = jnp.stack(fp_l)
    new_ffn_conv_sample = jnp.stack(fs_l)
    return (y_prompt, y_sample, new_k_prompt, new_v_prompt, new_k_sample, new_v_sample,
            new_c_prompt, new_n_prompt, new_m_prompt, new_mconv_prompt,
            new_c_sample, new_n_sample, new_m_sample, new_mconv_sample,
            new_cm_v_sample, new_ffn_conv_prompt, new_ffn_conv_sample)
```

```python
import functools

import jax
import jax.numpy as jnp
from jax import lax
from jax.experimental import pallas as pl
from jax.experimental.pallas import tpu as pltpu

F32 = jnp.float32
BF16 = jnp.bfloat16
EPS = 1e-6

ML_HEADS = 4
ML_DH = 128
ML_WIDTH = ML_HEADS * ML_DH
ML_CONV = 4
ML_CHUNK = 128
SB_HEADS = 8
SB_DH = 64
SB_WIDTH = SB_HEADS * SB_DH
SB_SCALE = SB_DH ** -0.5
CM_GROUPS = 8
CM_CHUNK = 128
FFN_CONV = 3
LANES = 128
SUBLANES = 8
MXU_N = 256
VMEM_LIMIT = 56 * 1024 * 1024


def _mm(a, b):
    return jnp.dot(a, b, preferred_element_type=F32)


def _mm_nt(a, b):
    return lax.dot_general(a, b, (((1,), (1,)), ((), ())), preferred_element_type=F32)


def _rms(x, g):
    return x * lax.rsqrt(jnp.mean(x * x, axis=-1, keepdims=True) + EPS) * g


def _sigmoid(x):
    return 1.0 / (1.0 + jnp.exp(-x))


def _softplus(x):
    return jnp.maximum(x, 0.0) + jnp.log1p(jnp.exp(-jnp.abs(x)))


def _gelu(x):
    c = 0.7978845608028654
    return 0.5 * x * (1.0 + jnp.tanh(c * (x + 0.044715 * (x * x * x))))


def _split_hi_lo(x):
    hi = x.astype(BF16)
    lo = (x - hi.astype(F32)).astype(BF16)
    return hi, lo


def _params(n_axes):
    return pltpu.CompilerParams(dimension_semantics=("arbitrary",) * n_axes,
                                vmem_limit_bytes=VMEM_LIMIT)


def _const_spec(shape):
    nd = len(shape)
    return pl.BlockSpec(shape, lambda *_: (0,) * nd, pipeline_mode=pl.Buffered(1))


_C_QK = 0
_C_V = 2 * ML_WIDTH
_C_O = _C_V + ML_WIDTH
_C_SQ = _C_O + ML_WIDTH
_C_SK = _C_SQ + SB_WIDTH
_C_SV = _C_SK + SB_WIDTH
_C_G = _C_SV + SB_WIDTH
_C_END = _C_G + LANES


def _even_in_kernel(x_ref, g_ref, w_ref, wkvt_ref, *out_refs, head_major):
    xn = _rms(x_ref[...], g_ref[...]).astype(BF16)
    qk_ref, v_ref, o_ref, gates_ref, sq_ref, sk_ref, sv_ref = out_refs
    qk_ref[...] = _mm(xn, w_ref[:, _C_QK:_C_V])
    v_ref[...] = _mm(xn, w_ref[:, _C_V:_C_O])
    o_ref[...] = _mm(xn, w_ref[:, _C_O:_C_SQ])
    gates_ref[...] = _mm(xn, w_ref[:, _C_G:_C_END])
    sq = _mm(xn, w_ref[:, _C_SQ:_C_SK]) * SB_SCALE
    if head_major:
        tm = sq.shape[0]
        for h in range(SB_HEADS):
            sq_ref[0, h] = sq[:, h * SB_DH:(h + 1) * SB_DH].astype(BF16)
        sk_ref[0] = _mm_nt(wkvt_ref[0:SB_WIDTH, :], xn).reshape(SB_HEADS, SB_DH, tm)
        sv_ref[0] = _mm_nt(wkvt_ref[SB_WIDTH:, :], xn).reshape(SB_HEADS, SB_DH, tm)
    else:
        sq_ref[...] = sq
        sk_ref[...] = _mm(xn, w_ref[:, _C_SK:_C_SV])
        sv_ref[...] = _mm(xn, w_ref[:, _C_SV:_C_G])


def _even_in(x, g, w, wkvt, *, n_seq, seq_len, tm, head_major):
    m, d = x.shape
    nt = m // tm
    row = lambda width: pl.BlockSpec((tm, width), lambda i: (i, 0))
    out_shape = [jax.ShapeDtypeStruct((m, 2 * ML_WIDTH), F32),
                 jax.ShapeDtypeStruct((m, ML_WIDTH), F32),
                 jax.ShapeDtypeStruct((m, ML_WIDTH), F32),
                 jax.ShapeDtypeStruct((m, LANES), F32)]
    out_specs = [row(2 * ML_WIDTH), row(ML_WIDTH), row(ML_WIDTH), row(LANES)]
    if head_major:
        tps = seq_len // tm
        out_shape += [jax.ShapeDtypeStruct((n_seq, SB_HEADS, seq_len, SB_DH), BF16)]
        out_specs += [pl.BlockSpec((1, SB_HEADS, tm, SB_DH), lambda i: (i // tps, 0, i % tps, 0))]
        out_shape += [jax.ShapeDtypeStruct((n_seq, SB_HEADS, SB_DH, seq_len), F32)] * 2
        out_specs += [pl.BlockSpec((1, SB_HEADS, SB_DH, tm), lambda i: (i // tps, 0, 0, i % tps))] * 2
    else:
        out_shape += [jax.ShapeDtypeStruct((m, SB_WIDTH), F32)] * 3
        out_specs += [row(SB_WIDTH)] * 3
    return pl.pallas_call(
        functools.partial(_even_in_kernel, head_major=head_major),
        grid=(nt,),
        in_specs=[row(d), _const_spec((1, d)), _const_spec(w.shape), _const_spec(wkvt.shape)],
        out_specs=out_specs, out_shape=out_shape,
        compiler_params=_params(1), name="even_in",
    )(x, g, w, wkvt)


def _log_sigmoid(x):
    return jnp.minimum(x, 0.0) - jnp.log1p(jnp.exp(-jnp.abs(x)))


def _mlstm_head(q, k, v, o, hg, bc, br, li_r, m_rows, cq, nq, mask):
    dmat = jnp.where(mask, bc - br + li_r, -jnp.inf)
    inter = bc + m_rows
    m_t = jnp.maximum(inter, jnp.max(dmat, axis=-1, keepdims=True))
    w_inter = jnp.exp(inter - m_t)
    s = _mm_nt(q.astype(BF16), k.astype(BF16)) * jnp.exp(dmat - m_t)
    num = w_inter * cq + _mm(s.astype(BF16), v.astype(BF16))
    den = w_inter * nq + jnp.sum(s, axis=-1, keepdims=True)
    hh = num / jnp.maximum(jnp.abs(den), jnp.exp(-m_t))
    ho = _sigmoid(o) * hh
    return ho * lax.rsqrt(jnp.mean(ho * ho, axis=-1, keepdims=True) + EPS) * hg


def _mlstm_prompt_kernel(qk_ref, v_ref, o_ref, gates_ref, cw_ref, cb_ref, gb_ref, hg_ref,
                         h_ref, c_out, n_out, m_out, tail_out,
                         xp_ref, c_ref, n_ref, m_ref, *, n_chunks):
    ci = pl.program_id(1)
    L = ML_CHUNK
    halo = SUBLANES

    @pl.when(ci == 0)
    def _():
        xp_ref[0:halo, :] = jnp.zeros((halo, 2 * ML_WIDTH), F32)
        c_ref[...] = jnp.zeros_like(c_ref)
        n_ref[...] = jnp.zeros_like(n_ref)
        m_ref[...] = jnp.zeros_like(m_ref)

    xp_ref[halo:halo + L, :] = qk_ref[...]
    y = cb_ref[...]
    for i in range(ML_CONV):
        off = halo - (ML_CONV - 1) + i
        y = y + xp_ref[off:off + L, :] * cw_ref[i:i + 1, :]
    xp_ref[0:halo, :] = qk_ref[L - halo:L, :]
    qk = y * _sigmoid(y)

    gl = gates_ref[...] + gb_ref[...]
    logf = _log_sigmoid(gl)
    rows = lax.broadcasted_iota(jnp.int32, (L, L), 0)
    cols = lax.broadcasted_iota(jnp.int32, (L, L), 1)
    causal = rows >= cols
    tri = jnp.where(causal, 1.0, 0.0).astype(BF16)
    bcol = _cumsum_rows(tri, logf)
    bT = bcol.T
    glT = gl.T

    outs = []
    for h in range(ML_HEADS):
        sl = slice(h * ML_DH, (h + 1) * ML_DH)
        q = qk[:, sl]
        k = qk[:, ML_WIDTH + h * ML_DH:ML_WIDTH + (h + 1) * ML_DH] * (ML_DH ** -0.5)
        v = v_ref[:, sl]
        bc = bcol[:, ML_HEADS + h:ML_HEADS + h + 1]
        br = bT[ML_HEADS + h:ML_HEADS + h + 1, :]
        li_r = glT[h:h + 1, :]
        li_c = gl[:, h:h + 1]
        m_prev = m_ref[h:h + 1, 0:1]
        c_prev = c_ref[h]
        n_prev = n_ref[h:h + 1, :]
        cq = _mm_nt(q.astype(BF16), c_prev.astype(BF16))
        nq = jnp.sum(q * n_prev, axis=-1, keepdims=True)
        outs.append(_mlstm_head(q, k, v, o_ref[:, sl], hg_ref[:, sl], bc, br, li_r,
                                jnp.broadcast_to(m_prev, (L, 1)), cq, nq, causal))
        b_last = bc[L - 1:L, :]
        g = b_last - bc + li_c
        m_new = jnp.maximum(b_last + m_prev, jnp.max(g, axis=0, keepdims=True))
        a_s = jnp.exp(g - m_new)
        a_c = jnp.exp(b_last + m_prev - m_new)
        av_t = (a_s * v).T.astype(BF16)
        c_ref[h] = a_c * c_prev + _mm(av_t, k.astype(BF16))
        n_ref[h:h + 1, :] = a_c * n_prev + jnp.sum(a_s * k, axis=0, keepdims=True)
        m_ref[h:h + 1, :] = jnp.broadcast_to(m_new, (1, LANES))
    h_ref[...] = jnp.concatenate(outs, axis=-1).astype(h_ref.dtype)

    @pl.when(ci == n_chunks - 1)
    def _():
        c_out[0] = c_ref[...]
        n_out[0] = n_ref[...]
        m_out[0] = m_ref[...]
        tail_out[0] = qk_ref[L - halo:L, :]


def _cumsum_rows(tri, x):
    hi = x.astype(BF16)
    r1 = x - hi.astype(F32)
    mid = r1.astype(BF16)
    lo = (r1 - mid.astype(F32)).astype(BF16)
    return _mm(tri, hi) + _mm(tri, mid) + _mm(tri, lo)


def _mlstm_prompt(qk, v, o, gates, cw, cb, gb, hg, *, n_seq, seq_len):
    nc = seq_len // ML_CHUNK
    L = ML_CHUNK
    row = lambda width: pl.BlockSpec((L, width), lambda b, c: (b * nc + c, 0))
    per_seq = lambda *dims: pl.BlockSpec((1,) + dims, lambda b, c: (b,) + (0,) * len(dims))
    return pl.pallas_call(
        functools.partial(_mlstm_prompt_kernel, n_chunks=nc),
        grid=(n_seq, nc),
        in_specs=[row(2 * ML_WIDTH), row(ML_WIDTH), row(ML_WIDTH), row(LANES),
                  _const_spec(cw.shape), _const_spec(cb.shape), _const_spec(gb.shape), _const_spec(hg.shape)],
        out_specs=[row(ML_WIDTH), per_seq(ML_HEADS, ML_DH, ML_DH), per_seq(SUBLANES, ML_DH),
                   per_seq(SUBLANES, LANES), per_seq(SUBLANES, 2 * ML_WIDTH)],
        out_shape=[jax.ShapeDtypeStruct((n_seq * seq_len, ML_WIDTH), BF16),
                   jax.ShapeDtypeStruct((n_seq, ML_HEADS, ML_DH, ML_DH), F32),
                   jax.ShapeDtypeStruct((n_seq, SUBLANES, ML_DH), F32),
                   jax.ShapeDtypeStruct((n_seq, SUBLANES, LANES), F32),
                   jax.ShapeDtypeStruct((n_seq, SUBLANES, 2 * ML_WIDTH), F32)],
        scratch_shapes=[pltpu.VMEM((SUBLANES + L, 2 * ML_WIDTH), F32),
                        pltpu.VMEM((ML_HEADS, ML_DH, ML_DH), F32),
                        pltpu.VMEM((SUBLANES, ML_DH), F32),
                        pltpu.VMEM((SUBLANES, LANES), F32)],
        compiler_params=_params(2), name="mlstm_prompt",
    )(qk, v, o, gates, cw, cb, gb, hg)


def _mlstm_sample_kernel(qk_ref, v_ref, o_ref, gates_ref, st_ref, cw_ref, cb_ref, gb_ref, hg_ref,
                         c0_ref, n0_ref, m0_ref,
                         h_ref, c_out, n_out, m_out, ac_ref, *, n_seq, n_new):
    L = n_seq * n_new
    ext = jnp.concatenate([st_ref[...], qk_ref[...]], axis=0)
    y = cb_ref[...]
    for i in range(ML_CONV):
        y = y + ext[i * n_seq:i * n_seq + L, :] * cw_ref[i:i + 1, :]
    qk = y * _sigmoid(y)

    gl = gates_ref[...] + gb_ref[...]
    logf = _log_sigmoid(gl)
    tblk = lambda a, t: a[t * n_seq:(t + 1) * n_seq, :]
    b_t = [tblk(logf, 0)]
    for t in range(1, n_new):
        b_t.append(b_t[-1] + tblk(logf, t))
    bcol = jnp.concatenate(b_t, axis=0)
    bT = bcol.T
    glT = gl.T
    rows = lax.broadcasted_iota(jnp.int32, (L, L), 0)
    cols = lax.broadcasted_iota(jnp.int32, (L, L), 1)
    same_seq_causal = (rows >= cols) & (((rows - cols) % n_seq) == 0)
    row_seq = lax.broadcasted_iota(jnp.int32, (L, ML_DH), 0) % n_seq
    col_seq = lax.broadcasted_iota(jnp.int32, (ML_DH, L), 1) % n_seq
    m0 = m0_ref[...]
    m_out[...] = m0

    outs = []
    for h in range(ML_HEADS):
        sl = slice(h * ML_DH, (h + 1) * ML_DH)
        q = qk[:, sl]
        k = qk[:, ML_WIDTH + h * ML_DH:ML_WIDTH + (h + 1) * ML_DH] * (ML_DH ** -0.5)
        v = v_ref[:, sl]
        q_bf = q.astype(BF16)
        k_bf = k.astype(BF16)
        bc = bcol[:, ML_HEADS + h:ML_HEADS + h + 1]
        br = bT[ML_HEADS + h:ML_HEADS + h + 1, :]
        li_r = glT[h:h + 1, :]
        li_c = gl[:, h:h + 1]
        m0_h = m0[:, h:h + 1]
        m_rows = jnp.concatenate([m0_h] * n_new, axis=0)
        n0_h = n0_ref[:, sl]
        nq = jnp.sum(q * jnp.concatenate([n0_h] * n_new, axis=0), axis=-1, keepdims=True)

        b_last = tblk(bc, n_new - 1)
        g_t = [b_last - tblk(bc, t) + tblk(li_c, t) for t in range(n_new)]
        m_new = b_last + m0_h
        for t in range(n_new):
            m_new = jnp.maximum(m_new, g_t[t])
        a_s_t = [jnp.exp(g_t[t] - m_new) for t in range(n_new)]
        a_c = jnp.exp(b_last + m0_h - m_new)
        n_new_h = a_c * n0_h
        for t in range(n_new):
            n_new_h = n_new_h + a_s_t[t] * tblk(k, t)
        n_out[:, sl] = n_new_h
        m_out[:, h:h + 1] = m_new
        ac_ref[...] = jnp.broadcast_to(a_c, (n_seq, LANES))
        av_t = (jnp.concatenate(a_s_t, axis=0) * v).T

        def per_seq(b, cq):
            c_prev = c0_ref[b, h]
            part = _mm_nt(q_bf, c_prev.astype(BF16))
            cq = jnp.where(row_seq == b, part, cq)
            upd = _mm(jnp.where(col_seq == b, av_t, 0.0).astype(BF16), k_bf)
            c_out[b, h] = ac_ref[pl.ds(b, 1), :] * c_prev + upd
            return cq

        cq = lax.fori_loop(0, n_seq, per_seq, jnp.zeros((L, ML_DH), F32))
        outs.append(_mlstm_head(q, k, v, o_ref[:, sl], hg_ref[:, sl], bc, br, li_r,
                                m_rows, cq, nq, same_seq_causal))
    h_ref[...] = jnp.concatenate(outs, axis=-1).astype(h_ref.dtype)


def _mlstm_sample(qk, v, o, gates, st, cw, cb, gb, hg, c0, n0, m0, *, n_seq, n_new):
    L = n_seq * n_new
    return pl.pallas_call(
        functools.partial(_mlstm_sample_kernel, n_seq=n_seq, n_new=n_new),
        out_shape=[jax.ShapeDtypeStruct((L, ML_WIDTH), BF16),
                   jax.ShapeDtypeStruct(c0.shape, F32),
                   jax.ShapeDtypeStruct(n0.shape, F32),
                   jax.ShapeDtypeStruct(m0.shape, F32)],
        scratch_shapes=[pltpu.VMEM((n_seq, LANES), F32)],
        compiler_params=pltpu.CompilerParams(vmem_limit_bytes=VMEM_LIMIT), name="mlstm_sample",
    )(qk, v, o, gates, st, cw, cb, gb, hg, c0, n0, m0)


def _sb_tile(z, vt_bf, upper, rest, mask):
    sp = _softplus(z)
    lsm = -sp
    if mask is not None:
        lsm = jnp.where(mask, lsm, 0.0)
    hi, lo = _split_hi_lo(lsm)
    cs = _mm(hi, upper) + _mm(lo, upper)
    a = jnp.exp(z - sp + cs + rest)
    if mask is not None:
        a = jnp.where(mask, a, 0.0)
    return _mm_nt(a.astype(BF16), vt_bf), rest + cs[:, 0:1] + lsm[:, 0:1]


def _sb_prompt_kernel(bias_ref, q_ref, kt_ref, vt_ref, o_ref, kb_ref, vb_ref, *, blk):
    hp = pl.program_id(1)
    qi = pl.program_id(2)
    n_blk = kb_ref.shape[1]

    @pl.when(qi == 0)
    def _():
        for hh in range(2):
            for j in range(n_blk):
                kb_ref[hh, j] = kt_ref[0, hh, :, j * blk:(j + 1) * blk].astype(BF16)
                vb_ref[hh, j] = vt_ref[0, hh, :, j * blk:(j + 1) * blk].astype(BF16)

    rows = lax.broadcasted_iota(jnp.int32, (blk, blk), 0)
    cols = lax.broadcasted_iota(jnp.int32, (blk, blk), 1)
    upper = jnp.where(rows > cols, 1.0, 0.0).astype(BF16)
    strict_causal = cols < rows
    outs = []
    for hh in range(2):
        bias = bias_ref[hp * 2 + hh]
        q = q_ref[0, hh]

        def tile(kb, mask, acc, rest):
            pv, rest = _sb_tile(_mm(q, kb_ref[hh, kb]) + bias, vb_ref[hh, kb], upper, rest, mask)
            return acc + pv, rest

        acc, rest = tile(qi, strict_causal, jnp.zeros((blk, SB_DH), F32), jnp.zeros((blk, 1), F32))
        acc, rest = lax.fori_loop(0, qi, lambda i, c: tile(qi - 1 - i, None, *c), (acc, rest))
        outs.append(acc)
    o_ref[...] = jnp.concatenate(outs, axis=-1).astype(o_ref.dtype)


def _sb_prompt(bias, qh, kt, vt, *, blk):
    n_seq, n_heads, seq_len, dh = qh.shape
    nq = seq_len // blk
    kv_spec = pl.BlockSpec((1, 2, dh, seq_len), lambda b, hp, qi: (b, hp, 0, 0))
    return pl.pallas_call(
        functools.partial(_sb_prompt_kernel, blk=blk),
        grid=(n_seq, n_heads // 2, nq),
        in_specs=[pl.BlockSpec(memory_space=pltpu.SMEM),
                  pl.BlockSpec((1, 2, blk, dh), lambda b, hp, qi: (b, hp, qi, 0)),
                  kv_spec, kv_spec],
        out_specs=pl.BlockSpec((blk, 2 * dh), lambda b, hp, qi: (b * nq + qi, hp)),
        out_shape=jax.ShapeDtypeStruct((n_seq * seq_len, n_heads * dh), BF16),
        scratch_shapes=[pltpu.VMEM((2, nq, dh, blk), BF16), pltpu.VMEM((2, nq, dh, blk), BF16)],
        compiler_params=_params(3), name="sb_prompt",
    )(bias, qh, kt, vt)


def _sb_sample_kernel(pt_ref, q_ref, bias_ref, kn_ref, vn_ref, *refs, pages_per_step, n_new):
    del pt_ref
    k_refs = refs[:pages_per_step]
    v_refs = refs[pages_per_step:2 * pages_per_step]
    o_ref, acc_ref, rest_ref = refs[2 * pages_per_step:]
    j = pl.program_id(1)
    n_rows = n_new * SB_HEADS
    page = kn_ref.shape[1]
    rows = lax.broadcasted_iota(jnp.int32, (page, page), 0)
    cols = lax.broadcasted_iota(jnp.int32, (page, page), 1)
    upper = jnp.where(rows > cols, 1.0, 0.0).astype(BF16)
    head_of_col = lax.broadcasted_iota(jnp.int32, (n_rows, SB_WIDTH), 1) // SB_DH
    head_of_row = lax.broadcasted_iota(jnp.int32, (n_rows, SB_WIDTH), 0) % SB_HEADS
    own_head = head_of_col == head_of_row
    q = jnp.where(own_head, q_ref[0], 0.0).astype(BF16)
    bias = bias_ref[...]

    def visit(kt_bf, vt_bf, mask):
        pv, rest = _sb_tile(_mm(q, kt_bf) + bias, vt_bf, upper, rest_ref[:, 0:1], mask)
        acc_ref[...] += pv
        rest_ref[...] = jnp.broadcast_to(rest, rest_ref.shape)

    @pl.when(j == 0)
    def _():
        acc_ref[...] = jnp.zeros_like(acc_ref)
        rest_ref[...] = jnp.zeros_like(rest_ref)
        t_of_row = lax.broadcasted_iota(jnp.int32, (n_rows, page), 0) // SB_HEADS
        s_of_col = lax.broadcasted_iota(jnp.int32, (n_rows, page), 1)
        visit(kn_ref[0].T.astype(BF16), vn_ref[0].T.astype(BF16), s_of_col < t_of_row)

    for i in range(pages_per_step):
        visit(k_refs[i][0, 0].reshape(SB_WIDTH, page).astype(BF16),
              v_refs[i][0, 0].reshape(SB_WIDTH, page).astype(BF16), None)

    @pl.when(j == pl.num_programs(1) - 1)
    def _():
        acc = jnp.where(own_head, acc_ref[...], 0.0)
        for t in range(n_new):
            o_ref[0, t:t + 1, :] = jnp.sum(acc[t * SB_HEADS:(t + 1) * SB_HEADS, :], axis=0, keepdims=True)


def _sb_sample(page_table, q_rows, bias_rows, k_new, v_new, cache_kt, cache_vt, *, layer, pages_per_step, n_new):
    n_seq, n_pages = page_table.shape
    page = cache_kt.shape[4]
    n_rows = n_new * SB_HEADS
    steps = n_pages // pages_per_step

    def page_spec(i):
        return pl.BlockSpec((1, 1, SB_HEADS, SB_DH, page),
                            lambda b, j, pt: (layer, pt[b, n_pages - 1 - (j * pages_per_step + i)], 0, 0, 0))

    per_seq = lambda r, c: pl.BlockSpec((1, r, c), lambda b, j, pt: (b, 0, 0))
    grid_spec = pltpu.PrefetchScalarGridSpec(
        num_scalar_prefetch=1, grid=(n_seq, steps),
        in_specs=[per_seq(n_rows, SB_WIDTH),
                  pl.BlockSpec((n_rows, 1), lambda b, j, pt: (0, 0)),
                  per_seq(page, SB_WIDTH), per_seq(page, SB_WIDTH)]
                 + [page_spec(i) for i in range(pages_per_step)] * 2,
        out_specs=per_seq(n_new, SB_WIDTH),
        scratch_shapes=[pltpu.VMEM((n_rows, SB_WIDTH), F32), pltpu.VMEM((n_rows, LANES), F32)])
    return pl.pallas_call(
        functools.partial(_sb_sample_kernel, pages_per_step=pages_per_step, n_new=n_new),
        grid_spec=grid_spec,
        out_shape=jax.ShapeDtypeStruct((n_seq, n_new, SB_WIDTH), F32),
        compiler_params=_params(2), name="sb_sample",
    )(page_table, q_rows, bias_rows, k_new, v_new,
      *([cache_kt] * pages_per_step), *([cache_vt] * pages_per_step))


def _proj_res_kernel(x_ref, a_ref, b_ref, w_ref, o_ref):
    ka = a_ref.shape[1]
    o_ref[...] = (x_ref[...] + _mm(a_ref[...].astype(BF16), w_ref[0:ka, :])
                  + _mm(b_ref[...].astype(BF16), w_ref[ka:, :]))


def _proj_res(x, a, b, w, *, tm):
    m, d = x.shape
    row = lambda width: pl.BlockSpec((tm, width), lambda i: (i, 0))
    return pl.pallas_call(
        _proj_res_kernel, grid=(m // tm,),
        in_specs=[row(d), row(a.shape[1]), row(b.shape[1]), _const_spec(w.shape)],
        out_specs=row(d), out_shape=jax.ShapeDtypeStruct((m, d), F32),
        compiler_params=_params(1), name="proj_res",
    )(x, a, b, w)


def _ffn_chunks(xn, win_ref, wout_ref, cw_ref, cb_ref, o_ref, conv_fn, d_ff):
    ck = MXU_N
    for c in range(d_ff // ck):
        sl = slice(c * ck, (c + 1) * ck)
        g = _mm(xn, win_ref[:, sl])
        u = _mm(xn, win_ref[:, d_ff + c * ck:d_ff + (c + 1) * ck])
        g_m2, g_m1 = conv_fn(g, sl)
        y = cb_ref[:, sl] + g_m2 * cw_ref[0:1, sl] + g_m1 * cw_ref[1:2, sl] + g * cw_ref[2:3, sl]
        hmid = (_gelu(y) * u).astype(BF16)
        o_ref[...] += _mm(hmid, wout_ref[sl, :])


def _ffn_prompt_kernel(x_ref, g_ref, win_ref, cw_ref, cb_ref, wout_ref, fg_ref, o_ref, tail_out,
                       xn_ref, prev_ref, st_ref, *, tiles_per_seq, d_ff, final_norm):
    i = pl.program_id(0)
    tm = x_ref.shape[0]
    halo = SUBLANES

    @pl.when(i % tiles_per_seq == 0)
    def _():
        prev_ref[...] = jnp.zeros_like(prev_ref)

    x = x_ref[...]
    xn_ref[...] = _rms(x, g_ref[...]).astype(BF16)
    o_ref[...] = x

    def conv_fn(g, sl):
        st_ref[0:halo, :] = prev_ref[:, sl]
        st_ref[halo:halo + tm, :] = g
        prev_ref[:, sl] = g[tm - halo:tm, :]
        return st_ref[halo - 2:halo - 2 + tm, :], st_ref[halo - 1:halo - 1 + tm, :]

    _ffn_chunks(xn_ref[...], win_ref, wout_ref, cw_ref, cb_ref, o_ref, conv_fn, d_ff)
    if final_norm:
        o_ref[...] = _rms(o_ref[...], fg_ref[...])

    @pl.when(i % tiles_per_seq == tiles_per_seq - 1)
    def _():
        tail_out[0] = prev_ref[...]


def _ffn_prompt(x, g, win, cw, cb, wout, fg, *, n_seq, seq_len, tm, final_norm):
    m, d = x.shape
    d_ff = wout.shape[0]
    tps = seq_len // tm
    row = pl.BlockSpec((tm, d), lambda i: (i, 0))
    return pl.pallas_call(
        functools.partial(_ffn_prompt_kernel, tiles_per_seq=tps, d_ff=d_ff, final_norm=final_norm),
        grid=(m // tm,),
        in_specs=[row, _const_spec(g.shape), _const_spec(win.shape), _const_spec(cw.shape),
                  _const_spec(cb.shape), _const_spec(wout.shape), _const_spec(fg.shape)],
        out_specs=[row, pl.BlockSpec((1, SUBLANES, d_ff), lambda i: (i // tps, 0, 0))],
        out_shape=[jax.ShapeDtypeStruct((m, d), F32), jax.ShapeDtypeStruct((n_seq, SUBLANES, d_ff), F32)],
        scratch_shapes=[pltpu.VMEM((tm, d), BF16), pltpu.VMEM((SUBLANES, d_ff), F32),
                        pltpu.VMEM((SUBLANES + tm, MXU_N), F32)],
        compiler_params=_params(1), name="ffn_prompt",
    )(x, g, win, cw, cb, wout, fg)


def _ffn_sample_kernel(x_ref, g_ref, win_ref, cw_ref, cb_ref, wout_ref, fg_ref, st_ref, o_ref, tail_out,
                       *, n_seq, d_ff, final_norm):
    rows = x_ref.shape[0]
    x = x_ref[...]
    xn = _rms(x, g_ref[...]).astype(BF16)
    o_ref[...] = x

    def conv_fn(g, sl):
        ext = jnp.concatenate([st_ref[:, sl], g], axis=0)
        tail_out[:, sl] = g[rows - 2 * n_seq:rows, :]
        return ext[0:rows, :], ext[n_seq:n_seq + rows, :]

    _ffn_chunks(xn, win_ref, wout_ref, cw_ref, cb_ref, o_ref, conv_fn, d_ff)
    if final_norm:
        o_ref[...] = _rms(o_ref[...], fg_ref[...])


def _ffn_sample(x, g, win, cw, cb, wout, fg, st, *, n_seq, final_norm):
    d_ff = wout.shape[0]
    return pl.pallas_call(
        functools.partial(_ffn_sample_kernel, n_seq=n_seq, d_ff=d_ff, final_norm=final_norm),
        out_shape=[jax.ShapeDtypeStruct(x.shape, F32), jax.ShapeDtypeStruct(st.shape, F32)],
        compiler_params=pltpu.CompilerParams(vmem_limit_bytes=VMEM_LIMIT), name="ffn_sample",
    )(x, g, win, cw, cb, wout, fg, st)


def _layernorm_stats(v_ref, width, n_groups):
    gw = width // n_groups
    s1 = 0.0
    for gi in range(n_groups):
        s1 = s1 + jnp.sum(v_ref[:, gi * gw:(gi + 1) * gw], axis=-1, keepdims=True)
    mean = s1 / width
    s2 = 0.0
    for gi in range(n_groups):
        xc = v_ref[:, gi * gw:(gi + 1) * gw] - mean
        s2 = s2 + jnp.sum(xc * xc, axis=-1, keepdims=True)
    return mean, lax.rsqrt(s2 / width + EPS)


def _cm_prompt_kernel(x_ref, g_ref, win_ref, lng_ref, lnb_ref, ws_ref, bst_ref, wout_ref, o_ref,
                      xn_ref, v_ref, *, width):
    tm = x_ref.shape[0]
    gw = width // CM_GROUPS
    x = x_ref[...]
    xn_ref[...] = _rms(x, g_ref[...]).astype(BF16)
    o_ref[...] = x
    for gi in range(CM_GROUPS):
        v_ref[:, gi * gw:(gi + 1) * gw] = _gelu(_mm(xn_ref[...], win_ref[:, width + gi * gw:width + (gi + 1) * gw]))
    mean, rstd = _layernorm_stats(v_ref, width, CM_GROUPS)
    rows = lax.broadcasted_iota(jnp.int32, (CM_CHUNK, CM_CHUNK), 0)
    cols = lax.broadcasted_iota(jnp.int32, (CM_CHUNK, CM_CHUNK), 1)
    tril = rows >= cols
    for gi in range(CM_GROUPS):
        sl = slice(gi * gw, (gi + 1) * gw)
        vn = ((v_ref[:, sl] - mean) * rstd * lng_ref[:, sl] + lnb_ref[:, sl]).astype(BF16)
        wsg = jnp.where(tril, ws_ref[gi], 0.0).astype(BF16)
        bias = bst_ref[:, gi:gi + 1]
        mixed = jnp.concatenate(
            [_mm(wsg, vn[c * CM_CHUNK:(c + 1) * CM_CHUNK, :]) + bias for c in range(tm // CM_CHUNK)], axis=0)
        u = _gelu(_mm(xn_ref[...], win_ref[:, sl]))
        o_ref[...] += _mm((u * mixed).astype(BF16), wout_ref[sl, :])


def _cm_prompt(x, g, win, lng, lnb, ws, bst, wout, *, tm):
    m, d = x.shape
    width = wout.shape[0]
    row = pl.BlockSpec((tm, d), lambda i: (i, 0))
    return pl.pallas_call(
        functools.partial(_cm_prompt_kernel, width=width),
        grid=(m // tm,),
        in_specs=[row] + [_const_spec(a.shape) for a in (g, win, lng, lnb, ws, bst, wout)],
        out_specs=row, out_shape=jax.ShapeDtypeStruct((m, d), F32),
        scratch_shapes=[pltpu.VMEM((tm, d), BF16), pltpu.VMEM((tm, width), F32)],
        compiler_params=_params(1), name="cm_prompt",
    )(x, g, win, lng, lnb, ws, bst, wout)


def _cm_sample_kernel(x_ref, g_ref, win_ref, lng_ref, lnb_ref, wexp_ref, bexp_ref, wout_ref, o_ref, v_out,
                      *, n_seq, n_new, width):
    gw = width // CM_GROUPS
    x = x_ref[...]
    xn = _rms(x, g_ref[...]).astype(BF16)
    for gi in range(CM_GROUPS):
        v_out[:, gi * gw:(gi + 1) * gw] = _gelu(_mm(xn, win_ref[:, width + gi * gw:width + (gi + 1) * gw]))
    mean, rstd = _layernorm_stats(v_out, width, CM_GROUPS)
    acc = x
    for gi in range(CM_GROUPS):
        sl = slice(gi * gw, (gi + 1) * gw)
        vn = (v_out[:, sl] - mean) * rstd * lng_ref[:, sl] + lnb_ref[:, sl]
        v_out[:, sl] = vn
        mixed = []
        for t in range(n_new):
            mt = bexp_ref[t:t + 1, sl]
            for s in range(t + 1):
                mt = mt + wexp_ref[t * n_new + s:t * n_new + s + 1, sl] * vn[s * n_seq:(s + 1) * n_seq, :]
            mixed.append(mt)
        u = _gelu(_mm(xn, win_ref[:, sl]))
        acc = acc + _mm((u * jnp.concatenate(mixed, axis=0)).astype(BF16), wout_ref[sl, :])
    o_ref[...] = acc


def _cm_sample(x, g, win, lng, lnb, wexp, bexp, wout, *, n_seq, n_new):
    width = wout.shape[0]
    return pl.pallas_call(
        functools.partial(_cm_sample_kernel, n_seq=n_seq, n_new=n_new, width=width),
        out_shape=[jax.ShapeDtypeStruct(x.shape, F32), jax.ShapeDtypeStruct((x.shape[0], width), F32)],
        compiler_params=pltpu.CompilerParams(vmem_limit_bytes=VMEM_LIMIT), name="cm_sample",
    )(x, g, win, lng, lnb, wexp, bexp, wout)


def _time_major(a):
    return jnp.swapaxes(a, 0, 1).reshape((a.shape[0] * a.shape[1],) + a.shape[2:])


def _batch_major(a, n_seq):
    return jnp.swapaxes(a.reshape((a.shape[0] // n_seq, n_seq) + a.shape[1:]), 0, 1)


def _pad_lanes(a):
    return jnp.pad(a, [(0, 0)] * (a.ndim - 1) + [(0, LANES - a.shape[-1])])


def kernel(x_prompt, x_sample, cache_k, cache_v, page_table, state_mlstm_c, state_mlstm_n, state_mlstm_m, state_mlstm_conv, state_ffn_conv, norm_mix_g, norm_ffn_g, norm_final_g, w_in_even, ml_conv_w, ml_conv_b, ml_gate_b, ml_head_g, sb_logit_b, w_out_even, w_in_odd, cm_ln_g, cm_ln_b, cm_spatial_w, cm_spatial_b, w_out_odd, ffn_w_in, ffn_conv_w, ffn_conv_b, ffn_w_out):
    n_pb, seq_p, d = x_prompt.shape
    n_sb, seq_s, _ = x_sample.shape
    depth = norm_mix_g.shape[0]
    d_ff = ffn_w_out.shape[1]
    hp = x_prompt.reshape(n_pb * seq_p, d)
    hs = _time_major(x_sample)
    fg = norm_final_g.reshape(1, d)

    kp_l, vp_l, ks_l, vs_l = [], [], [], []
    cp_l, np_l, mp_l, bp_l = [], [], [], []
    cs_l, ns_l, ms_l, bs_l = [], [], [], []
    cmv_l, fp_l, fs_l = [], [], []
    for layer in range(depth):
        mix_g = norm_mix_g[layer].reshape(1, d)
        if layer % 2 == 0:
            e = layer // 2
            w = w_in_even[e]
            gate_lo = _C_O + ML_WIDTH
            gate_hi = gate_lo + 2 * ML_HEADS
            w_all = jnp.concatenate([w[:, :gate_lo], w[:, gate_hi:], _pad_lanes(w[:, gate_lo:gate_hi])],
                                    axis=1).astype(BF16)
            w_kvt = w[:, gate_hi + SB_WIDTH:].T.astype(BF16)
            cw, cb = ml_conv_w[e], ml_conv_b[e].reshape(1, -1)
            gb = _pad_lanes(ml_gate_b[e].reshape(1, -1))
            hg = ml_head_g[e].reshape(1, -1)
            w_out = w_out_even[e].astype(BF16)
            qk, vm, om, gates, qh, kt, vt = _even_in(
                hp, mix_g, w_all, w_kvt, n_seq=n_pb, seq_len=seq_p, tm=512, head_major=True)
            h_ml, c_p, n_p, m_p, tail_p = _mlstm_prompt(qk, vm, om, gates, cw, cb, gb, hg, n_seq=n_pb, seq_len=seq_p)
            h_sb = _sb_prompt(sb_logit_b[e], qh, kt, vt, blk=256)
            hp = _proj_res(hp, h_ml, h_sb, w_out, tm=512)
            kp_l.append(jnp.transpose(kt, (0, 3, 1, 2)))
            vp_l.append(jnp.transpose(vt, (0, 3, 1, 2)))
            cp_l.append(c_p)
            np_l.append(n_p[:, :ML_HEADS])
            mp_l.append(m_p[:, :ML_HEADS, 0])
            bp_l.append(tail_p[:, SUBLANES - (ML_CONV - 1):])
            qk, vm, om, gates, sq, sk, sv = _even_in(
                hs, mix_g, w_all, w_kvt, n_seq=n_sb, seq_len=seq_s, tm=n_sb * seq_s, head_major=False)
            h_ml, c_s, n_s, m_s = _mlstm_sample(
                qk, vm, om, gates, _time_major(state_mlstm_conv[e]), cw, cb, gb, hg,
                state_mlstm_c[e], state_mlstm_n[e].reshape(n_sb, ML_WIDTH), _pad_lanes(state_mlstm_m[e]),
                n_seq=n_sb, n_new=seq_s)
            page = cache_k.shape[2]
            pad_keys = lambda a: jnp.pad(_batch_major(a, n_sb), ((0, 0), (0, page - seq_s), (0, 0)))
            q_rows = jnp.broadcast_to(_batch_major(sq, n_sb)[:, :, None, :],
                                      (n_sb, seq_s, SB_HEADS, SB_WIDTH)).reshape(n_sb, seq_s * SB_HEADS, SB_WIDTH)
            bias_rows = jnp.tile(sb_logit_b[e], seq_s).reshape(seq_s * SB_HEADS, 1)
            pool_view = lambda c: jnp.transpose(c, (0, 1, 3, 4, 2))
            h_sb = _sb_sample(page_table, q_rows, bias_rows, pad_keys(sk), pad_keys(sv),
                              pool_view(cache_k), pool_view(cache_v), layer=e, pages_per_step=8, n_new=seq_s)
            hs = _proj_res(hs, h_ml, _time_major(h_sb), w_out, tm=n_sb * seq_s)
            ks_l.append(_batch_major(sk, n_sb).reshape(n_sb, seq_s, SB_HEADS, SB_DH))
            vs_l.append(_batch_major(sv, n_sb).reshape(n_sb, seq_s, SB_HEADS, SB_DH))
            cs_l.append(c_s)
            ns_l.append(n_s.reshape(n_sb, ML_HEADS, ML_DH))
            ms_l.append(m_s[:, :ML_HEADS])
            bs_l.append(_batch_major(qk[(seq_s - (ML_CONV - 1)) * n_sb:], n_sb))
        else:
            o = layer // 2
            win = w_in_odd[o].astype(BF16)
            wout = w_out_odd[o].astype(BF16)
            lng, lnb = cm_ln_g[o].reshape(1, -1), cm_ln_b[o].reshape(1, -1)
            width = wout.shape[0]
            gw = width // CM_GROUPS
            hp = _cm_prompt(hp, mix_g, win, lng, lnb, cm_spatial_w[o], cm_spatial_b[o].T, wout, tm=512)
            wexp = jnp.repeat(cm_spatial_w[o][:, :seq_s, :seq_s].reshape(CM_GROUPS, seq_s * seq_s).T, gw, axis=1)
            bexp = jnp.repeat(cm_spatial_b[o][:, :seq_s].T, gw, axis=1)
            hs, v_rows = _cm_sample(hs, mix_g, win, lng, lnb, wexp, bexp, wout, n_seq=n_sb, n_new=seq_s)
            cmv_l.append(_batch_major(v_rows, n_sb))
        ffn_g = norm_ffn_g[layer].reshape(1, d)
        win = ffn_w_in[layer].astype(BF16)
        wout = ffn_w_out[layer].astype(BF16)
        cw, cb = ffn_conv_w[layer], ffn_conv_b[layer].reshape(1, -1)
        last = layer == depth - 1
        hp, tail_p = _ffn_prompt(hp, ffn_g, win, cw, cb, wout, fg, n_seq=n_pb, seq_len=seq_p, tm=1024,
                                 final_norm=last)
        hs, tail_s = _ffn_sample(hs, ffn_g, win, cw, cb, wout, fg, _time_major(state_ffn_conv[layer]),
                                 n_seq=n_sb, final_norm=last)
        fp_l.append(tail_p[:, SUBLANES - (FFN_CONV - 1):])
        fs_l.append(_batch_major(tail_s, n_sb))

    return (hp.reshape(n_pb, seq_p, d), _batch_major(hs, n_sb),
            jnp.stack(kp_l), jnp.stack(vp_l), jnp.stack(ks_l), jnp.stack(vs_l),
            jnp.stack(cp_l), jnp.stack(np_l), jnp.stack(mp_l), jnp.stack(bp_l),
            jnp.stack(cs_l), jnp.stack(ns_l), jnp.stack(ms_l), jnp.stack(bs_l),
            jnp.stack(cmv_l), jnp.stack(fp_l), jnp.stack(fs_l))
```

```python
import functools

import jax
import jax.numpy as jnp
from jax import lax
from jax.experimental import pallas as pl
from jax.experimental.pallas import tpu as pltpu

F32 = jnp.float32
BF16 = jnp.bfloat16
EPS = 1e-6
LOG2E = 1.4426950408889634

ML_HEADS = 4
ML_DH = 128
ML_WIDTH = ML_HEADS * ML_DH
ML_CONV = 4
ML_CHUNK = 128
SB_HEADS = 8
SB_DH = 64
SB_WIDTH = SB_HEADS * SB_DH
SB_SCALE = SB_DH ** -0.5
CM_GROUPS = 8
CM_CHUNK = 128
FFN_CONV = 3
LANES = 128
SUBLANES = 8
MXU_N = 256
VMEM_LIMIT = 56 * 1024 * 1024


def _mm(a, b):
    return jnp.dot(a, b, preferred_element_type=F32)


def _mm_nt(a, b):
    return lax.dot_general(a, b, (((1,), (1,)), ((), ())), preferred_element_type=F32)


def _rms(x, g):
    return x * lax.rsqrt(jnp.mean(x * x, axis=-1, keepdims=True) + EPS) * g


def _sigmoid(x):
    return 1.0 / (1.0 + jnp.exp(-x))


def _softplus(x):
    return jnp.maximum(x, 0.0) + jnp.log1p(jnp.exp(-jnp.abs(x)))


def _gelu(x):
    c = 0.7978845608028654
    return 0.5 * x * (1.0 + jnp.tanh(c * (x + 0.044715 * (x * x * x))))


def _split_hi_lo(x):
    hi = x.astype(BF16)
    lo = (x - hi.astype(F32)).astype(BF16)
    return hi, lo


def _params(n_axes):
    return pltpu.CompilerParams(dimension_semantics=("arbitrary",) * n_axes,
                                vmem_limit_bytes=VMEM_LIMIT)


def _const_spec(shape):
    nd = len(shape)
    return pl.BlockSpec(shape, lambda *_: (0,) * nd, pipeline_mode=pl.Buffered(1))


_C_QK = 0
_C_V = 2 * ML_WIDTH
_C_O = _C_V + ML_WIDTH
_C_SQ = _C_O + ML_WIDTH
_C_SK = _C_SQ + SB_WIDTH
_C_SV = _C_SK + SB_WIDTH
_C_G = _C_SV + SB_WIDTH
_C_END = _C_G + LANES


def _even_in_kernel(x_ref, g_ref, w_ref, wkvt_ref, *out_refs, head_major):
    xn = _rms(x_ref[...], g_ref[...]).astype(BF16)
    qk_ref, v_ref, o_ref, gates_ref, sq_ref, sk_ref, sv_ref = out_refs
    qk_ref[...] = _mm(xn, w_ref[:, _C_QK:_C_V])
    v_ref[...] = _mm(xn, w_ref[:, _C_V:_C_O])
    o_ref[...] = _mm(xn, w_ref[:, _C_O:_C_SQ])
    gates_ref[...] = _mm(xn, w_ref[:, _C_G:_C_END])
    sq = _mm(xn, w_ref[:, _C_SQ:_C_SK]) * SB_SCALE
    if head_major:
        tm = sq.shape[0]
        for h in range(SB_HEADS):
            sq_ref[0, h] = sq[:, h * SB_DH:(h + 1) * SB_DH].astype(BF16)
        sk_ref[0] = _mm_nt(wkvt_ref[0:SB_WIDTH, :], xn).reshape(SB_HEADS, SB_DH, tm)
        sv_ref[0] = _mm_nt(wkvt_ref[SB_WIDTH:, :], xn).reshape(SB_HEADS, SB_DH, tm)
    else:
        sq_ref[...] = sq
        sk_ref[...] = _mm(xn, w_ref[:, _C_SK:_C_SV])
        sv_ref[...] = _mm(xn, w_ref[:, _C_SV:_C_G])


def _even_in(x, g, w, wkvt, *, n_seq, seq_len, tm, head_major):
    m, d = x.shape
    nt = m // tm
    row = lambda width: pl.BlockSpec((tm, width), lambda i: (i, 0))
    out_shape = [jax.ShapeDtypeStruct((m, 2 * ML_WIDTH), F32),
                 jax.ShapeDtypeStruct((m, ML_WIDTH), F32),
                 jax.ShapeDtypeStruct((m, ML_WIDTH), F32),
                 jax.ShapeDtypeStruct((m, LANES), F32)]
    out_specs = [row(2 * ML_WIDTH), row(ML_WIDTH), row(ML_WIDTH), row(LANES)]
    if head_major:
        tps = seq_len // tm
        out_shape += [jax.ShapeDtypeStruct((n_seq, SB_HEADS, seq_len, SB_DH), BF16)]
        out_specs += [pl.BlockSpec((1, SB_HEADS, tm, SB_DH), lambda i: (i // tps, 0, i % tps, 0))]
        out_shape += [jax.ShapeDtypeStruct((n_seq, SB_HEADS, SB_DH, seq_len), F32)] * 2
        out_specs += [pl.BlockSpec((1, SB_HEADS, SB_DH, tm), lambda i: (i // tps, 0, 0, i % tps))] * 2
    else:
        out_shape += [jax.ShapeDtypeStruct((m, SB_WIDTH), F32)] * 3
        out_specs += [row(SB_WIDTH)] * 3
    return pl.pallas_call(
        functools.partial(_even_in_kernel, head_major=head_major),
        grid=(nt,),
        in_specs=[row(d), _const_spec((1, d)), _const_spec(w.shape), _const_spec(wkvt.shape)],
        out_specs=out_specs, out_shape=out_shape,
        compiler_params=_params(1), name="even_in",
    )(x, g, w, wkvt)


def _log_sigmoid(x):
    return jnp.minimum(x, 0.0) - jnp.log1p(jnp.exp(-jnp.abs(x)))


def _mlstm_head(q, k, v, o, hg, bc, br, li_r, m_rows, cq, nq, mask):
    dmat = jnp.where(mask, bc - br + li_r, -jnp.inf)
    inter = bc + m_rows
    m_t = jnp.maximum(inter, jnp.max(dmat, axis=-1, keepdims=True))
    w_inter = jnp.exp(inter - m_t)
    s = _mm_nt(q.astype(BF16), k.astype(BF16)) * jnp.exp(dmat - m_t)
    num = w_inter * cq + _mm(s.astype(BF16), v.astype(BF16))
    den = w_inter * nq + jnp.sum(s, axis=-1, keepdims=True)
    hh = num / jnp.maximum(jnp.abs(den), jnp.exp(-m_t))
    ho = _sigmoid(o) * hh
    return ho * lax.rsqrt(jnp.mean(ho * ho, axis=-1, keepdims=True) + EPS) * hg


def _mlstm_prompt_kernel(qk_ref, v_ref, o_ref, gates_ref, cw_ref, cb_ref, gb_ref, hg_ref,
                         h_ref, c_out, n_out, m_out, tail_out,
                         xp_ref, c_ref, n_ref, m_ref, *, n_chunks):
    ci = pl.program_id(1)
    L = ML_CHUNK
    halo = SUBLANES

    @pl.when(ci == 0)
    def _():
        xp_ref[0:halo, :] = jnp.zeros((halo, 2 * ML_WIDTH), F32)
        c_ref[...] = jnp.zeros_like(c_ref)
        n_ref[...] = jnp.zeros_like(n_ref)
        m_ref[...] = jnp.zeros_like(m_ref)

    xp_ref[halo:halo + L, :] = qk_ref[...]
    y = cb_ref[...]
    for i in range(ML_CONV):
        off = halo - (ML_CONV - 1) + i
        y = y + xp_ref[off:off + L, :] * cw_ref[i:i + 1, :]
    xp_ref[0:halo, :] = qk_ref[L - halo:L, :]
    qk = y * _sigmoid(y)

    gl = gates_ref[...] + gb_ref[...]
    logf = _log_sigmoid(gl)
    rows = lax.broadcasted_iota(jnp.int32, (L, L), 0)
    cols = lax.broadcasted_iota(jnp.int32, (L, L), 1)
    causal = rows >= cols
    tri = jnp.where(causal, 1.0, 0.0).astype(BF16)
    bcol = _cumsum_rows(tri, logf)
    bT = bcol.T
    glT = gl.T

    outs = []
    for h in range(ML_HEADS):
        sl = slice(h * ML_DH, (h + 1) * ML_DH)
        q = qk[:, sl]
        k = qk[:, ML_WIDTH + h * ML_DH:ML_WIDTH + (h + 1) * ML_DH] * (ML_DH ** -0.5)
        v = v_ref[:, sl]
        bc = bcol[:, ML_HEADS + h:ML_HEADS + h + 1]
        br = bT[ML_HEADS + h:ML_HEADS + h + 1, :]
        li_r = glT[h:h + 1, :]
        li_c = gl[:, h:h + 1]
        m_prev = m_ref[h:h + 1, 0:1]
        c_prev = c_ref[h]
        n_prev = n_ref[h:h + 1, :]
        cq = _mm_nt(q.astype(BF16), c_prev.astype(BF16))
        nq = jnp.sum(q * n_prev, axis=-1, keepdims=True)
        outs.append(_mlstm_head(q, k, v, o_ref[:, sl], hg_ref[:, sl], bc, br, li_r,
                                jnp.broadcast_to(m_prev, (L, 1)), cq, nq, causal))
        b_last = bc[L - 1:L, :]
        g = b_last - bc + li_c
        m_new = jnp.maximum(b_last + m_prev, jnp.max(g, axis=0, keepdims=True))
        a_s = jnp.exp(g - m_new)
        a_c = jnp.exp(b_last + m_prev - m_new)
        av_t = (a_s * v).T.astype(BF16)
        c_ref[h] = a_c * c_prev + _mm(av_t, k.astype(BF16))
        n_ref[h:h + 1, :] = a_c * n_prev + jnp.sum(a_s * k, axis=0, keepdims=True)
        m_ref[h:h + 1, :] = jnp.broadcast_to(m_new, (1, LANES))
    h_ref[...] = jnp.concatenate(outs, axis=-1).astype(h_ref.dtype)

    @pl.when(ci == n_chunks - 1)
    def _():
        c_out[0] = c_ref[...]
        n_out[0] = n_ref[...]
        m_out[0] = m_ref[...]
        tail_out[0] = qk_ref[L - halo:L, :]


def _cumsum_rows(tri, x):
    hi = x.astype(BF16)
    r1 = x - hi.astype(F32)
    mid = r1.astype(BF16)
    lo = (r1 - mid.astype(F32)).astype(BF16)
    return _mm(tri, hi) + _mm(tri, mid) + _mm(tri, lo)


def _mlstm_prompt(qk, v, o, gates, cw, cb, gb, hg, *, n_seq, seq_len):
    nc = seq_len // ML_CHUNK
    L = ML_CHUNK
    row = lambda width: pl.BlockSpec((L, width), lambda b, c: (b * nc + c, 0))
    per_seq = lambda *dims: pl.BlockSpec((1,) + dims, lambda b, c: (b,) + (0,) * len(dims))
    return pl.pallas_call(
        functools.partial(_mlstm_prompt_kernel, n_chunks=nc),
        grid=(n_seq, nc),
        in_specs=[row(2 * ML_WIDTH), row(ML_WIDTH), row(ML_WIDTH), row(LANES),
                  _const_spec(cw.shape), _const_spec(cb.shape), _const_spec(gb.shape), _const_spec(hg.shape)],
        out_specs=[row(ML_WIDTH), per_seq(ML_HEADS, ML_DH, ML_DH), per_seq(SUBLANES, ML_DH),
                   per_seq(SUBLANES, LANES), per_seq(SUBLANES, 2 * ML_WIDTH)],
        out_shape=[jax.ShapeDtypeStruct((n_seq * seq_len, ML_WIDTH), BF16),
                   jax.ShapeDtypeStruct((n_seq, ML_HEADS, ML_DH, ML_DH), F32),
                   jax.ShapeDtypeStruct((n_seq, SUBLANES, ML_DH), F32),
                   jax.ShapeDtypeStruct((n_seq, SUBLANES, LANES), F32),
                   jax.ShapeDtypeStruct((n_seq, SUBLANES, 2 * ML_WIDTH), F32)],
        scratch_shapes=[pltpu.VMEM((SUBLANES + L, 2 * ML_WIDTH), F32),
                        pltpu.VMEM((ML_HEADS, ML_DH, ML_DH), F32),
                        pltpu.VMEM((SUBLANES, ML_DH), F32),
                        pltpu.VMEM((SUBLANES, LANES), F32)],
        compiler_params=_params(2), name="mlstm_prompt",
    )(qk, v, o, gates, cw, cb, gb, hg)


def _mlstm_sample_kernel(qk_ref, v_ref, o_ref, gates_ref, st_ref, cw_ref, cb_ref, gb_ref, hg_ref,
                         c0_ref, n0_ref, m0_ref,
                         h_ref, c_out, n_out, m_out, ac_ref, *, n_seq, n_new):
    L = n_seq * n_new
    ext = jnp.concatenate([st_ref[...], qk_ref[...]], axis=0)
    y = cb_ref[...]
    for i in range(ML_CONV):
        y = y + ext[i * n_seq:i * n_seq + L, :] * cw_ref[i:i + 1, :]
    qk = y * _sigmoid(y)

    gl = gates_ref[...] + gb_ref[...]
    logf = _log_sigmoid(gl)
    tblk = lambda a, t: a[t * n_seq:(t + 1) * n_seq, :]
    b_t = [tblk(logf, 0)]
    for t in range(1, n_new):
        b_t.append(b_t[-1] + tblk(logf, t))
    bcol = jnp.concatenate(b_t, axis=0)
    bT = bcol.T
    glT = gl.T
    rows = lax.broadcasted_iota(jnp.int32, (L, L), 0)
    cols = lax.broadcasted_iota(jnp.int32, (L, L), 1)
    same_seq_causal = (rows >= cols) & (((rows - cols) % n_seq) == 0)
    row_seq = lax.broadcasted_iota(jnp.int32, (L, ML_DH), 0) % n_seq
    col_seq = lax.broadcasted_iota(jnp.int32, (ML_DH, L), 1) % n_seq
    m0 = m0_ref[...]
    m_out[...] = m0

    outs = []
    for h in range(ML_HEADS):
        sl = slice(h * ML_DH, (h + 1) * ML_DH)
        q = qk[:, sl]
        k = qk[:, ML_WIDTH + h * ML_DH:ML_WIDTH + (h + 1) * ML_DH] * (ML_DH ** -0.5)
        v = v_ref[:, sl]
        q_bf = q.astype(BF16)
        k_bf = k.astype(BF16)
        bc = bcol[:, ML_HEADS + h:ML_HEADS + h + 1]
        br = bT[ML_HEADS + h:ML_HEADS + h + 1, :]
        li_r = glT[h:h + 1, :]
        li_c = gl[:, h:h + 1]
        m0_h = m0[:, h:h + 1]
        m_rows = jnp.concatenate([m0_h] * n_new, axis=0)
        n0_h = n0_ref[:, sl]
        nq = jnp.sum(q * jnp.concatenate([n0_h] * n_new, axis=0), axis=-1, keepdims=True)

        b_last = tblk(bc, n_new - 1)
        g_t = [b_last - tblk(bc, t) + tblk(li_c, t) for t in range(n_new)]
        m_new = b_last + m0_h
        for t in range(n_new):
            m_new = jnp.maximum(m_new, g_t[t])
        a_s_t = [jnp.exp(g_t[t] - m_new) for t in range(n_new)]
        a_c = jnp.exp(b_last + m0_h - m_new)
        n_new_h = a_c * n0_h
        for t in range(n_new):
            n_new_h = n_new_h + a_s_t[t] * tblk(k, t)
        n_out[:, sl] = n_new_h
        m_out[:, h:h + 1] = m_new
        ac_ref[...] = jnp.broadcast_to(a_c, (n_seq, LANES))
        av_t = (jnp.concatenate(a_s_t, axis=0) * v).T

        def per_seq(b, cq):
            c_prev = c0_ref[b, h]
            part = _mm_nt(q_bf, c_prev.astype(BF16))
            cq = jnp.where(row_seq == b, part, cq)
            upd = _mm(jnp.where(col_seq == b, av_t, 0.0).astype(BF16), k_bf)
            c_out[b, h] = ac_ref[pl.ds(b, 1), :] * c_prev + upd
            return cq

        cq = lax.fori_loop(0, n_seq, per_seq, jnp.zeros((L, ML_DH), F32))
        outs.append(_mlstm_head(q, k, v, o_ref[:, sl], hg_ref[:, sl], bc, br, li_r,
                                m_rows, cq, nq, same_seq_causal))
    h_ref[...] = jnp.concatenate(outs, axis=-1).astype(h_ref.dtype)


def _mlstm_sample(qk, v, o, gates, st, cw, cb, gb, hg, c0, n0, m0, *, n_seq, n_new):
    L = n_seq * n_new
    return pl.pallas_call(
        functools.partial(_mlstm_sample_kernel, n_seq=n_seq, n_new=n_new),
        out_shape=[jax.ShapeDtypeStruct((L, ML_WIDTH), BF16),
                   jax.ShapeDtypeStruct(c0.shape, F32),
                   jax.ShapeDtypeStruct(n0.shape, F32),
                   jax.ShapeDtypeStruct(m0.shape, F32)],
        scratch_shapes=[pltpu.VMEM((n_seq, LANES), F32)],
        compiler_params=pltpu.CompilerParams(vmem_limit_bytes=VMEM_LIMIT), name="mlstm_sample",
    )(qk, v, o, gates, st, cw, cb, gb, hg, c0, n0, m0)


def _neg_upper(n):
    rows = lax.broadcasted_iota(jnp.int32, (n, n), 0)
    cols = lax.broadcasted_iota(jnp.int32, (n, n), 1)
    return jnp.where(rows > cols, -1.0, 0.0).astype(BF16)


def _sb_tiles(zs, vts, neg_upper, rests, mask, chained=False):
    z2s, sps, his, los = [], [], [], []
    for z in zs:
        z2 = z * LOG2E
        neg_abs = lax.bitcast_convert_type(lax.bitcast_convert_type(z2, jnp.uint32) | jnp.uint32(1 << 31), F32)
        sp = jnp.maximum(z2, 0.0) + jnp.log2(1.0 + jnp.exp2(neg_abs))
        if mask is not None:
            sp = jnp.where(mask, sp, 0.0)
        hi, lo = _split_hi_lo(sp)
        z2s.append(z2), sps.append(sp), his.append(hi), los.append(lo)
    css = [_mm(hi, neg_upper) + _mm(lo, neg_upper) for hi, lo in zip(his, los)]
    pvs, new_rests = [], []
    for i, (z2, sp, cs, vt) in enumerate(zip(z2s, sps, css, vts)):
        rest = new_rests[-1] if (chained and i) else rests[i]
        a = jnp.exp2(z2 - sp + cs + rest)
        if mask is not None:
            a = jnp.where(mask, a, 0.0)
        pvs.append(_mm_nt(a.astype(BF16), vt))
        new_rests.append(rest + cs[:, 0:1] - sp[:, 0:1])
    return pvs, (new_rests[-1:] if chained else new_rests)


def _sb_prompt_kernel(bias_ref, q_ref, kt_ref, vt_ref, o_ref, kb_ref, vb_ref, *, blk):
    hg = q_ref.shape[1]
    hi_ = pl.program_id(1)
    qi = pl.program_id(2)
    n_blk = kb_ref.shape[1]

    @pl.when(qi == 0)
    def _():
        for hh in range(hg):
            for j in range(n_blk):
                kb_ref[hh, j] = kt_ref[0, hh, :, j * blk:(j + 1) * blk].astype(BF16)
                vb_ref[hh, j] = vt_ref[0, hh, :, j * blk:(j + 1) * blk].astype(BF16)

    neg_upper = _neg_upper(blk)
    strict_causal = (lax.broadcasted_iota(jnp.int32, (blk, blk), 1)
                     < lax.broadcasted_iota(jnp.int32, (blk, blk), 0))

    def tiles(kb, mask, carry):
        accs, rests = carry
        zs = [_mm(q_ref[0, hh], kb_ref[hh, kb]) + bias_ref[hi_ * hg + hh] for hh in range(hg)]
        pvs, rests = _sb_tiles(zs, [vb_ref[hh, kb] for hh in range(hg)], neg_upper, rests, mask)
        return tuple(a + p for a, p in zip(accs, pvs)), tuple(rests)

    carry = ((jnp.zeros((blk, SB_DH), F32),) * hg, (jnp.zeros((blk, 1), F32),) * hg)
    carry = tiles(qi, strict_causal, carry)
    accs, _ = lax.fori_loop(0, qi, lambda i, c: tiles(qi - 1 - i, None, c), carry)
    o_ref[...] = jnp.concatenate(accs, axis=-1).astype(o_ref.dtype)


def _sb_prompt(bias, qh, kt, vt, *, blk, heads_per_step):
    n_seq, n_heads, seq_len, dh = qh.shape
    nq = seq_len // blk
    hg = heads_per_step
    kv_spec = pl.BlockSpec((1, hg, dh, seq_len), lambda b, h, qi: (b, h, 0, 0))
    return pl.pallas_call(
        functools.partial(_sb_prompt_kernel, blk=blk),
        grid=(n_seq, n_heads // hg, nq),
        in_specs=[pl.BlockSpec(memory_space=pltpu.SMEM),
                  pl.BlockSpec((1, hg, blk, dh), lambda b, h, qi: (b, h, qi, 0)),
                  kv_spec, kv_spec],
        out_specs=pl.BlockSpec((blk, hg * dh), lambda b, h, qi: (b * nq + qi, h)),
        out_shape=jax.ShapeDtypeStruct((n_seq * seq_len, n_heads * dh), BF16),
        scratch_shapes=[pltpu.VMEM((hg, nq, dh, blk), BF16), pltpu.VMEM((hg, nq, dh, blk), BF16)],
        compiler_params=_params(3), name="sb_prompt",
    )(bias, qh, kt, vt)


def _sb_sample_kernel(pt_ref, q_ref, bias_ref, kn_ref, vn_ref, *refs, pages_per_step, n_new):
    del pt_ref
    k_refs = refs[:pages_per_step]
    v_refs = refs[pages_per_step:2 * pages_per_step]
    o_ref, acc_ref, rest_ref = refs[2 * pages_per_step:]
    j = pl.program_id(1)
    n_rows = n_new * SB_HEADS
    page = kn_ref.shape[1]
    head_of_col = lax.broadcasted_iota(jnp.int32, (n_rows, SB_WIDTH), 1) // SB_DH
    head_of_row = lax.broadcasted_iota(jnp.int32, (n_rows, SB_WIDTH), 0) % SB_HEADS
    own_head = head_of_col == head_of_row
    q = jnp.where(own_head, q_ref[0], 0.0).astype(BF16)
    bias = bias_ref[...]

    def visit(kts, vts, mask):
        zs = [_mm(q, kt) + bias for kt in kts]
        pvs, rests = _sb_tiles(zs, vts, _neg_upper(kts[0].shape[1]), [rest_ref[:, 0:1]], mask, chained=True)
        acc_ref[...] += functools.reduce(lambda a, b: a + b, pvs)
        rest_ref[...] = jnp.broadcast_to(rests[0], rest_ref.shape)

    @pl.when(j == 0)
    def _():
        acc_ref[...] = jnp.zeros_like(acc_ref)
        rest_ref[...] = jnp.zeros_like(rest_ref)
        t_of_row = lax.broadcasted_iota(jnp.int32, (n_rows, page), 0) // SB_HEADS
        s_of_col = lax.broadcasted_iota(jnp.int32, (n_rows, page), 1)
        visit([kn_ref[0].T.astype(BF16)], [vn_ref[0].T.astype(BF16)], s_of_col < t_of_row)

    def page_pair(p_refs, i):
        lo_hi = [p_refs[i + 1][0, 0].reshape(SB_WIDTH, page), p_refs[i][0, 0].reshape(SB_WIDTH, page)]
        return jnp.concatenate(lo_hi, axis=1).astype(BF16)

    pairs = range(0, pages_per_step, 2)
    visit([page_pair(k_refs, i) for i in pairs], [page_pair(v_refs, i) for i in pairs], None)

    @pl.when(j == pl.num_programs(1) - 1)
    def _():
        acc = jnp.where(own_head, acc_ref[...], 0.0)
        for t in range(n_new):
            o_ref[0, t:t + 1, :] = jnp.sum(acc[t * SB_HEADS:(t + 1) * SB_HEADS, :], axis=0, keepdims=True)


def _sb_sample(page_table, q_rows, bias_rows, k_new, v_new, cache_kt, cache_vt, *, layer, pages_per_step, n_new):
    n_seq, n_pages = page_table.shape
    page = cache_kt.shape[4]
    n_rows = n_new * SB_HEADS
    steps = n_pages // pages_per_step

    def page_spec(i):
        return pl.BlockSpec((1, 1, SB_HEADS, SB_DH, page),
                            lambda b, j, pt: (layer, pt[b, n_pages - 1 - (j * pages_per_step + i)], 0, 0, 0))

    per_seq = lambda r, c: pl.BlockSpec((1, r, c), lambda b, j, pt: (b, 0, 0))
    grid_spec = pltpu.PrefetchScalarGridSpec(
        num_scalar_prefetch=1, grid=(n_seq, steps),
        in_specs=[per_seq(n_rows, SB_WIDTH),
                  pl.BlockSpec((n_rows, 1), lambda b, j, pt: (0, 0)),
                  per_seq(page, SB_WIDTH), per_seq(page, SB_WIDTH)]
                 + [page_spec(i) for i in range(pages_per_step)] * 2,
        out_specs=per_seq(n_new, SB_WIDTH),
        scratch_shapes=[pltpu.VMEM((n_rows, SB_WIDTH), F32), pltpu.VMEM((n_rows, LANES), F32)])
    return pl.pallas_call(
        functools.partial(_sb_sample_kernel, pages_per_step=pages_per_step, n_new=n_new),
        grid_spec=grid_spec,
        out_shape=jax.ShapeDtypeStruct((n_seq, n_new, SB_WIDTH), F32),
        compiler_params=_params(2), name="sb_sample",
    )(page_table, q_rows, bias_rows, k_new, v_new,
      *([cache_kt] * pages_per_step), *([cache_vt] * pages_per_step))


def _proj_res_kernel(x_ref, a_ref, b_ref, w_ref, o_ref):
    ka = a_ref.shape[1]
    o_ref[...] = (x_ref[...] + _mm(a_ref[...].astype(BF16), w_ref[0:ka, :])
                  + _mm(b_ref[...].astype(BF16), w_ref[ka:, :]))


def _proj_res(x, a, b, w, *, tm):
    m, d = x.shape
    row = lambda width: pl.BlockSpec((tm, width), lambda i: (i, 0))
    return pl.pallas_call(
        _proj_res_kernel, grid=(m // tm,),
        in_specs=[row(d), row(a.shape[1]), row(b.shape[1]), _const_spec(w.shape)],
        out_specs=row(d), out_shape=jax.ShapeDtypeStruct((m, d), F32),
        compiler_params=_params(1), name="proj_res",
    )(x, a, b, w)


def _ffn_chunks(xn, win_ref, wout_ref, cw_ref, cb_ref, o_ref, conv_fn, d_ff):
    ck = MXU_N
    n_chunks = d_ff // ck

    def up(c):
        return (_mm(xn, win_ref[:, c * ck:(c + 1) * ck]),
                _mm(xn, win_ref[:, d_ff + c * ck:d_ff + (c + 1) * ck]))

    nxt = up(0)
    for c in range(n_chunks):
        sl = slice(c * ck, (c + 1) * ck)
        g, u = nxt
        if c + 1 < n_chunks:
            nxt = up(c + 1)
        g_m2, g_m1 = conv_fn(g, sl)
        y = cb_ref[:, sl] + g_m2 * cw_ref[0:1, sl] + g_m1 * cw_ref[1:2, sl] + g * cw_ref[2:3, sl]
        hmid = (_gelu(y) * u).astype(BF16)
        o_ref[...] += _mm(hmid, wout_ref[sl, :])


def _ffn_prompt_kernel(x_ref, g_ref, win_ref, cw_ref, cb_ref, wout_ref, fg_ref, o_ref, tail_out,
                       xn_ref, prev_ref, st_ref, *, tiles_per_seq, d_ff, final_norm):
    i = pl.program_id(0)
    tm = x_ref.shape[0]
    halo = SUBLANES

    @pl.when(i % tiles_per_seq == 0)
    def _():
        prev_ref[...] = jnp.zeros_like(prev_ref)

    x = x_ref[...]
    xn_ref[...] = _rms(x, g_ref[...]).astype(BF16)
    o_ref[...] = x

    def conv_fn(g, sl):
        st_ref[0:halo, :] = prev_ref[:, sl]
        st_ref[halo:halo + tm, :] = g
        prev_ref[:, sl] = g[tm - halo:tm, :]
        return st_ref[halo - 2:halo - 2 + tm, :], st_ref[halo - 1:halo - 1 + tm, :]

    _ffn_chunks(xn_ref[...], win_ref, wout_ref, cw_ref, cb_ref, o_ref, conv_fn, d_ff)
    if final_norm:
        o_ref[...] = _rms(o_ref[...], fg_ref[...])

    @pl.when(i % tiles_per_seq == tiles_per_seq - 1)
    def _():
        tail_out[0] = prev_ref[...]


def _ffn_prompt(x, g, win, cw, cb, wout, fg, *, n_seq, seq_len, tm, final_norm):
    m, d = x.shape
    d_ff = wout.shape[0]
    tps = seq_len // tm
    row = pl.BlockSpec((tm, d), lambda i: (i, 0))
    return pl.pallas_call(
        functools.partial(_ffn_prompt_kernel, tiles_per_seq=tps, d_ff=d_ff, final_norm=final_norm),
        grid=(m // tm,),
        in_specs=[row, _const_spec(g.shape), _const_spec(win.shape), _const_spec(cw.shape),
                  _const_spec(cb.shape), _const_spec(wout.shape), _const_spec(fg.shape)],
        out_specs=[row, pl.BlockSpec((1, SUBLANES, d_ff), lambda i: (i // tps, 0, 0))],
        out_shape=[jax.ShapeDtypeStruct((m, d), F32), jax.ShapeDtypeStruct((n_seq, SUBLANES, d_ff), F32)],
        scratch_shapes=[pltpu.VMEM((tm, d), BF16), pltpu.VMEM((SUBLANES, d_ff), F32),
                        pltpu.VMEM((SUBLANES + tm, MXU_N), F32)],
        compiler_params=_params(1), name="ffn_prompt",
    )(x, g, win, cw, cb, wout, fg)


def _ffn_sample_kernel(x_ref, g_ref, win_ref, cw_ref, cb_ref, wout_ref, fg_ref, st_ref, o_ref, tail_out,
                       *, n_seq, d_ff, final_norm):
    rows = x_ref.shape[0]
    x = x_ref[...]
    xn = _rms(x, g_ref[...]).astype(BF16)
    o_ref[...] = x

    def conv_fn(g, sl):
        ext = jnp.concatenate([st_ref[:, sl], g], axis=0)
        tail_out[:, sl] = g[rows - 2 * n_seq:rows, :]
        return ext[0:rows, :], ext[n_seq:n_seq + rows, :]

    _ffn_chunks(xn, win_ref, wout_ref, cw_ref, cb_ref, o_ref, conv_fn, d_ff)
    if final_norm:
        o_ref[...] = _rms(o_ref[...], fg_ref[...])


def _ffn_sample(x, g, win, cw, cb, wout, fg, st, *, n_seq, final_norm):
    d_ff = wout.shape[0]
    return pl.pallas_call(
        functools.partial(_ffn_sample_kernel, n_seq=n_seq, d_ff=d_ff, final_norm=final_norm),
        out_shape=[jax.ShapeDtypeStruct(x.shape, F32), jax.ShapeDtypeStruct(st.shape, F32)],
        compiler_params=pltpu.CompilerParams(vmem_limit_bytes=VMEM_LIMIT), name="ffn_sample",
    )(x, g, win, cw, cb, wout, fg, st)


def _layernorm_stats(v_ref, width, n_groups):
    gw = width // n_groups
    s1 = 0.0
    for gi in range(n_groups):
        s1 = s1 + jnp.sum(v_ref[:, gi * gw:(gi + 1) * gw], axis=-1, keepdims=True)
    mean = s1 / width
    s2 = 0.0
    for gi in range(n_groups):
        xc = v_ref[:, gi * gw:(gi + 1) * gw] - mean
        s2 = s2 + jnp.sum(xc * xc, axis=-1, keepdims=True)
    return mean, lax.rsqrt(s2 / width + EPS)


def _cm_prompt_kernel(x_ref, g_ref, win_ref, lng_ref, lnb_ref, ws_ref, bst_ref, wout_ref, o_ref,
                      xn_ref, v_ref, *, width):
    tm = x_ref.shape[0]
    gw = width // CM_GROUPS
    x = x_ref[...]
    xn_ref[...] = _rms(x, g_ref[...]).astype(BF16)
    o_ref[...] = x
    proj = lambda col0, gi: _mm(xn_ref[...], win_ref[:, col0 + gi * gw:col0 + (gi + 1) * gw])
    nxt = proj(width, 0)
    for gi in range(CM_GROUPS):
        cur, nxt = nxt, (proj(width, gi + 1) if gi + 1 < CM_GROUPS else proj(0, 0))
        v_ref[:, gi * gw:(gi + 1) * gw] = _gelu(cur)
    mean, rstd = _layernorm_stats(v_ref, width, CM_GROUPS)
    rows = lax.broadcasted_iota(jnp.int32, (CM_CHUNK, CM_CHUNK), 0)
    cols = lax.broadcasted_iota(jnp.int32, (CM_CHUNK, CM_CHUNK), 1)
    tril = rows >= cols
    for gi in range(CM_GROUPS):
        sl = slice(gi * gw, (gi + 1) * gw)
        u_pre, nxt = nxt, (proj(0, gi + 1) if gi + 1 < CM_GROUPS else None)
        vn = ((v_ref[:, sl] - mean) * rstd * lng_ref[:, sl] + lnb_ref[:, sl]).astype(BF16)
        wsg = jnp.where(tril, ws_ref[gi], 0.0).astype(BF16)
        bias = bst_ref[:, gi:gi + 1]
        mixed = jnp.concatenate(
            [_mm(wsg, vn[c * CM_CHUNK:(c + 1) * CM_CHUNK, :]) + bias for c in range(tm // CM_CHUNK)], axis=0)
        o_ref[...] += _mm((_gelu(u_pre) * mixed).astype(BF16), wout_ref[sl, :])


def _cm_prompt(x, g, win, lng, lnb, ws, bst, wout, *, tm):
    m, d = x.shape
    width = wout.shape[0]
    row = pl.BlockSpec((tm, d), lambda i: (i, 0))
    return pl.pallas_call(
        functools.partial(_cm_prompt_kernel, width=width),
        grid=(m // tm,),
        in_specs=[row] + [_const_spec(a.shape) for a in (g, win, lng, lnb, ws, bst, wout)],
        out_specs=row, out_shape=jax.ShapeDtypeStruct((m, d), F32),
        scratch_shapes=[pltpu.VMEM((tm, d), BF16), pltpu.VMEM((tm, width), F32)],
        compiler_params=_params(1), name="cm_prompt",
    )(x, g, win, lng, lnb, ws, bst, wout)


def _cm_sample_kernel(x_ref, g_ref, win_ref, lng_ref, lnb_ref, wexp_ref, bexp_ref, wout_ref, o_ref, v_out,
                      *, n_seq, n_new, width):
    gw = width // CM_GROUPS
    x = x_ref[...]
    xn = _rms(x, g_ref[...]).astype(BF16)
    for gi in range(CM_GROUPS):
        v_out[:, gi * gw:(gi + 1) * gw] = _gelu(_mm(xn, win_ref[:, width + gi * gw:width + (gi + 1) * gw]))
    mean, rstd = _layernorm_stats(v_out, width, CM_GROUPS)
    acc = x
    for gi in range(CM_GROUPS):
        sl = slice(gi * gw, (gi + 1) * gw)
        vn = (v_out[:, sl] - mean) * rstd * lng_ref[:, sl] + lnb_ref[:, sl]
        v_out[:, sl] = vn
        mixed = []
        for t in range(n_new):
            mt = bexp_ref[t:t + 1, sl]
            for s in range(t + 1):
                mt = mt + wexp_ref[t * n_new + s:t * n_new + s + 1, sl] * vn[s * n_seq:(s + 1) * n_seq, :]
            mixed.append(mt)
        u = _gelu(_mm(xn, win_ref[:, sl]))
        acc = acc + _mm((u * jnp.concatenate(mixed, axis=0)).astype(BF16), wout_ref[sl, :])
    o_ref[...] = acc


def _cm_sample(x, g, win, lng, lnb, wexp, bexp, wout, *, n_seq, n_new):
    width = wout.shape[0]
    return pl.pallas_call(
        functools.partial(_cm_sample_kernel, n_seq=n_seq, n_new=n_new, width=width),
        out_shape=[jax.ShapeDtypeStruct(x.shape, F32), jax.ShapeDtypeStruct((x.shape[0], width), F32)],
        compiler_params=pltpu.CompilerParams(vmem_limit_bytes=VMEM_LIMIT), name="cm_sample",
    )(x, g, win, lng, lnb, wexp, bexp, wout)


def _time_major(a):
    return jnp.swapaxes(a, 0, 1).reshape((a.shape[0] * a.shape[1],) + a.shape[2:])


def _batch_major(a, n_seq):
    return jnp.swapaxes(a.reshape((a.shape[0] // n_seq, n_seq) + a.shape[1:]), 0, 1)


def _pad_lanes(a):
    return jnp.pad(a, [(0, 0)] * (a.ndim - 1) + [(0, LANES - a.shape[-1])])


def kernel(x_prompt, x_sample, cache_k, cache_v, page_table, state_mlstm_c, state_mlstm_n, state_mlstm_m, state_mlstm_conv, state_ffn_conv, norm_mix_g, norm_ffn_g, norm_final_g, w_in_even, ml_conv_w, ml_conv_b, ml_gate_b, ml_head_g, sb_logit_b, w_out_even, w_in_odd, cm_ln_g, cm_ln_b, cm_spatial_w, cm_spatial_b, w_out_odd, ffn_w_in, ffn_conv_w, ffn_conv_b, ffn_w_out):
    n_pb, seq_p, d = x_prompt.shape
    n_sb, seq_s, _ = x_sample.shape
    depth = norm_mix_g.shape[0]
    d_ff = ffn_w_out.shape[1]
    hp = x_prompt.reshape(n_pb * seq_p, d)
    hs = _time_major(x_sample)
    fg = norm_final_g.reshape(1, d)

    kp_l, vp_l, ks_l, vs_l = [], [], [], []
    cp_l, np_l, mp_l, bp_l = [], [], [], []
    cs_l, ns_l, ms_l, bs_l = [], [], [], []
    cmv_l, fp_l, fs_l = [], [], []
    for layer in range(depth):
        mix_g = norm_mix_g[layer].reshape(1, d)
        if layer % 2 == 0:
            e = layer // 2
            w = w_in_even[e]
            gate_lo = _C_O + ML_WIDTH
            gate_hi = gate_lo + 2 * ML_HEADS
            w_all = jnp.concatenate([w[:, :gate_lo], w[:, gate_hi:], _pad_lanes(w[:, gate_lo:gate_hi])],
                                    axis=1).astype(BF16)
            w_kvt = w[:, gate_hi + SB_WIDTH:].T.astype(BF16)
            cw, cb = ml_conv_w[e], ml_conv_b[e].reshape(1, -1)
            gb = _pad_lanes(ml_gate_b[e].reshape(1, -1))
            hg = ml_head_g[e].reshape(1, -1)
            w_out = w_out_even[e].astype(BF16)
            qk, vm, om, gates, qh, kt, vt = _even_in(
                hp, mix_g, w_all, w_kvt, n_seq=n_pb, seq_len=seq_p, tm=512, head_major=True)
            h_ml, c_p, n_p, m_p, tail_p = _mlstm_prompt(qk, vm, om, gates, cw, cb, gb, hg, n_seq=n_pb, seq_len=seq_p)
            h_sb = _sb_prompt(sb_logit_b[e], qh, kt, vt, blk=256, heads_per_step=4)
            hp = _proj_res(hp, h_ml, h_sb, w_out, tm=512)
            kp_l.append(jnp.transpose(kt, (0, 3, 1, 2)))
            vp_l.append(jnp.transpose(vt, (0, 3, 1, 2)))
            cp_l.append(c_p)
            np_l.append(n_p[:, :ML_HEADS])
            mp_l.append(m_p[:, :ML_HEADS, 0])
            bp_l.append(tail_p[:, SUBLANES - (ML_CONV - 1):])
            qk, vm, om, gates, sq, sk, sv = _even_in(
                hs, mix_g, w_all, w_kvt, n_seq=n_sb, seq_len=seq_s, tm=n_sb * seq_s, head_major=False)
            h_ml, c_s, n_s, m_s = _mlstm_sample(
                qk, vm, om, gates, _time_major(state_mlstm_conv[e]), cw, cb, gb, hg,
                state_mlstm_c[e], state_mlstm_n[e].reshape(n_sb, ML_WIDTH), _pad_lanes(state_mlstm_m[e]),
                n_seq=n_sb, n_new=seq_s)
            page = cache_k.shape[2]
            pad_keys = lambda a: jnp.pad(_batch_major(a, n_sb), ((0, 0), (0, page - seq_s), (0, 0)))
            q_rows = jnp.broadcast_to(_batch_major(sq, n_sb)[:, :, None, :],
                                      (n_sb, seq_s, SB_HEADS, SB_WIDTH)).reshape(n_sb, seq_s * SB_HEADS, SB_WIDTH)
            bias_rows = jnp.tile(sb_logit_b[e], seq_s).reshape(seq_s * SB_HEADS, 1)
            pool_view = lambda c: jnp.transpose(c, (0, 1, 3, 4, 2))
            h_sb = _sb_sample(page_table, q_rows, bias_rows, pad_keys(sk), pad_keys(sv),
                              pool_view(cache_k), pool_view(cache_v), layer=e, pages_per_step=16, n_new=seq_s)
            hs = _proj_res(hs, h_ml, _time_major(h_sb), w_out, tm=n_sb * seq_s)
            ks_l.append(_batch_major(sk, n_sb).reshape(n_sb, seq_s, SB_HEADS, SB_DH))
            vs_l.append(_batch_major(sv, n_sb).reshape(n_sb, seq_s, SB_HEADS, SB_DH))
            cs_l.append(c_s)
            ns_l.append(n_s.reshape(n_sb, ML_HEADS, ML_DH))
            ms_l.append(m_s[:, :ML_HEADS])
            bs_l.append(_batch_major(qk[(seq_s - (ML_CONV - 1)) * n_sb:], n_sb))
        else:
            o = layer // 2
            win = w_in_odd[o].astype(BF16)
            wout = w_out_odd[o].astype(BF16)
            lng, lnb = cm_ln_g[o].reshape(1, -1), cm_ln_b[o].reshape(1, -1)
            width = wout.shape[0]
            gw = width // CM_GROUPS
            hp = _cm_prompt(hp, mix_g, win, lng, lnb, cm_spatial_w[o], cm_spatial_b[o].T, wout, tm=512)
            wexp = jnp.repeat(cm_spatial_w[o][:, :seq_s, :seq_s].reshape(CM_GROUPS, seq_s * seq_s).T, gw, axis=1)
            bexp = jnp.repeat(cm_spatial_b[o][:, :seq_s].T, gw, axis=1)
            hs, v_rows = _cm_sample(hs, mix_g, win, lng, lnb, wexp, bexp, wout, n_seq=n_sb, n_new=seq_s)
            cmv_l.append(_batch_major(v_rows, n_sb))
        ffn_g = norm_ffn_g[layer].reshape(1, d)
        win = ffn_w_in[layer].astype(BF16)
        wout = ffn_w_out[layer].astype(BF16)
        cw, cb = ffn_conv_w[layer], ffn_conv_b[layer].reshape(1, -1)
        last = layer == depth - 1
        hp, tail_p = _ffn_prompt(hp, ffn_g, win, cw, cb, wout, fg, n_seq=n_pb, seq_len=seq_p, tm=1024,
                                 final_norm=last)
        hs, tail_s = _ffn_sample(hs, ffn_g, win, cw, cb, wout, fg, _time_major(state_ffn_conv[layer]),
                                 n_seq=n_sb, final_norm=last)
        fp_l.append(tail_p[:, SUBLANES - (FFN_CONV - 1):])
        fs_l.append(_batch_major(tail_s, n_sb))

    return (hp.reshape(n_pb, seq_p, d), _batch_major(hs, n_sb),
            jnp.stack(kp_l), jnp.stack(vp_l), jnp.stack(ks_l), jnp.stack(vs_l),
            jnp.stack(cp_l), jnp.stack(np_l), jnp.stack(mp_l), jnp.stack(bp_l),
            jnp.stack(cs_l), jnp.stack(ns_l), jnp.stack(ms_l), jnp.stack(bs_l),
            jnp.stack(cmv_l), jnp.stack(fp_l), jnp.stack(fs_l))
```

```python
import functools

import jax
import jax.numpy as jnp
from jax import lax
from jax.experimental import pallas as pl
from jax.experimental.pallas import tpu as pltpu

F32 = jnp.float32
BF16 = jnp.bfloat16
EPS = 1e-6
LOG2E = 1.4426950408889634

ML_HEADS = 4
ML_DH = 128
ML_WIDTH = ML_HEADS * ML_DH
ML_CONV = 4
ML_CHUNK = 128
SB_HEADS = 8
SB_DH = 64
SB_WIDTH = SB_HEADS * SB_DH
SB_SCALE = SB_DH ** -0.5
CM_GROUPS = 8
CM_CHUNK = 128
FFN_CONV = 3
LANES = 128
SUBLANES = 8
MXU_N = 256
VMEM_LIMIT = 56 * 1024 * 1024


def _mm(a, b):
    return jnp.dot(a, b, preferred_element_type=F32)


def _mm_nt(a, b):
    return lax.dot_general(a, b, (((1,), (1,)), ((), ())), preferred_element_type=F32)


def _rms(x, g):
    return x * lax.rsqrt(jnp.mean(x * x, axis=-1, keepdims=True) + EPS) * g


def _sigmoid(x):
    return 1.0 / (1.0 + jnp.exp(-x))


def _softplus(x):
    return jnp.maximum(x, 0.0) + jnp.log1p(jnp.exp(-jnp.abs(x)))


def _gelu(x):
    c = 0.7978845608028654
    return 0.5 * x * (1.0 + jnp.tanh(c * (x + 0.044715 * (x * x * x))))


def _split_hi_lo(x):
    hi = x.astype(BF16)
    lo = (x - hi.astype(F32)).astype(BF16)
    return hi, lo


def _params(n_axes):
    return pltpu.CompilerParams(dimension_semantics=("arbitrary",) * n_axes,
                                vmem_limit_bytes=VMEM_LIMIT)


def _const_spec(shape):
    nd = len(shape)
    return pl.BlockSpec(shape, lambda *_: (0,) * nd, pipeline_mode=pl.Buffered(1))


_C_QK = 0
_C_V = 2 * ML_WIDTH
_C_O = _C_V + ML_WIDTH
_C_SQ = _C_O + ML_WIDTH
_C_SK = _C_SQ + SB_WIDTH
_C_SV = _C_SK + SB_WIDTH
_C_G = _C_SV + SB_WIDTH
_C_END = _C_G + LANES


def _even_in_kernel(x_ref, g_ref, w_ref, wkvt_ref, *out_refs, head_major):
    xn = _rms(x_ref[...], g_ref[...]).astype(BF16)
    qk_ref, v_ref, o_ref, gates_ref, sq_ref, sk_ref, sv_ref = out_refs
    qk_ref[...] = _mm(xn, w_ref[:, _C_QK:_C_V])
    v_ref[...] = _mm(xn, w_ref[:, _C_V:_C_O])
    o_ref[...] = _mm(xn, w_ref[:, _C_O:_C_SQ])
    gates_ref[...] = _mm(xn, w_ref[:, _C_G:_C_END])
    sq = _mm(xn, w_ref[:, _C_SQ:_C_SK]) * SB_SCALE
    if head_major:
        tm = sq.shape[0]
        for h in range(SB_HEADS):
            sq_ref[0, h] = sq[:, h * SB_DH:(h + 1) * SB_DH].astype(BF16)
        sk_ref[0] = _mm_nt(wkvt_ref[0:SB_WIDTH, :], xn).reshape(SB_HEADS, SB_DH, tm)
        sv_ref[0] = _mm_nt(wkvt_ref[SB_WIDTH:, :], xn).reshape(SB_HEADS, SB_DH, tm)
    else:
        sq_ref[...] = sq
        sk_ref[...] = _mm(xn, w_ref[:, _C_SK:_C_SV])
        sv_ref[...] = _mm(xn, w_ref[:, _C_SV:_C_G])


def _even_in(x, g, w, wkvt, *, n_seq, seq_len, tm, head_major):
    m, d = x.shape
    nt = m // tm
    row = lambda width: pl.BlockSpec((tm, width), lambda i: (i, 0))
    out_shape = [jax.ShapeDtypeStruct((m, 2 * ML_WIDTH), F32),
                 jax.ShapeDtypeStruct((m, ML_WIDTH), F32),
                 jax.ShapeDtypeStruct((m, ML_WIDTH), F32),
                 jax.ShapeDtypeStruct((m, LANES), F32)]
    out_specs = [row(2 * ML_WIDTH), row(ML_WIDTH), row(ML_WIDTH), row(LANES)]
    if head_major:
        tps = seq_len // tm
        out_shape += [jax.ShapeDtypeStruct((n_seq, SB_HEADS, seq_len, SB_DH), BF16)]
        out_specs += [pl.BlockSpec((1, SB_HEADS, tm, SB_DH), lambda i: (i // tps, 0, i % tps, 0))]
        out_shape += [jax.ShapeDtypeStruct((n_seq, SB_HEADS, SB_DH, seq_len), F32)] * 2
        out_specs += [pl.BlockSpec((1, SB_HEADS, SB_DH, tm), lambda i: (i // tps, 0, 0, i % tps))] * 2
    else:
        out_shape += [jax.ShapeDtypeStruct((m, SB_WIDTH), F32)] * 3
        out_specs += [row(SB_WIDTH)] * 3
    return pl.pallas_call(
        functools.partial(_even_in_kernel, head_major=head_major),
        grid=(nt,),
        in_specs=[row(d), _const_spec((1, d)), _const_spec(w.shape), _const_spec(wkvt.shape)],
        out_specs=out_specs, out_shape=out_shape,
        compiler_params=_params(1), name="even_in",
    )(x, g, w, wkvt)


def _log_sigmoid(x):
    return jnp.minimum(x, 0.0) - jnp.log1p(jnp.exp(-jnp.abs(x)))


def _mlstm_heads(heads, mask):
    qk = [_mm_nt(h["q"], h["k"]) for h in heads]
    ss, w_inters, floors = [], [], []
    for h, qk_h in zip(heads, qk):
        dmat = jnp.where(mask, h["bc"] - h["br"] + h["li_r"], -jnp.inf)
        inter = h["bc"] + h["m_rows"]
        m_t = jnp.maximum(inter, jnp.max(dmat, axis=-1, keepdims=True))
        ss.append(qk_h * jnp.exp(dmat - m_t))
        w_inters.append(jnp.exp(inter - m_t))
        floors.append(jnp.exp(-m_t))
    svs = [_mm(s.astype(BF16), h["v"]) for s, h in zip(ss, heads)]
    outs = []
    for h, s, sv, w_inter, floor in zip(heads, ss, svs, w_inters, floors):
        num = w_inter * h["cq"] + sv
        den = w_inter * h["nq"] + jnp.sum(s, axis=-1, keepdims=True)
        ho = _sigmoid(h["o"]) * (num / jnp.maximum(jnp.abs(den), floor))
        outs.append(ho * lax.rsqrt(jnp.mean(ho * ho, axis=-1, keepdims=True) + EPS) * h["hg"])
    return outs


def _mlstm_prompt_kernel(qk_ref, v_ref, o_ref, gates_ref, cw_ref, cb_ref, gb_ref, hg_ref,
                         h_ref, c_out, n_out, m_out, tail_out,
                         xp_ref, c_ref, n_ref, m_ref, *, n_chunks):
    ci = pl.program_id(1)
    L = ML_CHUNK
    halo = SUBLANES

    @pl.when(ci == 0)
    def _():
        xp_ref[0:halo, :] = jnp.zeros((halo, 2 * ML_WIDTH), F32)
        c_ref[...] = jnp.zeros_like(c_ref)
        n_ref[...] = jnp.zeros_like(n_ref)
        m_ref[...] = jnp.zeros_like(m_ref)

    xp_ref[halo:halo + L, :] = qk_ref[...]
    y = cb_ref[...]
    for i in range(ML_CONV):
        off = halo - (ML_CONV - 1) + i
        y = y + xp_ref[off:off + L, :] * cw_ref[i:i + 1, :]
    xp_ref[0:halo, :] = qk_ref[L - halo:L, :]
    qk = y * _sigmoid(y)

    gl = gates_ref[...] + gb_ref[...]
    logf = _log_sigmoid(gl)
    rows = lax.broadcasted_iota(jnp.int32, (L, L), 0)
    cols = lax.broadcasted_iota(jnp.int32, (L, L), 1)
    causal = rows >= cols
    tri = jnp.where(causal, 1.0, 0.0).astype(BF16)
    bcol = _cumsum_rows(tri, logf)
    bT = bcol.T
    glT = gl.T

    outs = []
    for h in range(ML_HEADS):
        sl = slice(h * ML_DH, (h + 1) * ML_DH)
        q = qk[:, sl]
        k = qk[:, ML_WIDTH + h * ML_DH:ML_WIDTH + (h + 1) * ML_DH] * (ML_DH ** -0.5)
        v = v_ref[:, sl]
        bc = bcol[:, ML_HEADS + h:ML_HEADS + h + 1]
        m_prev = m_ref[h:h + 1, 0:1]
        c_prev = c_ref[h]
        n_prev = n_ref[h:h + 1, :]
        q_bf, k_bf = q.astype(BF16), k.astype(BF16)
        head = dict(
            q=q_bf, k=k_bf, v=v.astype(BF16), o=o_ref[:, sl], hg=hg_ref[:, sl], bc=bc,
            br=bT[ML_HEADS + h:ML_HEADS + h + 1, :], li_r=glT[h:h + 1, :],
            m_rows=jnp.broadcast_to(m_prev, (L, 1)),
            cq=_mm_nt(q_bf, c_prev.astype(BF16)), nq=jnp.sum(q * n_prev, axis=-1, keepdims=True))
        outs += _mlstm_heads([head], causal)
        b_last = bc[L - 1:L, :]
        g = b_last - bc + gl[:, h:h + 1]
        m_new = jnp.maximum(b_last + m_prev, jnp.max(g, axis=0, keepdims=True))
        a_s = jnp.exp(g - m_new)
        a_c = jnp.exp(b_last + m_prev - m_new)
        c_ref[h] = a_c * c_prev + _mm((a_s * v).T.astype(BF16), k_bf)
        n_ref[h:h + 1, :] = a_c * n_prev + jnp.sum(a_s * k, axis=0, keepdims=True)
        m_ref[h:h + 1, :] = jnp.broadcast_to(m_new, (1, LANES))
    h_ref[...] = jnp.concatenate(outs, axis=-1).astype(h_ref.dtype)

    @pl.when(ci == n_chunks - 1)
    def _():
        c_out[0] = c_ref[...]
        n_out[0] = n_ref[...]
        m_out[0] = m_ref[...]
        tail_out[0] = qk_ref[L - halo:L, :]


def _cumsum_rows(tri, x):
    hi = x.astype(BF16)
    r1 = x - hi.astype(F32)
    mid = r1.astype(BF16)
    lo = (r1 - mid.astype(F32)).astype(BF16)
    return _mm(tri, hi) + _mm(tri, mid) + _mm(tri, lo)


def _mlstm_prompt(qk, v, o, gates, cw, cb, gb, hg, *, n_seq, seq_len):
    nc = seq_len // ML_CHUNK
    L = ML_CHUNK
    row = lambda width: pl.BlockSpec((L, width), lambda b, c: (b * nc + c, 0))
    per_seq = lambda *dims: pl.BlockSpec((1,) + dims, lambda b, c: (b,) + (0,) * len(dims))
    return pl.pallas_call(
        functools.partial(_mlstm_prompt_kernel, n_chunks=nc),
        grid=(n_seq, nc),
        in_specs=[row(2 * ML_WIDTH), row(ML_WIDTH), row(ML_WIDTH), row(LANES),
                  _const_spec(cw.shape), _const_spec(cb.shape), _const_spec(gb.shape), _const_spec(hg.shape)],
        out_specs=[row(ML_WIDTH), per_seq(ML_HEADS, ML_DH, ML_DH), per_seq(SUBLANES, ML_DH),
                   per_seq(SUBLANES, LANES), per_seq(SUBLANES, 2 * ML_WIDTH)],
        out_shape=[jax.ShapeDtypeStruct((n_seq * seq_len, ML_WIDTH), BF16),
                   jax.ShapeDtypeStruct((n_seq, ML_HEADS, ML_DH, ML_DH), F32),
                   jax.ShapeDtypeStruct((n_seq, SUBLANES, ML_DH), F32),
                   jax.ShapeDtypeStruct((n_seq, SUBLANES, LANES), F32),
                   jax.ShapeDtypeStruct((n_seq, SUBLANES, 2 * ML_WIDTH), F32)],
        scratch_shapes=[pltpu.VMEM((SUBLANES + L, 2 * ML_WIDTH), F32),
                        pltpu.VMEM((ML_HEADS, ML_DH, ML_DH), F32),
                        pltpu.VMEM((SUBLANES, ML_DH), F32),
                        pltpu.VMEM((SUBLANES, LANES), F32)],
        compiler_params=_params(2), name="mlstm_prompt",
    )(qk, v, o, gates, cw, cb, gb, hg)


def _mlstm_sample_kernel(qk_ref, v_ref, o_ref, gates_ref, st_ref, cw_ref, cb_ref, gb_ref, hg_ref,
                         c0_ref, n0_ref, m0_ref,
                         h_ref, c_out, n_out, m_out, ac_ref, *, n_seq, n_new):
    L = n_seq * n_new
    ext = jnp.concatenate([st_ref[...], qk_ref[...]], axis=0)
    y = cb_ref[...]
    for i in range(ML_CONV):
        y = y + ext[i * n_seq:i * n_seq + L, :] * cw_ref[i:i + 1, :]
    qk = y * _sigmoid(y)

    gl = gates_ref[...] + gb_ref[...]
    logf = _log_sigmoid(gl)
    tblk = lambda a, t: a[t * n_seq:(t + 1) * n_seq, :]
    b_t = [tblk(logf, 0)]
    for t in range(1, n_new):
        b_t.append(b_t[-1] + tblk(logf, t))
    bcol = jnp.concatenate(b_t, axis=0)
    bT = bcol.T
    glT = gl.T
    rows = lax.broadcasted_iota(jnp.int32, (L, L), 0)
    cols = lax.broadcasted_iota(jnp.int32, (L, L), 1)
    same_seq_causal = (rows >= cols) & (((rows - cols) % n_seq) == 0)
    row_seq = lax.broadcasted_iota(jnp.int32, (L, ML_DH), 0) % n_seq
    col_seq = lax.broadcasted_iota(jnp.int32, (ML_DH, L), 1) % n_seq
    m0 = m0_ref[...]
    m_out[...] = m0

    heads = []
    for h in range(ML_HEADS):
        sl = slice(h * ML_DH, (h + 1) * ML_DH)
        q = qk[:, sl]
        k = qk[:, ML_WIDTH + h * ML_DH:ML_WIDTH + (h + 1) * ML_DH] * (ML_DH ** -0.5)
        v = v_ref[:, sl]
        q_bf = q.astype(BF16)
        k_bf = k.astype(BF16)
        bc = bcol[:, ML_HEADS + h:ML_HEADS + h + 1]
        li_c = gl[:, h:h + 1]
        m0_h = m0[:, h:h + 1]
        m_rows = jnp.concatenate([m0_h] * n_new, axis=0)
        n0_h = n0_ref[:, sl]
        nq = jnp.sum(q * jnp.concatenate([n0_h] * n_new, axis=0), axis=-1, keepdims=True)

        b_last = tblk(bc, n_new - 1)
        g_t = [b_last - tblk(bc, t) + tblk(li_c, t) for t in range(n_new)]
        m_new = b_last + m0_h
        for t in range(n_new):
            m_new = jnp.maximum(m_new, g_t[t])
        a_s_t = [jnp.exp(g_t[t] - m_new) for t in range(n_new)]
        a_c = jnp.exp(b_last + m0_h - m_new)
        n_new_h = a_c * n0_h
        for t in range(n_new):
            n_new_h = n_new_h + a_s_t[t] * tblk(k, t)
        n_out[:, sl] = n_new_h
        m_out[:, h:h + 1] = m_new
        ac_ref[...] = jnp.broadcast_to(a_c, (n_seq, LANES))
        av_t = (jnp.concatenate(a_s_t, axis=0) * v).T

        def per_seq(b, cq):
            c_prev = c0_ref[b, h]
            part = _mm_nt(q_bf, c_prev.astype(BF16))
            cq = jnp.where(row_seq == b, part, cq)
            upd = _mm(jnp.where(col_seq == b, av_t, 0.0).astype(BF16), k_bf)
            c_out[b, h] = ac_ref[pl.ds(b, 1), :] * c_prev + upd
            return cq

        cq = lax.fori_loop(0, n_seq, per_seq, jnp.zeros((L, ML_DH), F32))
        heads.append(dict(q=q_bf, k=k_bf, v=v.astype(BF16), o=o_ref[:, sl], hg=hg_ref[:, sl], bc=bc,
                          br=bT[ML_HEADS + h:ML_HEADS + h + 1, :], li_r=glT[h:h + 1, :],
                          m_rows=m_rows, cq=cq, nq=nq))
    h_ref[...] = jnp.concatenate(_mlstm_heads(heads, same_seq_causal), axis=-1).astype(h_ref.dtype)


def _mlstm_sample(qk, v, o, gates, st, cw, cb, gb, hg, c0, n0, m0, *, n_seq, n_new):
    L = n_seq * n_new
    return pl.pallas_call(
        functools.partial(_mlstm_sample_kernel, n_seq=n_seq, n_new=n_new),
        out_shape=[jax.ShapeDtypeStruct((L, ML_WIDTH), BF16),
                   jax.ShapeDtypeStruct(c0.shape, F32),
                   jax.ShapeDtypeStruct(n0.shape, F32),
                   jax.ShapeDtypeStruct(m0.shape, F32)],
        scratch_shapes=[pltpu.VMEM((n_seq, LANES), F32)],
        compiler_params=pltpu.CompilerParams(vmem_limit_bytes=VMEM_LIMIT), name="mlstm_sample",
    )(qk, v, o, gates, st, cw, cb, gb, hg, c0, n0, m0)


def _neg_upper(n):
    rows = lax.broadcasted_iota(jnp.int32, (n, n), 0)
    cols = lax.broadcasted_iota(jnp.int32, (n, n), 1)
    return jnp.where(rows > cols, -1.0, 0.0).astype(BF16)


def _sb_weights(zs, neg_upper, rests, mask, chained=False):
    log_betas, totals, his, los = [], [], [], []
    for z in zs:
        z2 = z * LOG2E
        sp = jnp.maximum(z2, 0.0) + jnp.log2(1.0 + jnp.exp2(-jnp.abs(z2)))
        if mask is not None:
            sp = jnp.where(mask, sp, 0.0)
        hi, lo = _split_hi_lo(sp)
        log_betas.append(z2 - sp), totals.append(sp[:, 0:1]), his.append(hi), los.append(lo)
    upper2 = jnp.concatenate([neg_upper, neg_upper], axis=0)
    css = [_mm(jnp.concatenate([hi, lo], axis=1), upper2) for hi, lo in zip(his, los)]
    weights, new_rests = [], []
    for i, (lb, cs, tot) in enumerate(zip(log_betas, css, totals)):
        rest = new_rests[-1] if (chained and i) else rests[i]
        a = jnp.exp2(lb + cs + rest)
        if mask is not None:
            a = jnp.where(mask, a, 0.0)
        weights.append(a.astype(BF16))
        new_rests.append(rest + cs[:, 0:1] - tot)
    return weights, (new_rests[-1:] if chained else new_rests)


def _sb_prompt_kernel(bias_ref, q_ref, kt_ref, vt_ref, o_ref, kb_ref, vb_ref, z_ref, a_ref, *, blk):
    hg = q_ref.shape[1]
    hi_ = pl.program_id(1)
    qi = pl.program_id(2)
    n_blk = kb_ref.shape[1]
    heads = range(hg)

    @pl.when(qi == 0)
    def _():
        for hh in heads:
            for j in range(n_blk):
                kb_ref[hh, j] = kt_ref[0, hh, :, j * blk:(j + 1) * blk].astype(BF16)
                vb_ref[hh, j] = vt_ref[0, hh, :, j * blk:(j + 1) * blk].astype(BF16)

    neg_upper = _neg_upper(blk)
    strict_causal = (lax.broadcasted_iota(jnp.int32, (blk, blk), 1)
                     < lax.broadcasted_iota(jnp.int32, (blk, blk), 0))

    def logits(kb):
        return [_mm(q_ref[0, hh], kb_ref[hh, kb]) + bias_ref[hi_ * hg + hh] for hh in heads]

    def weighted_values(kb):
        return [_mm_nt(a_ref[hh], vb_ref[hh, kb]) for hh in heads]

    def visit(kb, mask, carry, first):
        accs, rests = carry
        zs = [z_ref[hh] for hh in heads]
        z_next = logits(jnp.maximum(kb - 1, 0))
        if not first:
            accs = tuple(a + p for a, p in zip(accs, weighted_values(kb + 1)))
        for hh in heads:
            z_ref[hh] = z_next[hh]
        weights, rests = _sb_weights(zs, neg_upper, rests, mask)
        for hh in heads:
            a_ref[hh] = weights[hh]
        return accs, tuple(rests)

    for hh, z in enumerate(logits(qi)):
        z_ref[hh] = z
    carry = ((jnp.zeros((blk, SB_DH), F32),) * hg, (jnp.zeros((blk, 1), F32),) * hg)
    carry = visit(qi, strict_causal, carry, True)
    accs, _ = lax.fori_loop(0, qi, lambda i, c: visit(qi - 1 - i, None, c, False), carry)
    accs = [a + p for a, p in zip(accs, weighted_values(0))]
    o_ref[...] = jnp.concatenate(accs, axis=-1).astype(o_ref.dtype)


def _sb_prompt(bias, qh, kt, vt, *, blk, heads_per_step):
    n_seq, n_heads, seq_len, dh = qh.shape
    nq = seq_len // blk
    hg = heads_per_step
    kv_spec = pl.BlockSpec((1, hg, dh, seq_len), lambda b, h, qi: (b, h, 0, 0))
    return pl.pallas_call(
        functools.partial(_sb_prompt_kernel, blk=blk),
        grid=(n_seq, n_heads // hg, nq),
        in_specs=[pl.BlockSpec(memory_space=pltpu.SMEM),
                  pl.BlockSpec((1, hg, blk, dh), lambda b, h, qi: (b, h, qi, 0)),
                  kv_spec, kv_spec],
        out_specs=pl.BlockSpec((blk, hg * dh), lambda b, h, qi: (b * nq + qi, h)),
        out_shape=jax.ShapeDtypeStruct((n_seq * seq_len, n_heads * dh), BF16),
        scratch_shapes=[pltpu.VMEM((hg, nq, dh, blk), BF16), pltpu.VMEM((hg, nq, dh, blk), BF16),
                        pltpu.VMEM((hg, blk, blk), F32), pltpu.VMEM((hg, blk, blk), BF16)],
        compiler_params=_params(3), name="sb_prompt",
    )(bias, qh, kt, vt)


def _sb_sample_kernel(pt_ref, q_ref, bias_ref, kn_ref, vn_ref, *refs, pages_per_step, n_new):
    del pt_ref
    k_refs = refs[:pages_per_step]
    v_refs = refs[pages_per_step:2 * pages_per_step]
    o_ref, acc_ref, rest_ref = refs[2 * pages_per_step:]
    j = pl.program_id(1)
    n_rows = n_new * SB_HEADS
    page = kn_ref.shape[1]
    head_of_col = lax.broadcasted_iota(jnp.int32, (n_rows, SB_WIDTH), 1) // SB_DH
    head_of_row = lax.broadcasted_iota(jnp.int32, (n_rows, SB_WIDTH), 0) % SB_HEADS
    own_head = head_of_col == head_of_row
    q = jnp.where(own_head, q_ref[0], 0.0).astype(BF16)
    bias = bias_ref[...]

    def visit(kts, vts, mask):
        zs = [_mm(q, kt) + bias for kt in kts]
        weights, rests = _sb_weights(zs, _neg_upper(kts[0].shape[1]), [rest_ref[:, 0:1]], mask, chained=True)
        acc_ref[...] += functools.reduce(lambda a, b: a + b, [_mm_nt(a, vt) for a, vt in zip(weights, vts)])
        rest_ref[...] = jnp.broadcast_to(rests[0], rest_ref.shape)

    @pl.when(j == 0)
    def _():
        acc_ref[...] = jnp.zeros_like(acc_ref)
        rest_ref[...] = jnp.zeros_like(rest_ref)
        t_of_row = lax.broadcasted_iota(jnp.int32, (n_rows, page), 0) // SB_HEADS
        s_of_col = lax.broadcasted_iota(jnp.int32, (n_rows, page), 1)
        visit([kn_ref[0].T.astype(BF16)], [vn_ref[0].T.astype(BF16)], s_of_col < t_of_row)

    def page_pair(p_refs, i):
        lo_hi = [p_refs[i + 1][0, 0].reshape(SB_WIDTH, page), p_refs[i][0, 0].reshape(SB_WIDTH, page)]
        return jnp.concatenate(lo_hi, axis=1).astype(BF16)

    pairs = range(0, pages_per_step, 2)
    visit([page_pair(k_refs, i) for i in pairs], [page_pair(v_refs, i) for i in pairs], None)

    @pl.when(j == pl.num_programs(1) - 1)
    def _():
        acc = jnp.where(own_head, acc_ref[...], 0.0)
        for t in range(n_new):
            o_ref[0, t:t + 1, :] = jnp.sum(acc[t * SB_HEADS:(t + 1) * SB_HEADS, :], axis=0, keepdims=True)


def _sb_sample(page_table, q_rows, bias_rows, k_new, v_new, cache_kt, cache_vt, *, layer, pages_per_step, n_new):
    n_seq, n_pages = page_table.shape
    page = cache_kt.shape[4]
    n_rows = n_new * SB_HEADS
    steps = n_pages // pages_per_step

    def page_spec(i):
        return pl.BlockSpec((1, 1, SB_HEADS, SB_DH, page),
                            lambda b, j, pt: (layer, pt[b, n_pages - 1 - (j * pages_per_step + i)], 0, 0, 0))

    per_seq = lambda r, c: pl.BlockSpec((1, r, c), lambda b, j, pt: (b, 0, 0))
    grid_spec = pltpu.PrefetchScalarGridSpec(
        num_scalar_prefetch=1, grid=(n_seq, steps),
        in_specs=[per_seq(n_rows, SB_WIDTH),
                  pl.BlockSpec((n_rows, 1), lambda b, j, pt: (0, 0)),
                  per_seq(page, SB_WIDTH), per_seq(page, SB_WIDTH)]
                 + [page_spec(i) for i in range(pages_per_step)] * 2,
        out_specs=per_seq(n_new, SB_WIDTH),
        scratch_shapes=[pltpu.VMEM((n_rows, SB_WIDTH), F32), pltpu.VMEM((n_rows, LANES), F32)])
    return pl.pallas_call(
        functools.partial(_sb_sample_kernel, pages_per_step=pages_per_step, n_new=n_new),
        grid_spec=grid_spec,
        out_shape=jax.ShapeDtypeStruct((n_seq, n_new, SB_WIDTH), F32),
        compiler_params=_params(2), name="sb_sample",
    )(page_table, q_rows, bias_rows, k_new, v_new,
      *([cache_kt] * pages_per_step), *([cache_vt] * pages_per_step))


def _proj_res_kernel(x_ref, a_ref, b_ref, w_ref, o_ref):
    ka = a_ref.shape[1]
    o_ref[...] = (x_ref[...] + _mm(a_ref[...].astype(BF16), w_ref[0:ka, :])
                  + _mm(b_ref[...].astype(BF16), w_ref[ka:, :]))


def _proj_res(x, a, b, w, *, tm):
    m, d = x.shape
    row = lambda width: pl.BlockSpec((tm, width), lambda i: (i, 0))
    return pl.pallas_call(
        _proj_res_kernel, grid=(m // tm,),
        in_specs=[row(d), row(a.shape[1]), row(b.shape[1]), _const_spec(w.shape)],
        out_specs=row(d), out_shape=jax.ShapeDtypeStruct((m, d), F32),
        compiler_params=_params(1), name="proj_res",
    )(x, a, b, w)


def _ffn_chunks(xns, win_ref, wout_ref, cw_ref, cb_ref, o_ref, row_slices, conv_fn, d_ff):
    ck = MXU_N
    n_chunks = d_ff // ck

    def up(r, c):
        return (_mm(xns[r], win_ref[:, c * ck:(c + 1) * ck]),
                _mm(xns[r], win_ref[:, d_ff + c * ck:d_ff + (c + 1) * ck]))

    nxt = [up(r, 0) for r in range(len(xns))]
    for c in range(n_chunks):
        sl = slice(c * ck, (c + 1) * ck)
        for r, rows in enumerate(row_slices):
            g, u = nxt[r]
            if c + 1 < n_chunks:
                nxt[r] = up(r, c + 1)
            g_m2, g_m1 = conv_fn(r, g, sl)
            y = cb_ref[:, sl] + g_m2 * cw_ref[0:1, sl] + g_m1 * cw_ref[1:2, sl] + g * cw_ref[2:3, sl]
            hmid = (_gelu(y) * u).astype(BF16)
            o_ref[rows, :] += _mm(hmid, wout_ref[sl, :])


def _ffn_prompt_kernel(x_ref, g_ref, win_ref, cw_ref, cb_ref, wout_ref, fg_ref, o_ref, tail_out,
                       xn_ref, prev_ref, st_ref, *, tiles_per_seq, d_ff, final_norm, n_sub):
    i = pl.program_id(0)
    tm = x_ref.shape[0]
    ts = tm // n_sub
    halo = SUBLANES
    row_slices = [slice(r * ts, (r + 1) * ts) for r in range(n_sub)]

    @pl.when(i % tiles_per_seq == 0)
    def _():
        prev_ref[...] = jnp.zeros_like(prev_ref)

    xns = []
    for rows in row_slices:
        x = x_ref[rows, :]
        xn_ref[rows, :] = _rms(x, g_ref[...]).astype(BF16)
        o_ref[rows, :] = x
        xns.append(xn_ref[rows, :])

    def conv_fn(r, g, sl):
        if r == 0:
            st_ref[0:halo, :] = prev_ref[:, sl]
        lo = halo + r * ts
        st_ref[lo:lo + ts, :] = g
        if r == n_sub - 1:
            prev_ref[:, sl] = g[ts - halo:ts, :]
        return st_ref[lo - 2:lo - 2 + ts, :], st_ref[lo - 1:lo - 1 + ts, :]

    _ffn_chunks(xns, win_ref, wout_ref, cw_ref, cb_ref, o_ref, row_slices, conv_fn, d_ff)
    if final_norm:
        o_ref[...] = _rms(o_ref[...], fg_ref[...])

    @pl.when(i % tiles_per_seq == tiles_per_seq - 1)
    def _():
        tail_out[0] = prev_ref[...]


def _ffn_prompt(x, g, win, cw, cb, wout, fg, *, n_seq, seq_len, tm, final_norm, n_sub):
    m, d = x.shape
    d_ff = wout.shape[0]
    tps = seq_len // tm
    row = pl.BlockSpec((tm, d), lambda i: (i, 0))
    return pl.pallas_call(
        functools.partial(_ffn_prompt_kernel, tiles_per_seq=tps, d_ff=d_ff, final_norm=final_norm, n_sub=n_sub),
        grid=(m // tm,),
        in_specs=[row, _const_spec(g.shape), _const_spec(win.shape), _const_spec(cw.shape),
                  _const_spec(cb.shape), _const_spec(wout.shape), _const_spec(fg.shape)],
        out_specs=[row, pl.BlockSpec((1, SUBLANES, d_ff), lambda i: (i // tps, 0, 0))],
        out_shape=[jax.ShapeDtypeStruct((m, d), F32), jax.ShapeDtypeStruct((n_seq, SUBLANES, d_ff), F32)],
        scratch_shapes=[pltpu.VMEM((tm, d), BF16), pltpu.VMEM((SUBLANES, d_ff), F32),
                        pltpu.VMEM((SUBLANES + tm, MXU_N), F32)],
        compiler_params=_params(1), name="ffn_prompt",
    )(x, g, win, cw, cb, wout, fg)


def _ffn_sample_kernel(x_ref, g_ref, win_ref, cw_ref, cb_ref, wout_ref, fg_ref, st_ref, o_ref, tail_out,
                       *, n_seq, d_ff, final_norm):
    rows = x_ref.shape[0]
    x = x_ref[...]
    xn = _rms(x, g_ref[...]).astype(BF16)
    o_ref[...] = x

    def conv_fn(r, g, sl):
        ext = jnp.concatenate([st_ref[:, sl], g], axis=0)
        tail_out[:, sl] = g[rows - 2 * n_seq:rows, :]
        return ext[0:rows, :], ext[n_seq:n_seq + rows, :]

    _ffn_chunks([xn], win_ref, wout_ref, cw_ref, cb_ref, o_ref, [slice(0, rows)], conv_fn, d_ff)
    if final_norm:
        o_ref[...] = _rms(o_ref[...], fg_ref[...])


def _ffn_sample(x, g, win, cw, cb, wout, fg, st, *, n_seq, final_norm):
    d_ff = wout.shape[0]
    return pl.pallas_call(
        functools.partial(_ffn_sample_kernel, n_seq=n_seq, d_ff=d_ff, final_norm=final_norm),
        out_shape=[jax.ShapeDtypeStruct(x.shape, F32), jax.ShapeDtypeStruct(st.shape, F32)],
        compiler_params=pltpu.CompilerParams(vmem_limit_bytes=VMEM_LIMIT), name="ffn_sample",
    )(x, g, win, cw, cb, wout, fg, st)


def _layernorm_stats(v_ref, width, n_groups):
    gw = width // n_groups
    s1 = 0.0
    for gi in range(n_groups):
        s1 = s1 + jnp.sum(v_ref[:, gi * gw:(gi + 1) * gw], axis=-1, keepdims=True)
    mean = s1 / width
    s2 = 0.0
    for gi in range(n_groups):
        xc = v_ref[:, gi * gw:(gi + 1) * gw] - mean
        s2 = s2 + jnp.sum(xc * xc, axis=-1, keepdims=True)
    return mean, lax.rsqrt(s2 / width + EPS)


def _cm_prompt_kernel(x_ref, g_ref, win_ref, lng_ref, lnb_ref, ws_ref, bst_ref, wout_ref, o_ref,
                      xn_ref, v_ref, *, width):
    tm = x_ref.shape[0]
    gw = width // CM_GROUPS
    x = x_ref[...]
    xn_ref[...] = _rms(x, g_ref[...]).astype(BF16)
    o_ref[...] = x
    proj = lambda col0, gi: _mm(xn_ref[...], win_ref[:, col0 + gi * gw:col0 + (gi + 1) * gw])
    nxt = proj(width, 0)
    for gi in range(CM_GROUPS):
        cur, nxt = nxt, (proj(width, gi + 1) if gi + 1 < CM_GROUPS else proj(0, 0))
        v_ref[:, gi * gw:(gi + 1) * gw] = _gelu(cur)
    mean, rstd = _layernorm_stats(v_ref, width, CM_GROUPS)
    rows = lax.broadcasted_iota(jnp.int32, (CM_CHUNK, CM_CHUNK), 0)
    cols = lax.broadcasted_iota(jnp.int32, (CM_CHUNK, CM_CHUNK), 1)
    tril = rows >= cols
    for gi in range(CM_GROUPS):
        sl = slice(gi * gw, (gi + 1) * gw)
        u_pre, nxt = nxt, (proj(0, gi + 1) if gi + 1 < CM_GROUPS else None)
        vn = ((v_ref[:, sl] - mean) * rstd * lng_ref[:, sl] + lnb_ref[:, sl]).astype(BF16)
        wsg = jnp.where(tril, ws_ref[gi], 0.0).astype(BF16)
        bias = bst_ref[:, gi:gi + 1]
        mixed = jnp.concatenate(
            [_mm(wsg, vn[c * CM_CHUNK:(c + 1) * CM_CHUNK, :]) + bias for c in range(tm // CM_CHUNK)], axis=0)
        o_ref[...] += _mm((_gelu(u_pre) * mixed).astype(BF16), wout_ref[sl, :])


def _cm_prompt(x, g, win, lng, lnb, ws, bst, wout, *, tm):
    m, d = x.shape
    width = wout.shape[0]
    row = pl.BlockSpec((tm, d), lambda i: (i, 0))
    return pl.pallas_call(
        functools.partial(_cm_prompt_kernel, width=width),
        grid=(m // tm,),
        in_specs=[row] + [_const_spec(a.shape) for a in (g, win, lng, lnb, ws, bst, wout)],
        out_specs=row, out_shape=jax.ShapeDtypeStruct((m, d), F32),
        scratch_shapes=[pltpu.VMEM((tm, d), BF16), pltpu.VMEM((tm, width), F32)],
        compiler_params=_params(1), name="cm_prompt",
    )(x, g, win, lng, lnb, ws, bst, wout)


def _cm_sample_kernel(x_ref, g_ref, win_ref, lng_ref, lnb_ref, wexp_ref, bexp_ref, wout_ref, o_ref, v_out,
                      *, n_seq, n_new, width):
    gw = width // CM_GROUPS
    x = x_ref[...]
    xn = _rms(x, g_ref[...]).astype(BF16)
    for gi in range(CM_GROUPS):
        v_out[:, gi * gw:(gi + 1) * gw] = _gelu(_mm(xn, win_ref[:, width + gi * gw:width + (gi + 1) * gw]))
    mean, rstd = _layernorm_stats(v_out, width, CM_GROUPS)
    acc = x
    for gi in range(CM_GROUPS):
        sl = slice(gi * gw, (gi + 1) * gw)
        vn = (v_out[:, sl] - mean) * rstd * lng_ref[:, sl] + lnb_ref[:, sl]
        v_out[:, sl] = vn
        mixed = []
        for t in range(n_new):
            mt = bexp_ref[t:t + 1, sl]
            for s in range(t + 1):
                mt = mt + wexp_ref[t * n_new + s:t * n_new + s + 1, sl] * vn[s * n_seq:(s + 1) * n_seq, :]
            mixed.append(mt)
        u = _gelu(_mm(xn, win_ref[:, sl]))
        acc = acc + _mm((u * jnp.concatenate(mixed, axis=0)).astype(BF16), wout_ref[sl, :])
    o_ref[...] = acc


def _cm_sample(x, g, win, lng, lnb, wexp, bexp, wout, *, n_seq, n_new):
    width = wout.shape[0]
    return pl.pallas_call(
        functools.partial(_cm_sample_kernel, n_seq=n_seq, n_new=n_new, width=width),
        out_shape=[jax.ShapeDtypeStruct(x.shape, F32), jax.ShapeDtypeStruct((x.shape[0], width), F32)],
        compiler_params=pltpu.CompilerParams(vmem_limit_bytes=VMEM_LIMIT), name="cm_sample",
    )(x, g, win, lng, lnb, wexp, bexp, wout)


def _time_major(a):
    return jnp.swapaxes(a, 0, 1).reshape((a.shape[0] * a.shape[1],) + a.shape[2:])


def _batch_major(a, n_seq):
    return jnp.swapaxes(a.reshape((a.shape[0] // n_seq, n_seq) + a.shape[1:]), 0, 1)


def _pad_lanes(a):
    return jnp.pad(a, [(0, 0)] * (a.ndim - 1) + [(0, LANES - a.shape[-1])])


def kernel(x_prompt, x_sample, cache_k, cache_v, page_table, state_mlstm_c, state_mlstm_n, state_mlstm_m, state_mlstm_conv, state_ffn_conv, norm_mix_g, norm_ffn_g, norm_final_g, w_in_even, ml_conv_w, ml_conv_b, ml_gate_b, ml_head_g, sb_logit_b, w_out_even, w_in_odd, cm_ln_g, cm_ln_b, cm_spatial_w, cm_spatial_b, w_out_odd, ffn_w_in, ffn_conv_w, ffn_conv_b, ffn_w_out):
    n_pb, seq_p, d = x_prompt.shape
    n_sb, seq_s, _ = x_sample.shape
    depth = norm_mix_g.shape[0]
    d_ff = ffn_w_out.shape[1]
    hp = x_prompt.reshape(n_pb * seq_p, d)
    hs = _time_major(x_sample)
    fg = norm_final_g.reshape(1, d)

    kp_l, vp_l, ks_l, vs_l = [], [], [], []
    cp_l, np_l, mp_l, bp_l = [], [], [], []
    cs_l, ns_l, ms_l, bs_l = [], [], [], []
    cmv_l, fp_l, fs_l = [], [], []
    for layer in range(depth):
        mix_g = norm_mix_g[layer].reshape(1, d)
        if layer % 2 == 0:
            e = layer // 2
            w = w_in_even[e]
            gate_lo = _C_O + ML_WIDTH
            gate_hi = gate_lo + 2 * ML_HEADS
            w_all = jnp.concatenate([w[:, :gate_lo], w[:, gate_hi:], _pad_lanes(w[:, gate_lo:gate_hi])],
                                    axis=1).astype(BF16)
            w_kvt = w[:, gate_hi + SB_WIDTH:].T.astype(BF16)
            cw, cb = ml_conv_w[e], ml_conv_b[e].reshape(1, -1)
            gb = _pad_lanes(ml_gate_b[e].reshape(1, -1))
            hg = ml_head_g[e].reshape(1, -1)
            w_out = w_out_even[e].astype(BF16)
            qk, vm, om, gates, qh, kt, vt = _even_in(
                hp, mix_g, w_all, w_kvt, n_seq=n_pb, seq_len=seq_p, tm=512, head_major=True)
            h_ml, c_p, n_p, m_p, tail_p = _mlstm_prompt(qk, vm, om, gates, cw, cb, gb, hg, n_seq=n_pb, seq_len=seq_p)
            h_sb = _sb_prompt(sb_logit_b[e], qh, kt, vt, blk=256, heads_per_step=4)
            hp = _proj_res(hp, h_ml, h_sb, w_out, tm=512)
            kp_l.append(jnp.transpose(kt, (0, 3, 1, 2)))
            vp_l.append(jnp.transpose(vt, (0, 3, 1, 2)))
            cp_l.append(c_p)
            np_l.append(n_p[:, :ML_HEADS])
            mp_l.append(m_p[:, :ML_HEADS, 0])
            bp_l.append(tail_p[:, SUBLANES - (ML_CONV - 1):])
            qk, vm, om, gates, sq, sk, sv = _even_in(
                hs, mix_g, w_all, w_kvt, n_seq=n_sb, seq_len=seq_s, tm=n_sb * seq_s, head_major=False)
            h_ml, c_s, n_s, m_s = _mlstm_sample(
                qk, vm, om, gates, _time_major(state_mlstm_conv[e]), cw, cb, gb, hg,
                state_mlstm_c[e], state_mlstm_n[e].reshape(n_sb, ML_WIDTH), _pad_lanes(state_mlstm_m[e]),
                n_seq=n_sb, n_new=seq_s)
            page = cache_k.shape[2]
            pad_keys = lambda a: jnp.pad(_batch_major(a, n_sb), ((0, 0), (0, page - seq_s), (0, 0)))
            q_rows = jnp.broadcast_to(_batch_major(sq, n_sb)[:, :, None, :],
                                      (n_sb, seq_s, SB_HEADS, SB_WIDTH)).reshape(n_sb, seq_s * SB_HEADS, SB_WIDTH)
            bias_rows = jnp.tile(sb_logit_b[e], seq_s).reshape(seq_s * SB_HEADS, 1)
            pool_view = lambda c: jnp.transpose(c, (0, 1, 3, 4, 2))
            h_sb = _sb_sample(page_table, q_rows, bias_rows, pad_keys(sk), pad_keys(sv),
                              pool_view(cache_k), pool_view(cache_v), layer=e, pages_per_step=16, n_new=seq_s)
            hs = _proj_res(hs, h_ml, _time_major(h_sb), w_out, tm=n_sb * seq_s)
            ks_l.append(_batch_major(sk, n_sb).reshape(n_sb, seq_s, SB_HEADS, SB_DH))
            vs_l.append(_batch_major(sv, n_sb).reshape(n_sb, seq_s, SB_HEADS, SB_DH))
            cs_l.append(c_s)
            ns_l.append(n_s.reshape(n_sb, ML_HEADS, ML_DH))
            ms_l.append(m_s[:, :ML_HEADS])
            bs_l.append(_batch_major(qk[(seq_s - (ML_CONV - 1)) * n_sb:], n_sb))
        else:
            o = layer // 2
            win = w_in_odd[o].astype(BF16)
            wout = w_out_odd[o].astype(BF16)
            lng, lnb = cm_ln_g[o].reshape(1, -1), cm_ln_b[o].reshape(1, -1)
            width = wout.shape[0]
            gw = width // CM_GROUPS
            hp = _cm_prompt(hp, mix_g, win, lng, lnb, cm_spatial_w[o], cm_spatial_b[o].T, wout, tm=512)
            wexp = jnp.repeat(cm_spatial_w[o][:, :seq_s, :seq_s].reshape(CM_GROUPS, seq_s * seq_s).T, gw, axis=1)
            bexp = jnp.repeat(cm_spatial_b[o][:, :seq_s].T, gw, axis=1)
            hs, v_rows = _cm_sample(hs, mix_g, win, lng, lnb, wexp, bexp, wout, n_seq=n_sb, n_new=seq_s)
            cmv_l.append(_batch_major(v_rows, n_sb))
        ffn_g = norm_ffn_g[layer].reshape(1, d)
        win = ffn_w_in[layer].astype(BF16)
        wout = ffn_w_out[layer].astype(BF16)
        cw, cb = ffn_conv_w[layer], ffn_conv_b[layer].reshape(1, -1)
        last = layer == depth - 1
        hp, tail_p = _ffn_prompt(hp, ffn_g, win, cw, cb, wout, fg, n_seq=n_pb, seq_len=seq_p, tm=1024,
                                 final_norm=last, n_sub=1)
        hs, tail_s = _ffn_sample(hs, ffn_g, win, cw, cb, wout, fg, _time_major(state_ffn_conv[layer]),
                                 n_seq=n_sb, final_norm=last)
        fp_l.append(tail_p[:, SUBLANES - (FFN_CONV - 1):])
        fs_l.append(_batch_major(tail_s, n_sb))

    return (hp.reshape(n_pb, seq_p, d), _batch_major(hs, n_sb),
            jnp.stack(kp_l), jnp.stack(vp_l), jnp.stack(ks_l), jnp.stack(vs_l),
            jnp.stack(cp_l), jnp.stack(np_l), jnp.stack(mp_l), jnp.stack(bp_l),
            jnp.stack(cs_l), jnp.stack(ns_l), jnp.stack(ms_l), jnp.stack(bs_l),
            jnp.stack(cmv_l), jnp.stack(fp_l), jnp.stack(fs_l))
```

```python
import functools

import jax
import jax.numpy as jnp
from jax import lax
from jax.experimental import pallas as pl
from jax.experimental.pallas import tpu as pltpu

F32 = jnp.float32
BF16 = jnp.bfloat16
EPS = 1e-6
LOG2E = 1.4426950408889634

ML_HEADS = 4
ML_DH = 128
ML_WIDTH = ML_HEADS * ML_DH
ML_CONV = 4
ML_CHUNK = 128
SB_HEADS = 8
SB_DH = 64
SB_WIDTH = SB_HEADS * SB_DH
SB_SCALE = SB_DH ** -0.5
CM_GROUPS = 8
CM_CHUNK = 128
CM_OUT_GROUPS = 2
FFN_CONV = 3
LANES = 128
SUBLANES = 8
MXU_N = 256
FFN_CHUNK = 2 * MXU_N
VMEM_LIMIT = 56 * 1024 * 1024


def _mm(a, b):
    return jnp.dot(a, b, preferred_element_type=F32)


def _mm_nt(a, b):
    return lax.dot_general(a, b, (((1,), (1,)), ((), ())), preferred_element_type=F32)


def _rms(x, g):
    return x * lax.rsqrt(jnp.mean(x * x, axis=-1, keepdims=True) + EPS) * g


def _sigmoid(x):
    return 1.0 / (1.0 + jnp.exp(-x))


def _softplus(x):
    return jnp.maximum(x, 0.0) + jnp.log1p(jnp.exp(-jnp.abs(x)))


def _gelu(x):
    c = 0.7978845608028654
    return 0.5 * x * (1.0 + jnp.tanh(c * (x + 0.044715 * (x * x * x))))


def _split_hi_lo(x):
    hi = x.astype(BF16)
    lo = (x - hi.astype(F32)).astype(BF16)
    return hi, lo


def _params(n_axes):
    return pltpu.CompilerParams(dimension_semantics=("arbitrary",) * n_axes,
                                vmem_limit_bytes=VMEM_LIMIT)


def _const_spec(shape):
    nd = len(shape)
    return pl.BlockSpec(shape, lambda *_: (0,) * nd, pipeline_mode=pl.Buffered(1))


_C_QK = 0
_C_V = 2 * ML_WIDTH
_C_O = _C_V + ML_WIDTH
_C_SQ = _C_O + ML_WIDTH
_C_SK = _C_SQ + SB_WIDTH
_C_SV = _C_SK + SB_WIDTH
_C_G = _C_SV + SB_WIDTH
_C_END = _C_G + LANES


def _even_in_kernel(x_ref, g_ref, w_ref, wkvt_ref, *out_refs, head_major):
    xn = _rms(x_ref[...], g_ref[...]).astype(BF16)
    qk_ref, v_ref, o_ref, gates_ref, sq_ref, sk_ref, sv_ref = out_refs
    qk_ref[...] = _mm(xn, w_ref[:, _C_QK:_C_V])
    v_ref[...] = _mm(xn, w_ref[:, _C_V:_C_O])
    o_ref[...] = _mm(xn, w_ref[:, _C_O:_C_SQ])
    gates_ref[...] = _mm(xn, w_ref[:, _C_G:_C_END])
    sq = _mm(xn, w_ref[:, _C_SQ:_C_SK]) * SB_SCALE
    if head_major:
        tm = sq.shape[0]
        for h in range(SB_HEADS):
            sq_ref[0, h] = sq[:, h * SB_DH:(h + 1) * SB_DH].astype(BF16)
        sk_ref[0] = _mm_nt(wkvt_ref[0:SB_WIDTH, :], xn).reshape(SB_HEADS, SB_DH, tm)
        sv_ref[0] = _mm_nt(wkvt_ref[SB_WIDTH:, :], xn).reshape(SB_HEADS, SB_DH, tm)
    else:
        sq_ref[...] = sq
        sk_ref[...] = _mm(xn, w_ref[:, _C_SK:_C_SV])
        sv_ref[...] = _mm(xn, w_ref[:, _C_SV:_C_G])


def _even_in(x, g, w, wkvt, *, n_seq, seq_len, tm, head_major):
    m, d = x.shape
    nt = m // tm
    row = lambda width: pl.BlockSpec((tm, width), lambda i: (i, 0))
    out_shape = [jax.ShapeDtypeStruct((m, 2 * ML_WIDTH), F32),
                 jax.ShapeDtypeStruct((m, ML_WIDTH), F32),
                 jax.ShapeDtypeStruct((m, ML_WIDTH), F32),
                 jax.ShapeDtypeStruct((m, LANES), F32)]
    out_specs = [row(2 * ML_WIDTH), row(ML_WIDTH), row(ML_WIDTH), row(LANES)]
    if head_major:
        tps = seq_len // tm
        out_shape += [jax.ShapeDtypeStruct((n_seq, SB_HEADS, seq_len, SB_DH), BF16)]
        out_specs += [pl.BlockSpec((1, SB_HEADS, tm, SB_DH), lambda i: (i // tps, 0, i % tps, 0))]
        out_shape += [jax.ShapeDtypeStruct((n_seq, SB_HEADS, SB_DH, seq_len), F32)] * 2
        out_specs += [pl.BlockSpec((1, SB_HEADS, SB_DH, tm), lambda i: (i // tps, 0, 0, i % tps))] * 2
    else:
        out_shape += [jax.ShapeDtypeStruct((m, SB_WIDTH), F32)] * 3
        out_specs += [row(SB_WIDTH)] * 3
    return pl.pallas_call(
        functools.partial(_even_in_kernel, head_major=head_major),
        grid=(nt,),
        in_specs=[row(d), _const_spec((1, d)), _const_spec(w.shape), _const_spec(wkvt.shape)],
        out_specs=out_specs, out_shape=out_shape,
        compiler_params=_params(1), name="even_in",
    )(x, g, w, wkvt)


def _log_sigmoid(x):
    return jnp.minimum(x, 0.0) - jnp.log1p(jnp.exp(-jnp.abs(x)))


def _mlstm_heads(heads, mask):
    qk = [_mm_nt(h["q"], h["k"]) for h in heads]
    ss, w_inters, floors = [], [], []
    for h, qk_h in zip(heads, qk):
        dmat = jnp.where(mask, h["bc"] - h["br"] + h["li_r"], -jnp.inf)
        inter = h["bc"] + h["m_rows"]
        m_t = jnp.maximum(inter, jnp.max(dmat, axis=-1, keepdims=True))
        ss.append(qk_h * jnp.exp(dmat - m_t))
        w_inters.append(jnp.exp(inter - m_t))
        floors.append(jnp.exp(-m_t))
    svs = [_mm(s.astype(BF16), h["v"]) for s, h in zip(ss, heads)]
    outs = []
    for h, s, sv, w_inter, floor in zip(heads, ss, svs, w_inters, floors):
        num = w_inter * h["cq"] + sv
        den = w_inter * h["nq"] + jnp.sum(s, axis=-1, keepdims=True)
        ho = _sigmoid(h["o"]) * (num / jnp.maximum(jnp.abs(den), floor))
        outs.append(ho * lax.rsqrt(jnp.mean(ho * ho, axis=-1, keepdims=True) + EPS) * h["hg"])
    return outs


def _mlstm_prompt_kernel(qk_ref, v_ref, o_ref, gates_ref, cw_ref, cb_ref, gb_ref, hg_ref,
                         h_ref, c_out, n_out, m_out, tail_out,
                         xp_ref, c_ref, n_ref, m_ref, *, n_chunks):
    ci = pl.program_id(1)
    L = ML_CHUNK
    halo = SUBLANES

    @pl.when(ci == 0)
    def _():
        xp_ref[0:halo, :] = jnp.zeros((halo, 2 * ML_WIDTH), F32)
        c_ref[...] = jnp.zeros_like(c_ref)
        n_ref[...] = jnp.zeros_like(n_ref)
        m_ref[...] = jnp.zeros_like(m_ref)

    xp_ref[halo:halo + L, :] = qk_ref[...]
    y = cb_ref[...]
    for i in range(ML_CONV):
        off = halo - (ML_CONV - 1) + i
        y = y + xp_ref[off:off + L, :] * cw_ref[i:i + 1, :]
    xp_ref[0:halo, :] = qk_ref[L - halo:L, :]
    qk = y * _sigmoid(y)

    gl = gates_ref[...] + gb_ref[...]
    logf = _log_sigmoid(gl)
    rows = lax.broadcasted_iota(jnp.int32, (L, L), 0)
    cols = lax.broadcasted_iota(jnp.int32, (L, L), 1)
    causal = rows >= cols
    tri = jnp.where(causal, 1.0, 0.0).astype(BF16)
    bcol = _cumsum_rows(tri, logf)
    bT = bcol.T
    glT = gl.T

    outs = []
    for h in range(ML_HEADS):
        sl = slice(h * ML_DH, (h + 1) * ML_DH)
        q = qk[:, sl]
        k = qk[:, ML_WIDTH + h * ML_DH:ML_WIDTH + (h + 1) * ML_DH] * (ML_DH ** -0.5)
        v = v_ref[:, sl]
        bc = bcol[:, ML_HEADS + h:ML_HEADS + h + 1]
        m_prev = m_ref[h:h + 1, 0:1]
        c_prev = c_ref[h]
        n_prev = n_ref[h:h + 1, :]
        q_bf, k_bf = q.astype(BF16), k.astype(BF16)
        head = dict(
            q=q_bf, k=k_bf, v=v.astype(BF16), o=o_ref[:, sl], hg=hg_ref[:, sl], bc=bc,
            br=bT[ML_HEADS + h:ML_HEADS + h + 1, :], li_r=glT[h:h + 1, :],
            m_rows=jnp.broadcast_to(m_prev, (L, 1)),
            cq=_mm_nt(q_bf, c_prev.astype(BF16)), nq=jnp.sum(q * n_prev, axis=-1, keepdims=True))
        outs += _mlstm_heads([head], causal)
        b_last = bc[L - 1:L, :]
        g = b_last - bc + gl[:, h:h + 1]
        m_new = jnp.maximum(b_last + m_prev, jnp.max(g, axis=0, keepdims=True))
        a_s = jnp.exp(g - m_new)
        a_c = jnp.exp(b_last + m_prev - m_new)
        c_ref[h] = a_c * c_prev + _mm((a_s * v).T.astype(BF16), k_bf)
        n_ref[h:h + 1, :] = a_c * n_prev + jnp.sum(a_s * k, axis=0, keepdims=True)
        m_ref[h:h + 1, :] = jnp.broadcast_to(m_new, (1, LANES))
    h_ref[...] = jnp.concatenate(outs, axis=-1).astype(h_ref.dtype)

    @pl.when(ci == n_chunks - 1)
    def _():
        c_out[0] = c_ref[...]
        n_out[0] = n_ref[...]
        m_out[0] = m_ref[...]
        tail_out[0] = qk_ref[L - halo:L, :]


def _cumsum_rows(tri, x):
    hi = x.astype(BF16)
    r1 = x - hi.astype(F32)
    mid = r1.astype(BF16)
    lo = (r1 - mid.astype(F32)).astype(BF16)
    return _mm(tri, hi) + _mm(tri, mid) + _mm(tri, lo)


def _mlstm_prompt(qk, v, o, gates, cw, cb, gb, hg, *, n_seq, seq_len):
    nc = seq_len // ML_CHUNK
    L = ML_CHUNK
    row = lambda width: pl.BlockSpec((L, width), lambda b, c: (b * nc + c, 0))
    per_seq = lambda *dims: pl.BlockSpec((1,) + dims, lambda b, c: (b,) + (0,) * len(dims))
    return pl.pallas_call(
        functools.partial(_mlstm_prompt_kernel, n_chunks=nc),
        grid=(n_seq, nc),
        in_specs=[row(2 * ML_WIDTH), row(ML_WIDTH), row(ML_WIDTH), row(LANES),
                  _const_spec(cw.shape), _const_spec(cb.shape), _const_spec(gb.shape), _const_spec(hg.shape)],
        out_specs=[row(ML_WIDTH), per_seq(ML_HEADS, ML_DH, ML_DH), per_seq(SUBLANES, ML_DH),
                   per_seq(SUBLANES, LANES), per_seq(SUBLANES, 2 * ML_WIDTH)],
        out_shape=[jax.ShapeDtypeStruct((n_seq * seq_len, ML_WIDTH), BF16),
                   jax.ShapeDtypeStruct((n_seq, ML_HEADS, ML_DH, ML_DH), F32),
                   jax.ShapeDtypeStruct((n_seq, SUBLANES, ML_DH), F32),
                   jax.ShapeDtypeStruct((n_seq, SUBLANES, LANES), F32),
                   jax.ShapeDtypeStruct((n_seq, SUBLANES, 2 * ML_WIDTH), F32)],
        scratch_shapes=[pltpu.VMEM((SUBLANES + L, 2 * ML_WIDTH), F32),
                        pltpu.VMEM((ML_HEADS, ML_DH, ML_DH), F32),
                        pltpu.VMEM((SUBLANES, ML_DH), F32),
                        pltpu.VMEM((SUBLANES, LANES), F32)],
        compiler_params=_params(2), name="mlstm_prompt",
    )(qk, v, o, gates, cw, cb, gb, hg)


def _mlstm_sample_kernel(qk_ref, v_ref, o_ref, gates_ref, st_ref, cw_ref, cb_ref, gb_ref, hg_ref,
                         c0_ref, n0_ref, m0_ref,
                         h_ref, c_out, n_out, m_out, ac_ref, *, n_seq, n_new):
    L = n_seq * n_new
    ext = jnp.concatenate([st_ref[...], qk_ref[...]], axis=0)
    y = cb_ref[...]
    for i in range(ML_CONV):
        y = y + ext[i * n_seq:i * n_seq + L, :] * cw_ref[i:i + 1, :]
    qk = y * _sigmoid(y)

    gl = gates_ref[...] + gb_ref[...]
    logf = _log_sigmoid(gl)
    tblk = lambda a, t: a[t * n_seq:(t + 1) * n_seq, :]
    b_t = [tblk(logf, 0)]
    for t in range(1, n_new):
        b_t.append(b_t[-1] + tblk(logf, t))
    bcol = jnp.concatenate(b_t, axis=0)
    bT = bcol.T
    glT = gl.T
    rows = lax.broadcasted_iota(jnp.int32, (L, L), 0)
    cols = lax.broadcasted_iota(jnp.int32, (L, L), 1)
    same_seq_causal = (rows >= cols) & (((rows - cols) % n_seq) == 0)
    row_seq = lax.broadcasted_iota(jnp.int32, (L, ML_DH), 0) % n_seq
    col_seq = lax.broadcasted_iota(jnp.int32, (ML_DH, L), 1) % n_seq
    m0 = m0_ref[...]
    m_out[...] = m0

    heads = []
    for h in range(ML_HEADS):
        sl = slice(h * ML_DH, (h + 1) * ML_DH)
        q = qk[:, sl]
        k = qk[:, ML_WIDTH + h * ML_DH:ML_WIDTH + (h + 1) * ML_DH] * (ML_DH ** -0.5)
        v = v_ref[:, sl]
        q_bf = q.astype(BF16)
        k_bf = k.astype(BF16)
        bc = bcol[:, ML_HEADS + h:ML_HEADS + h + 1]
        li_c = gl[:, h:h + 1]
        m0_h = m0[:, h:h + 1]
        m_rows = jnp.concatenate([m0_h] * n_new, axis=0)
        n0_h = n0_ref[:, sl]
        nq = jnp.sum(q * jnp.concatenate([n0_h] * n_new, axis=0), axis=-1, keepdims=True)

        b_last = tblk(bc, n_new - 1)
        g_t = [b_last - tblk(bc, t) + tblk(li_c, t) for t in range(n_new)]
        m_new = b_last + m0_h
        for t in range(n_new):
            m_new = jnp.maximum(m_new, g_t[t])
        a_s_t = [jnp.exp(g_t[t] - m_new) for t in range(n_new)]
        a_c = jnp.exp(b_last + m0_h - m_new)
        n_new_h = a_c * n0_h
        for t in range(n_new):
            n_new_h = n_new_h + a_s_t[t] * tblk(k, t)
        n_out[:, sl] = n_new_h
        m_out[:, h:h + 1] = m_new
        ac_ref[...] = jnp.broadcast_to(a_c, (n_seq, LANES))
        av_t = (jnp.concatenate(a_s_t, axis=0) * v).T

        def per_seq(b, cq):
            c_prev = c0_ref[b, h]
            part = _mm_nt(q_bf, c_prev.astype(BF16))
            cq = jnp.where(row_seq == b, part, cq)
            upd = _mm(jnp.where(col_seq == b, av_t, 0.0).astype(BF16), k_bf)
            c_out[b, h] = ac_ref[pl.ds(b, 1), :] * c_prev + upd
            return cq

        cq = lax.fori_loop(0, n_seq, per_seq, jnp.zeros((L, ML_DH), F32))
        heads.append(dict(q=q_bf, k=k_bf, v=v.astype(BF16), o=o_ref[:, sl], hg=hg_ref[:, sl], bc=bc,
                          br=bT[ML_HEADS + h:ML_HEADS + h + 1, :], li_r=glT[h:h + 1, :],
                          m_rows=m_rows, cq=cq, nq=nq))
    h_ref[...] = jnp.concatenate(_mlstm_heads(heads, same_seq_causal), axis=-1).astype(h_ref.dtype)


def _mlstm_sample(qk, v, o, gates, st, cw, cb, gb, hg, c0, n0, m0, *, n_seq, n_new):
    L = n_seq * n_new
    return pl.pallas_call(
        functools.partial(_mlstm_sample_kernel, n_seq=n_seq, n_new=n_new),
        out_shape=[jax.ShapeDtypeStruct((L, ML_WIDTH), BF16),
                   jax.ShapeDtypeStruct(c0.shape, F32),
                   jax.ShapeDtypeStruct(n0.shape, F32),
                   jax.ShapeDtypeStruct(m0.shape, F32)],
        scratch_shapes=[pltpu.VMEM((n_seq, LANES), F32)],
        compiler_params=pltpu.CompilerParams(vmem_limit_bytes=VMEM_LIMIT), name="mlstm_sample",
    )(qk, v, o, gates, st, cw, cb, gb, hg, c0, n0, m0)


def _neg_upper(n):
    rows = lax.broadcasted_iota(jnp.int32, (n, n), 0)
    cols = lax.broadcasted_iota(jnp.int32, (n, n), 1)
    return jnp.where(rows > cols, -1.0, 0.0).astype(BF16)


def _sb_weights(zs, neg_upper, rests, mask, chained=False):
    log_betas, totals, his, los = [], [], [], []
    for z in zs:
        z2 = z * LOG2E
        sp = jnp.maximum(z2, 0.0) + jnp.log2(1.0 + jnp.exp2(-jnp.abs(z2)))
        if mask is not None:
            sp = jnp.where(mask, sp, 0.0)
        hi, lo = _split_hi_lo(sp)
        log_betas.append(z2 - sp), totals.append(sp[:, 0:1]), his.append(hi), los.append(lo)
    upper2 = jnp.concatenate([neg_upper, neg_upper], axis=0)
    css = [_mm(jnp.concatenate([hi, lo], axis=1), upper2) for hi, lo in zip(his, los)]
    weights, new_rests = [], []
    for i, (lb, cs, tot) in enumerate(zip(log_betas, css, totals)):
        rest = new_rests[-1] if (chained and i) else rests[i]
        a = jnp.exp2(lb + cs + rest)
        if mask is not None:
            a = jnp.where(mask, a, 0.0)
        weights.append(a.astype(BF16))
        new_rests.append(rest + cs[:, 0:1] - tot)
    return weights, (new_rests[-1:] if chained else new_rests)


def _sb_prompt_kernel(bias_ref, q_ref, kt_ref, vt_ref, o_ref, kb_ref, vb_ref, z_ref, a_ref, *, blk):
    hg = q_ref.shape[1]
    hi_ = pl.program_id(1)
    qi = pl.program_id(2)
    n_blk = kb_ref.shape[1]
    heads = range(hg)

    @pl.when(qi == 0)
    def _():
        for hh in heads:
            for j in range(n_blk):
                kb_ref[hh, j] = kt_ref[0, hh, :, j * blk:(j + 1) * blk].astype(BF16)
                vb_ref[hh, j] = vt_ref[0, hh, :, j * blk:(j + 1) * blk].astype(BF16)

    neg_upper = _neg_upper(blk)
    strict_causal = (lax.broadcasted_iota(jnp.int32, (blk, blk), 1)
                     < lax.broadcasted_iota(jnp.int32, (blk, blk), 0))

    def logits(kb):
        return [_mm(q_ref[0, hh], kb_ref[hh, kb]) + bias_ref[hi_ * hg + hh] for hh in heads]

    def weighted_values(kb):
        return [_mm_nt(a_ref[hh], vb_ref[hh, kb]) for hh in heads]

    def visit(kb, mask, carry, first):
        accs, rests = carry
        zs = [z_ref[hh] for hh in heads]
        z_next = logits(jnp.maximum(kb - 1, 0))
        if not first:
            accs = tuple(a + p for a, p in zip(accs, weighted_values(kb + 1)))
        for hh in heads:
            z_ref[hh] = z_next[hh]
        weights, rests = _sb_weights(zs, neg_upper, rests, mask)
        for hh in heads:
            a_ref[hh] = weights[hh]
        return accs, tuple(rests)

    for hh, z in enumerate(logits(qi)):
        z_ref[hh] = z
    carry = ((jnp.zeros((blk, SB_DH), F32),) * hg, (jnp.zeros((blk, 1), F32),) * hg)
    carry = visit(qi, strict_causal, carry, True)
    accs, _ = lax.fori_loop(0, qi, lambda i, c: visit(qi - 1 - i, None, c, False), carry)
    accs = [a + p for a, p in zip(accs, weighted_values(0))]
    o_ref[...] = jnp.concatenate(accs, axis=-1).astype(o_ref.dtype)


def _sb_prompt(bias, qh, kt, vt, *, blk, heads_per_step):
    n_seq, n_heads, seq_len, dh = qh.shape
    nq = seq_len // blk
    hg = heads_per_step
    kv_spec = pl.BlockSpec((1, hg, dh, seq_len), lambda b, h, qi: (b, h, 0, 0))
    return pl.pallas_call(
        functools.partial(_sb_prompt_kernel, blk=blk),
        grid=(n_seq, n_heads // hg, nq),
        in_specs=[pl.BlockSpec(memory_space=pltpu.SMEM),
                  pl.BlockSpec((1, hg, blk, dh), lambda b, h, qi: (b, h, qi, 0)),
                  kv_spec, kv_spec],
        out_specs=pl.BlockSpec((blk, hg * dh), lambda b, h, qi: (b * nq + qi, h)),
        out_shape=jax.ShapeDtypeStruct((n_seq * seq_len, n_heads * dh), BF16),
        scratch_shapes=[pltpu.VMEM((hg, nq, dh, blk), BF16), pltpu.VMEM((hg, nq, dh, blk), BF16),
                        pltpu.VMEM((hg, blk, blk), F32), pltpu.VMEM((hg, blk, blk), BF16)],
        compiler_params=_params(3), name="sb_prompt",
    )(bias, qh, kt, vt)


def _sb_sample_kernel(pt_ref, q_ref, bias_ref, kn_ref, vn_ref, *refs, pages_per_step, n_new):
    del pt_ref
    k_refs = refs[:pages_per_step]
    v_refs = refs[pages_per_step:2 * pages_per_step]
    o_ref, acc_ref, rest_ref = refs[2 * pages_per_step:]
    j = pl.program_id(1)
    n_rows = n_new * SB_HEADS
    page = kn_ref.shape[1]
    head_of_col = lax.broadcasted_iota(jnp.int32, (n_rows, SB_WIDTH), 1) // SB_DH
    head_of_row = lax.broadcasted_iota(jnp.int32, (n_rows, SB_WIDTH), 0) % SB_HEADS
    own_head = head_of_col == head_of_row
    q = jnp.where(own_head, q_ref[0], 0.0).astype(BF16)
    bias = bias_ref[...]

    def visit(kts, vts, mask):
        zs = [_mm(q, kt) + bias for kt in kts]
        weights, rests = _sb_weights(zs, _neg_upper(kts[0].shape[1]), [rest_ref[:, 0:1]], mask, chained=True)
        acc_ref[...] += functools.reduce(lambda a, b: a + b, [_mm_nt(a, vt) for a, vt in zip(weights, vts)])
        rest_ref[...] = jnp.broadcast_to(rests[0], rest_ref.shape)

    @pl.when(j == 0)
    def _():
        acc_ref[...] = jnp.zeros_like(acc_ref)
        rest_ref[...] = jnp.zeros_like(rest_ref)
        t_of_row = lax.broadcasted_iota(jnp.int32, (n_rows, page), 0) // SB_HEADS
        s_of_col = lax.broadcasted_iota(jnp.int32, (n_rows, page), 1)
        visit([kn_ref[0].T.astype(BF16)], [vn_ref[0].T.astype(BF16)], s_of_col < t_of_row)

    def page_pair(p_refs, i):
        lo_hi = [p_refs[i + 1][0, 0].reshape(SB_WIDTH, page), p_refs[i][0, 0].reshape(SB_WIDTH, page)]
        return jnp.concatenate(lo_hi, axis=1).astype(BF16)

    pairs = range(0, pages_per_step, 2)
    visit([page_pair(k_refs, i) for i in pairs], [page_pair(v_refs, i) for i in pairs], None)

    @pl.when(j == pl.num_programs(1) - 1)
    def _():
        acc = jnp.where(own_head, acc_ref[...], 0.0)
        for t in range(n_new):
            o_ref[0, t:t + 1, :] = jnp.sum(acc[t * SB_HEADS:(t + 1) * SB_HEADS, :], axis=0, keepdims=True)


def _sb_sample(page_table, q_rows, bias_rows, k_new, v_new, cache_kt, cache_vt, *, layer, pages_per_step, n_new):
    n_seq, n_pages = page_table.shape
    page = cache_kt.shape[4]
    n_rows = n_new * SB_HEADS
    steps = n_pages // pages_per_step

    def page_spec(i):
        return pl.BlockSpec((1, 1, SB_HEADS, SB_DH, page),
                            lambda b, j, pt: (layer, pt[b, n_pages - 1 - (j * pages_per_step + i)], 0, 0, 0))

    per_seq = lambda r, c: pl.BlockSpec((1, r, c), lambda b, j, pt: (b, 0, 0))
    grid_spec = pltpu.PrefetchScalarGridSpec(
        num_scalar_prefetch=1, grid=(n_seq, steps),
        in_specs=[per_seq(n_rows, SB_WIDTH),
                  pl.BlockSpec((n_rows, 1), lambda b, j, pt: (0, 0)),
                  per_seq(page, SB_WIDTH), per_seq(page, SB_WIDTH)]
                 + [page_spec(i) for i in range(pages_per_step)] * 2,
        out_specs=per_seq(n_new, SB_WIDTH),
        scratch_shapes=[pltpu.VMEM((n_rows, SB_WIDTH), F32), pltpu.VMEM((n_rows, LANES), F32)])
    return pl.pallas_call(
        functools.partial(_sb_sample_kernel, pages_per_step=pages_per_step, n_new=n_new),
        grid_spec=grid_spec,
        out_shape=jax.ShapeDtypeStruct((n_seq, n_new, SB_WIDTH), F32),
        compiler_params=_params(2), name="sb_sample",
    )(page_table, q_rows, bias_rows, k_new, v_new,
      *([cache_kt] * pages_per_step), *([cache_vt] * pages_per_step))


def _proj_res_kernel(x_ref, a_ref, b_ref, w_ref, o_ref):
    ka = a_ref.shape[1]
    o_ref[...] = (x_ref[...] + _mm(a_ref[...].astype(BF16), w_ref[0:ka, :])
                  + _mm(b_ref[...].astype(BF16), w_ref[ka:, :]))


def _proj_res(x, a, b, w, *, tm):
    m, d = x.shape
    row = lambda width: pl.BlockSpec((tm, width), lambda i: (i, 0))
    return pl.pallas_call(
        _proj_res_kernel, grid=(m // tm,),
        in_specs=[row(d), row(a.shape[1]), row(b.shape[1]), _const_spec(w.shape)],
        out_specs=row(d), out_shape=jax.ShapeDtypeStruct((m, d), F32),
        compiler_params=_params(1), name="proj_res",
    )(x, a, b, w)


def _ffn_chunks(xn, win_ref, wout_ref, cw_ref, cb_ref, o_ref, conv_fn, d_ff):
    bounds = [(c0, min(c0 + FFN_CHUNK, d_ff)) for c0 in range(0, d_ff, FFN_CHUNK)]

    def up(c):
        lo, hi = bounds[c]
        return _mm(xn, win_ref[:, lo:hi]), _mm(xn, win_ref[:, d_ff + lo:d_ff + hi])

    nxt = up(0)
    for c in range(len(bounds)):
        sl = slice(*bounds[c])
        g, u = nxt
        if c + 1 < len(bounds):
            nxt = up(c + 1)
        g_m2, g_m1 = conv_fn(g, sl)
        y = cb_ref[:, sl] + g_m2 * cw_ref[0:1, sl] + g_m1 * cw_ref[1:2, sl] + g * cw_ref[2:3, sl]
        hmid = (_gelu(y) * u).astype(BF16)
        o_ref[...] += _mm(hmid, wout_ref[sl, :])


def _ffn_prompt_kernel(x_ref, g_ref, win_ref, cw_ref, cb_ref, wout_ref, fg_ref, o_ref, tail_out,
                       xn_ref, prev_ref, st_ref, *, tiles_per_seq, d_ff, final_norm):
    i = pl.program_id(0)
    tm = x_ref.shape[0]
    halo = SUBLANES

    @pl.when(i % tiles_per_seq == 0)
    def _():
        prev_ref[...] = jnp.zeros_like(prev_ref)

    x = x_ref[...]
    xn_ref[...] = _rms(x, g_ref[...]).astype(BF16)
    o_ref[...] = x

    def conv_fn(g, sl):
        w = sl.stop - sl.start
        st_ref[0:halo, 0:w] = prev_ref[:, sl]
        st_ref[halo:halo + tm, 0:w] = g
        prev_ref[:, sl] = g[tm - halo:tm, :]
        return st_ref[halo - 2:halo - 2 + tm, 0:w], st_ref[halo - 1:halo - 1 + tm, 0:w]

    _ffn_chunks(xn_ref[...], win_ref, wout_ref, cw_ref, cb_ref, o_ref, conv_fn, d_ff)
    if final_norm:
        o_ref[...] = _rms(o_ref[...], fg_ref[...])

    @pl.when(i % tiles_per_seq == tiles_per_seq - 1)
    def _():
        tail_out[0] = prev_ref[...]


def _ffn_prompt(x, g, win, cw, cb, wout, fg, *, n_seq, seq_len, tm, final_norm):
    m, d = x.shape
    d_ff = wout.shape[0]
    tps = seq_len // tm
    row = pl.BlockSpec((tm, d), lambda i: (i, 0))
    return pl.pallas_call(
        functools.partial(_ffn_prompt_kernel, tiles_per_seq=tps, d_ff=d_ff, final_norm=final_norm),
        grid=(m // tm,),
        in_specs=[row, _const_spec(g.shape), _const_spec(win.shape), _const_spec(cw.shape),
                  _const_spec(cb.shape), _const_spec(wout.shape), _const_spec(fg.shape)],
        out_specs=[row, pl.BlockSpec((1, SUBLANES, d_ff), lambda i: (i // tps, 0, 0))],
        out_shape=[jax.ShapeDtypeStruct((m, d), F32), jax.ShapeDtypeStruct((n_seq, SUBLANES, d_ff), F32)],
        scratch_shapes=[pltpu.VMEM((tm, d), BF16), pltpu.VMEM((SUBLANES, d_ff), F32),
                        pltpu.VMEM((SUBLANES + tm, FFN_CHUNK), F32)],
        compiler_params=_params(1), name="ffn_prompt",
    )(x, g, win, cw, cb, wout, fg)


def _ffn_sample_kernel(x_ref, g_ref, win_ref, cw_ref, cb_ref, wout_ref, fg_ref, st_ref, o_ref, tail_out,
                       *, n_seq, d_ff, final_norm):
    rows = x_ref.shape[0]
    x = x_ref[...]
    xn = _rms(x, g_ref[...]).astype(BF16)
    o_ref[...] = x

    def conv_fn(g, sl):
        ext = jnp.concatenate([st_ref[:, sl], g], axis=0)
        tail_out[:, sl] = g[rows - 2 * n_seq:rows, :]
        return ext[0:rows, :], ext[n_seq:n_seq + rows, :]

    _ffn_chunks(xn, win_ref, wout_ref, cw_ref, cb_ref, o_ref, conv_fn, d_ff)
    if final_norm:
        o_ref[...] = _rms(o_ref[...], fg_ref[...])


def _ffn_sample(x, g, win, cw, cb, wout, fg, st, *, n_seq, final_norm):
    d_ff = wout.shape[0]
    return pl.pallas_call(
        functools.partial(_ffn_sample_kernel, n_seq=n_seq, d_ff=d_ff, final_norm=final_norm),
        out_shape=[jax.ShapeDtypeStruct(x.shape, F32), jax.ShapeDtypeStruct(st.shape, F32)],
        compiler_params=pltpu.CompilerParams(vmem_limit_bytes=VMEM_LIMIT), name="ffn_sample",
    )(x, g, win, cw, cb, wout, fg, st)


def _layernorm_stats(v_ref, width, n_groups):
    gw = width // n_groups
    s1 = 0.0
    for gi in range(n_groups):
        s1 = s1 + jnp.sum(v_ref[:, gi * gw:(gi + 1) * gw], axis=-1, keepdims=True)
    mean = s1 / width
    s2 = 0.0
    for gi in range(n_groups):
        xc = v_ref[:, gi * gw:(gi + 1) * gw] - mean
        s2 = s2 + jnp.sum(xc * xc, axis=-1, keepdims=True)
    return mean, lax.rsqrt(s2 / width + EPS)


def _cm_prompt_kernel(x_ref, g_ref, win_ref, lng_ref, lnb_ref, ws_ref, bst_ref, wout_ref, o_ref,
                      xn_ref, v_ref, *, width):
    tm = x_ref.shape[0]
    gw = width // CM_GROUPS
    x = x_ref[...]
    xn_ref[...] = _rms(x, g_ref[...]).astype(BF16)
    o_ref[...] = x
    proj = lambda col0, gi: _mm(xn_ref[...], win_ref[:, col0 + gi * gw:col0 + (gi + 1) * gw])
    nxt = proj(width, 0)
    for gi in range(CM_GROUPS):
        cur, nxt = nxt, (proj(width, gi + 1) if gi + 1 < CM_GROUPS else proj(0, 0))
        v_ref[:, gi * gw:(gi + 1) * gw] = _gelu(cur)
    mean, rstd = _layernorm_stats(v_ref, width, CM_GROUPS)
    rows = lax.broadcasted_iota(jnp.int32, (CM_CHUNK, CM_CHUNK), 0)
    cols = lax.broadcasted_iota(jnp.int32, (CM_CHUNK, CM_CHUNK), 1)
    tril = rows >= cols
    gated = []
    for gi in range(CM_GROUPS):
        sl = slice(gi * gw, (gi + 1) * gw)
        u_pre, nxt = nxt, (proj(0, gi + 1) if gi + 1 < CM_GROUPS else None)
        vn = ((v_ref[:, sl] - mean) * rstd * lng_ref[:, sl] + lnb_ref[:, sl]).astype(BF16)
        wsg = jnp.where(tril, ws_ref[gi], 0.0).astype(BF16)
        bias = bst_ref[:, gi:gi + 1]
        mixed = jnp.concatenate(
            [_mm(wsg, vn[c * CM_CHUNK:(c + 1) * CM_CHUNK, :]) + bias for c in range(tm // CM_CHUNK)], axis=0)
        gated.append((_gelu(u_pre) * mixed).astype(BF16))
        if len(gated) == CM_OUT_GROUPS:
            lo = (gi + 1 - CM_OUT_GROUPS) * gw
            o_ref[...] += _mm(jnp.concatenate(gated, axis=1), wout_ref[lo:(gi + 1) * gw, :])
            gated = []


def _cm_prompt(x, g, win, lng, lnb, ws, bst, wout, *, tm):
    m, d = x.shape
    width = wout.shape[0]
    row = pl.BlockSpec((tm, d), lambda i: (i, 0))
    return pl.pallas_call(
        functools.partial(_cm_prompt_kernel, width=width),
        grid=(m // tm,),
        in_specs=[row] + [_const_spec(a.shape) for a in (g, win, lng, lnb, ws, bst, wout)],
        out_specs=row, out_shape=jax.ShapeDtypeStruct((m, d), F32),
        scratch_shapes=[pltpu.VMEM((tm, d), BF16), pltpu.VMEM((tm, width), F32)],
        compiler_params=_params(1), name="cm_prompt",
    )(x, g, win, lng, lnb, ws, bst, wout)


def _cm_sample_kernel(x_ref, g_ref, win_ref, lng_ref, lnb_ref, wexp_ref, bexp_ref, wout_ref, o_ref, v_out,
                      *, n_seq, n_new, width):
    gw = width // CM_GROUPS
    x = x_ref[...]
    xn = _rms(x, g_ref[...]).astype(BF16)
    for gi in range(CM_GROUPS):
        v_out[:, gi * gw:(gi + 1) * gw] = _gelu(_mm(xn, win_ref[:, width + gi * gw:width + (gi + 1) * gw]))
    mean, rstd = _layernorm_stats(v_out, width, CM_GROUPS)
    acc = x
    for gi in range(CM_GROUPS):
        sl = slice(gi * gw, (gi + 1) * gw)
        vn = (v_out[:, sl] - mean) * rstd * lng_ref[:, sl] + lnb_ref[:, sl]
        v_out[:, sl] = vn
        mixed = []
        for t in range(n_new):
            mt = bexp_ref[t:t + 1, sl]
            for s in range(t + 1):
                mt = mt + wexp_ref[t * n_new + s:t * n_new + s + 1, sl] * vn[s * n_seq:(s + 1) * n_seq, :]
            mixed.append(mt)
        u = _gelu(_mm(xn, win_ref[:, sl]))
        acc = acc + _mm((u * jnp.concatenate(mixed, axis=0)).astype(BF16), wout_ref[sl, :])
    o_ref[...] = acc


def _cm_sample(x, g, win, lng, lnb, wexp, bexp, wout, *, n_seq, n_new):
    width = wout.shape[0]
    return pl.pallas_call(
        functools.partial(_cm_sample_kernel, n_seq=n_seq, n_new=n_new, width=width),
        out_shape=[jax.ShapeDtypeStruct(x.shape, F32), jax.ShapeDtypeStruct((x.shape[0], width), F32)],
        compiler_params=pltpu.CompilerParams(vmem_limit_bytes=VMEM_LIMIT), name="cm_sample",
    )(x, g, win, lng, lnb, wexp, bexp, wout)


def _time_major(a):
    return jnp.swapaxes(a, 0, 1).reshape((a.shape[0] * a.shape[1],) + a.shape[2:])


def _batch_major(a, n_seq):
    return jnp.swapaxes(a.reshape((a.shape[0] // n_seq, n_seq) + a.shape[1:]), 0, 1)


def _pad_lanes(a):
    return jnp.pad(a, [(0, 0)] * (a.ndim - 1) + [(0, LANES - a.shape[-1])])


def kernel(x_prompt, x_sample, cache_k, cache_v, page_table, state_mlstm_c, state_mlstm_n, state_mlstm_m, state_mlstm_conv, state_ffn_conv, norm_mix_g, norm_ffn_g, norm_final_g, w_in_even, ml_conv_w, ml_conv_b, ml_gate_b, ml_head_g, sb_logit_b, w_out_even, w_in_odd, cm_ln_g, cm_ln_b, cm_spatial_w, cm_spatial_b, w_out_odd, ffn_w_in, ffn_conv_w, ffn_conv_b, ffn_w_out):
    n_pb, seq_p, d = x_prompt.shape
    n_sb, seq_s, _ = x_sample.shape
    depth = norm_mix_g.shape[0]
    d_ff = ffn_w_out.shape[1]
    hp = x_prompt.reshape(n_pb * seq_p, d)
    hs = _time_major(x_sample)
    fg = norm_final_g.reshape(1, d)

    kp_l, vp_l, ks_l, vs_l = [], [], [], []
    cp_l, np_l, mp_l, bp_l = [], [], [], []
    cs_l, ns_l, ms_l, bs_l = [], [], [], []
    cmv_l, fp_l, fs_l = [], [], []
    for layer in range(depth):
        mix_g = norm_mix_g[layer].reshape(1, d)
        if layer % 2 == 0:
            e = layer // 2
            w = w_in_even[e]
            gate_lo = _C_O + ML_WIDTH
            gate_hi = gate_lo + 2 * ML_HEADS
            w_all = jnp.concatenate([w[:, :gate_lo], w[:, gate_hi:], _pad_lanes(w[:, gate_lo:gate_hi])],
                                    axis=1).astype(BF16)
            w_kvt = w[:, gate_hi + SB_WIDTH:].T.astype(BF16)
            cw, cb = ml_conv_w[e], ml_conv_b[e].reshape(1, -1)
            gb = _pad_lanes(ml_gate_b[e].reshape(1, -1))
            hg = ml_head_g[e].reshape(1, -1)
            w_out = w_out_even[e].astype(BF16)
            qk, vm, om, gates, qh, kt, vt = _even_in(
                hp, mix_g, w_all, w_kvt, n_seq=n_pb, seq_len=seq_p, tm=512, head_major=True)
            h_ml, c_p, n_p, m_p, tail_p = _mlstm_prompt(qk, vm, om, gates, cw, cb, gb, hg, n_seq=n_pb, seq_len=seq_p)
            h_sb = _sb_prompt(sb_logit_b[e], qh, kt, vt, blk=256, heads_per_step=4)
            hp = _proj_res(hp, h_ml, h_sb, w_out, tm=512)
            kp_l.append(jnp.transpose(kt, (0, 3, 1, 2)))
            vp_l.append(jnp.transpose(vt, (0, 3, 1, 2)))
            cp_l.append(c_p)
            np_l.append(n_p[:, :ML_HEADS])
            mp_l.append(m_p[:, :ML_HEADS, 0])
            bp_l.append(tail_p[:, SUBLANES - (ML_CONV - 1):])
            qk, vm, om, gates, sq, sk, sv = _even_in(
                hs, mix_g, w_all, w_kvt, n_seq=n_sb, seq_len=seq_s, tm=n_sb * seq_s, head_major=False)
            h_ml, c_s, n_s, m_s = _mlstm_sample(
                qk, vm, om, gates, _time_major(state_mlstm_conv[e]), cw, cb, gb, hg,
                state_mlstm_c[e], state_mlstm_n[e].reshape(n_sb, ML_WIDTH), _pad_lanes(state_mlstm_m[e]),
                n_seq=n_sb, n_new=seq_s)
            page = cache_k.shape[2]
            pad_keys = lambda a: jnp.pad(_batch_major(a, n_sb), ((0, 0), (0, page - seq_s), (0, 0)))
            q_rows = jnp.broadcast_to(_batch_major(sq, n_sb)[:, :, None, :],
                                      (n_sb, seq_s, SB_HEADS, SB_WIDTH)).reshape(n_sb, seq_s * SB_HEADS, SB_WIDTH)
            bias_rows = jnp.tile(sb_logit_b[e], seq_s).reshape(seq_s * SB_HEADS, 1)
            pool_view = lambda c: jnp.transpose(c, (0, 1, 3, 4, 2))
            h_sb = _sb_sample(page_table, q_rows, bias_rows, pad_keys(sk), pad_keys(sv),
                              pool_view(cache_k), pool_view(cache_v), layer=e, pages_per_step=32, n_new=seq_s)
            hs = _proj_res(hs, h_ml, _time_major(h_sb), w_out, tm=n_sb * seq_s)
            ks_l.append(_batch_major(sk, n_sb).reshape(n_sb, seq_s, SB_HEADS, SB_DH))
            vs_l.append(_batch_major(sv, n_sb).reshape(n_sb, seq_s, SB_HEADS, SB_DH))
            cs_l.append(c_s)
            ns_l.append(n_s.reshape(n_sb, ML_HEADS, ML_DH))
            ms_l.append(m_s[:, :ML_HEADS])
            bs_l.append(_batch_major(qk[(seq_s - (ML_CONV - 1)) * n_sb:], n_sb))
        else:
            o = layer // 2
            win = w_in_odd[o].astype(BF16)
            wout = w_out_odd[o].astype(BF16)
            lng, lnb = cm_ln_g[o].reshape(1, -1), cm_ln_b[o].reshape(1, -1)
            width = wout.shape[0]
            gw = width // CM_GROUPS
            hp = _cm_prompt(hp, mix_g, win, lng, lnb, cm_spatial_w[o], cm_spatial_b[o].T, wout, tm=512)
            wexp = jnp.repeat(cm_spatial_w[o][:, :seq_s, :seq_s].reshape(CM_GROUPS, seq_s * seq_s).T, gw, axis=1)
            bexp = jnp.repeat(cm_spatial_b[o][:, :seq_s].T, gw, axis=1)
            hs, v_rows = _cm_sample(hs, mix_g, win, lng, lnb, wexp, bexp, wout, n_seq=n_sb, n_new=seq_s)
            cmv_l.append(_batch_major(v_rows, n_sb))
        ffn_g = norm_ffn_g[layer].reshape(1, d)
        win = ffn_w_in[layer].astype(BF16)
        wout = ffn_w_out[layer].astype(BF16)
        cw, cb = ffn_conv_w[layer], ffn_conv_b[layer].reshape(1, -1)
        last = layer == depth - 1
        hp, tail_p = _ffn_prompt(hp, ffn_g, win, cw, cb, wout, fg, n_seq=n_pb, seq_len=seq_p, tm=1024,
                                 final_norm=last)
        hs, tail_s = _ffn_sample(hs, ffn_g, win, cw, cb, wout, fg, _time_major(state_ffn_conv[layer]),
                                 n_seq=n_sb, final_norm=last)
        fp_l.append(tail_p[:, SUBLANES - (FFN_CONV - 1):])
        fs_l.append(_batch_major(tail_s, n_sb))

    return (hp.reshape(n_pb, seq_p, d), _batch_major(hs, n_sb),
            jnp.stack(kp_l), jnp.stack(vp_l), jnp.stack(ks_l), jnp.stack(vs_l),
            jnp.stack(cp_l), jnp.stack(np_l), jnp.stack(mp_l), jnp.stack(bp_l),
            jnp.stack(cs_l), jnp.stack(ns_l), jnp.stack(ms_l), jnp.stack(bs_l),
            jnp.stack(cmv_l), jnp.stack(fp_l), jnp.stack(fs_l))
```

```python
import functools

import jax
import jax.numpy as jnp
from jax import lax
from jax.experimental import pallas as pl
from jax.experimental.pallas import tpu as pltpu

F32 = jnp.float32
BF16 = jnp.bfloat16
EPS = 1e-6
LOG2E = 1.4426950408889634

ML_HEADS = 4
ML_DH = 128
ML_WIDTH = ML_HEADS * ML_DH
ML_CONV = 4
ML_CHUNK = 128
SB_HEADS = 8
SB_DH = 64
SB_WIDTH = SB_HEADS * SB_DH
SB_SCALE = SB_DH ** -0.5
CM_GROUPS = 8
CM_CHUNK = 128
CM_OUT_GROUPS = 2
FFN_CONV = 3
LANES = 128
SUBLANES = 8
MXU_N = 256
FFN_CHUNK = 2 * MXU_N
VMEM_LIMIT = 56 * 1024 * 1024


def _mm(a, b):
    return jnp.dot(a, b, preferred_element_type=F32)


def _mm_nt(a, b):
    return lax.dot_general(a, b, (((1,), (1,)), ((), ())), preferred_element_type=F32)


def _rms(x, g):
    return x * lax.rsqrt(jnp.mean(x * x, axis=-1, keepdims=True) + EPS) * g


def _sigmoid(x):
    return 1.0 / (1.0 + jnp.exp(-x))


def _softplus(x):
    return jnp.maximum(x, 0.0) + jnp.log1p(jnp.exp(-jnp.abs(x)))


def _gelu(x):
    c = 0.7978845608028654
    return 0.5 * x * (1.0 + jnp.tanh(c * (x + 0.044715 * (x * x * x))))


def _split_hi_lo(x):
    hi = x.astype(BF16)
    lo = (x - hi.astype(F32)).astype(BF16)
    return hi, lo


def _params(n_axes):
    return pltpu.CompilerParams(dimension_semantics=("arbitrary",) * n_axes,
                                vmem_limit_bytes=VMEM_LIMIT)


def _const_spec(shape):
    nd = len(shape)
    return pl.BlockSpec(shape, lambda *_: (0,) * nd, pipeline_mode=pl.Buffered(1))


_C_QK = 0
_C_V = 2 * ML_WIDTH
_C_O = _C_V + ML_WIDTH
_C_SQ = _C_O + ML_WIDTH
_C_SK = _C_SQ + SB_WIDTH
_C_SV = _C_SK + SB_WIDTH
_C_G = _C_SV + SB_WIDTH
_C_END = _C_G + LANES


def _even_in_kernel(x_ref, g_ref, w_ref, wkvt_ref, *out_refs, head_major):
    xn = _rms(x_ref[...], g_ref[...]).astype(BF16)
    qk_ref, v_ref, o_ref, gates_ref, sq_ref, sk_ref, sv_ref = out_refs
    qk_ref[...] = _mm(xn, w_ref[:, _C_QK:_C_V])
    v_ref[...] = _mm(xn, w_ref[:, _C_V:_C_O])
    o_ref[...] = _mm(xn, w_ref[:, _C_O:_C_SQ])
    gates_ref[...] = _mm(xn, w_ref[:, _C_G:_C_END])
    sq = _mm(xn, w_ref[:, _C_SQ:_C_SK]) * SB_SCALE
    if head_major:
        tm = sq.shape[0]
        for h in range(SB_HEADS):
            sq_ref[0, h] = sq[:, h * SB_DH:(h + 1) * SB_DH].astype(BF16)
        sk_ref[0] = _mm_nt(wkvt_ref[0:SB_WIDTH, :], xn).reshape(SB_HEADS, SB_DH, tm)
        sv_ref[0] = _mm_nt(wkvt_ref[SB_WIDTH:, :], xn).reshape(SB_HEADS, SB_DH, tm)
    else:
        sq_ref[...] = sq
        sk_ref[...] = _mm(xn, w_ref[:, _C_SK:_C_SV])
        sv_ref[...] = _mm(xn, w_ref[:, _C_SV:_C_G])


def _even_in(x, g, w, wkvt, *, n_seq, seq_len, tm, head_major):
    m, d = x.shape
    nt = m // tm
    row = lambda width: pl.BlockSpec((tm, width), lambda i: (i, 0))
    out_shape = [jax.ShapeDtypeStruct((m, 2 * ML_WIDTH), F32),
                 jax.ShapeDtypeStruct((m, ML_WIDTH), F32),
                 jax.ShapeDtypeStruct((m, ML_WIDTH), F32),
                 jax.ShapeDtypeStruct((m, LANES), F32)]
    out_specs = [row(2 * ML_WIDTH), row(ML_WIDTH), row(ML_WIDTH), row(LANES)]
    if head_major:
        tps = seq_len // tm
        out_shape += [jax.ShapeDtypeStruct((n_seq, SB_HEADS, seq_len, SB_DH), BF16)]
        out_specs += [pl.BlockSpec((1, SB_HEADS, tm, SB_DH), lambda i: (i // tps, 0, i % tps, 0))]
        out_shape += [jax.ShapeDtypeStruct((n_seq, SB_HEADS, SB_DH, seq_len), F32)] * 2
        out_specs += [pl.BlockSpec((1, SB_HEADS, SB_DH, tm), lambda i: (i // tps, 0, 0, i % tps))] * 2
    else:
        out_shape += [jax.ShapeDtypeStruct((m, SB_WIDTH), F32)] * 3
        out_specs += [row(SB_WIDTH)] * 3
    return pl.pallas_call(
        functools.partial(_even_in_kernel, head_major=head_major),
        grid=(nt,),
        in_specs=[row(d), _const_spec((1, d)), _const_spec(w.shape), _const_spec(wkvt.shape)],
        out_specs=out_specs, out_shape=out_shape,
        compiler_params=_params(1), name="even_in",
    )(x, g, w, wkvt)


def _log_sigmoid(x):
    return jnp.minimum(x, 0.0) - jnp.log1p(jnp.exp(-jnp.abs(x)))


def _mlstm_heads(heads, mask):
    qk = [_mm_nt(h["q"], h["k"]) for h in heads]
    ss, w_inters, floors = [], [], []
    for h, qk_h in zip(heads, qk):
        dmat = jnp.where(mask, h["bc"] - h["br"] + h["li_r"], -jnp.inf)
        inter = h["bc"] + h["m_rows"]
        m_t = jnp.maximum(inter, jnp.max(dmat, axis=-1, keepdims=True))
        ss.append(qk_h * jnp.exp(dmat - m_t))
        w_inters.append(jnp.exp(inter - m_t))
        floors.append(jnp.exp(-m_t))
    svs = [_mm(s.astype(BF16), h["v"]) for s, h in zip(ss, heads)]
    outs = []
    for h, s, sv, w_inter, floor in zip(heads, ss, svs, w_inters, floors):
        num = w_inter * h["cq"] + sv
        den = w_inter * h["nq"] + jnp.sum(s, axis=-1, keepdims=True)
        ho = _sigmoid(h["o"]) * (num / jnp.maximum(jnp.abs(den), floor))
        outs.append(ho * lax.rsqrt(jnp.mean(ho * ho, axis=-1, keepdims=True) + EPS) * h["hg"])
    return outs


def _mlstm_prompt_kernel(qk_ref, v_ref, o_ref, gates_ref, cw_ref, cb_ref, gb_ref, hg_ref,
                         h_ref, c_out, n_out, m_out, tail_out,
                         xp_ref, c_ref, n_ref, m_ref, *, n_chunks):
    ci = pl.program_id(1)
    L = ML_CHUNK
    halo = SUBLANES

    @pl.when(ci == 0)
    def _():
        xp_ref[0:halo, :] = jnp.zeros((halo, 2 * ML_WIDTH), F32)
        c_ref[...] = jnp.zeros_like(c_ref)
        n_ref[...] = jnp.zeros_like(n_ref)
        m_ref[...] = jnp.zeros_like(m_ref)

    xp_ref[halo:halo + L, :] = qk_ref[...]
    y = cb_ref[...]
    for i in range(ML_CONV):
        off = halo - (ML_CONV - 1) + i
        y = y + xp_ref[off:off + L, :] * cw_ref[i:i + 1, :]
    xp_ref[0:halo, :] = qk_ref[L - halo:L, :]
    qk = y * _sigmoid(y)

    gl = gates_ref[...] + gb_ref[...]
    logf = _log_sigmoid(gl)
    rows = lax.broadcasted_iota(jnp.int32, (L, L), 0)
    cols = lax.broadcasted_iota(jnp.int32, (L, L), 1)
    causal = rows >= cols
    tri = jnp.where(causal, 1.0, 0.0).astype(BF16)
    bcol = _cumsum_rows(tri, logf)
    bT = bcol.T
    glT = gl.T

    outs = []
    for h in range(ML_HEADS):
        sl = slice(h * ML_DH, (h + 1) * ML_DH)
        q = qk[:, sl]
        k = qk[:, ML_WIDTH + h * ML_DH:ML_WIDTH + (h + 1) * ML_DH] * (ML_DH ** -0.5)
        v = v_ref[:, sl]
        bc = bcol[:, ML_HEADS + h:ML_HEADS + h + 1]
        m_prev = m_ref[h:h + 1, 0:1]
        c_prev = c_ref[h]
        n_prev = n_ref[h:h + 1, :]
        q_bf, k_bf = q.astype(BF16), k.astype(BF16)
        head = dict(
            q=q_bf, k=k_bf, v=v.astype(BF16), o=o_ref[:, sl], hg=hg_ref[:, sl], bc=bc,
            br=bT[ML_HEADS + h:ML_HEADS + h + 1, :], li_r=glT[h:h + 1, :],
            m_rows=jnp.broadcast_to(m_prev, (L, 1)),
            cq=_mm_nt(q_bf, c_prev.astype(BF16)), nq=jnp.sum(q * n_prev, axis=-1, keepdims=True))
        outs += _mlstm_heads([head], causal)
        b_last = bc[L - 1:L, :]
        g = b_last - bc + gl[:, h:h + 1]
        m_new = jnp.maximum(b_last + m_prev, jnp.max(g, axis=0, keepdims=True))
        a_s = jnp.exp(g - m_new)
        a_c = jnp.exp(b_last + m_prev - m_new)
        c_ref[h] = a_c * c_prev + _mm((a_s * v).T.astype(BF16), k_bf)
        n_ref[h:h + 1, :] = a_c * n_prev + jnp.sum(a_s * k, axis=0, keepdims=True)
        m_ref[h:h + 1, :] = jnp.broadcast_to(m_new, (1, LANES))
    h_ref[...] = jnp.concatenate(outs, axis=-1).astype(h_ref.dtype)

    @pl.when(ci == n_chunks - 1)
    def _():
        c_out[0] = c_ref[...]
        n_out[0] = n_ref[...]
        m_out[0] = m_ref[...]
        tail_out[0] = qk_ref[L - halo:L, :]


def _cumsum_rows(tri, x):
    hi = x.astype(BF16)
    r1 = x - hi.astype(F32)
    mid = r1.astype(BF16)
    lo = (r1 - mid.astype(F32)).astype(BF16)
    return _mm(tri, hi) + _mm(tri, mid) + _mm(tri, lo)


def _mlstm_prompt(qk, v, o, gates, cw, cb, gb, hg, *, n_seq, seq_len):
    nc = seq_len // ML_CHUNK
    L = ML_CHUNK
    row = lambda width: pl.BlockSpec((L, width), lambda b, c: (b * nc + c, 0))
    per_seq = lambda *dims: pl.BlockSpec((1,) + dims, lambda b, c: (b,) + (0,) * len(dims))
    return pl.pallas_call(
        functools.partial(_mlstm_prompt_kernel, n_chunks=nc),
        grid=(n_seq, nc),
        in_specs=[row(2 * ML_WIDTH), row(ML_WIDTH), row(ML_WIDTH), row(LANES),
                  _const_spec(cw.shape), _const_spec(cb.shape), _const_spec(gb.shape), _const_spec(hg.shape)],
        out_specs=[row(ML_WIDTH), per_seq(ML_HEADS, ML_DH, ML_DH), per_seq(SUBLANES, ML_DH),
                   per_seq(SUBLANES, LANES), per_seq(SUBLANES, 2 * ML_WIDTH)],
        out_shape=[jax.ShapeDtypeStruct((n_seq * seq_len, ML_WIDTH), BF16),
                   jax.ShapeDtypeStruct((n_seq, ML_HEADS, ML_DH, ML_DH), F32),
                   jax.ShapeDtypeStruct((n_seq, SUBLANES, ML_DH), F32),
                   jax.ShapeDtypeStruct((n_seq, SUBLANES, LANES), F32),
                   jax.ShapeDtypeStruct((n_seq, SUBLANES, 2 * ML_WIDTH), F32)],
        scratch_shapes=[pltpu.VMEM((SUBLANES + L, 2 * ML_WIDTH), F32),
                        pltpu.VMEM((ML_HEADS, ML_DH, ML_DH), F32),
                        pltpu.VMEM((SUBLANES, ML_DH), F32),
                        pltpu.VMEM((SUBLANES, LANES), F32)],
        compiler_params=_params(2), name="mlstm_prompt",
    )(qk, v, o, gates, cw, cb, gb, hg)


def _mlstm_sample_kernel(qk_ref, v_ref, o_ref, gates_ref, st_ref, cw_ref, cb_ref, gb_ref, hg_ref,
                         c0_ref, n0_ref, m0_ref,
                         h_ref, c_out, n_out, m_out, ac_ref, *, n_seq, n_new):
    L = n_seq * n_new
    ext = jnp.concatenate([st_ref[...], qk_ref[...]], axis=0)
    y = cb_ref[...]
    for i in range(ML_CONV):
        y = y + ext[i * n_seq:i * n_seq + L, :] * cw_ref[i:i + 1, :]
    qk = y * _sigmoid(y)

    gl = gates_ref[...] + gb_ref[...]
    logf = _log_sigmoid(gl)
    tblk = lambda a, t: a[t * n_seq:(t + 1) * n_seq, :]
    b_t = [tblk(logf, 0)]
    for t in range(1, n_new):
        b_t.append(b_t[-1] + tblk(logf, t))
    bcol = jnp.concatenate(b_t, axis=0)
    bT = bcol.T
    glT = gl.T
    rows = lax.broadcasted_iota(jnp.int32, (L, L), 0)
    cols = lax.broadcasted_iota(jnp.int32, (L, L), 1)
    same_seq_causal = (rows >= cols) & (((rows - cols) % n_seq) == 0)
    row_seq = lax.broadcasted_iota(jnp.int32, (L, ML_DH), 0) % n_seq
    col_seq = lax.broadcasted_iota(jnp.int32, (ML_DH, L), 1) % n_seq
    m0 = m0_ref[...]
    m_out[...] = m0

    heads = []
    for h in range(ML_HEADS):
        sl = slice(h * ML_DH, (h + 1) * ML_DH)
        q = qk[:, sl]
        k = qk[:, ML_WIDTH + h * ML_DH:ML_WIDTH + (h + 1) * ML_DH] * (ML_DH ** -0.5)
        v = v_ref[:, sl]
        q_bf = q.astype(BF16)
        k_bf = k.astype(BF16)
        bc = bcol[:, ML_HEADS + h:ML_HEADS + h + 1]
        li_c = gl[:, h:h + 1]
        m0_h = m0[:, h:h + 1]
        m_rows = jnp.concatenate([m0_h] * n_new, axis=0)
        n0_h = n0_ref[:, sl]
        nq = jnp.sum(q * jnp.concatenate([n0_h] * n_new, axis=0), axis=-1, keepdims=True)

        b_last = tblk(bc, n_new - 1)
        g_t = [b_last - tblk(bc, t) + tblk(li_c, t) for t in range(n_new)]
        m_new = b_last + m0_h
        for t in range(n_new):
            m_new = jnp.maximum(m_new, g_t[t])
        a_s_t = [jnp.exp(g_t[t] - m_new) for t in range(n_new)]
        a_c = jnp.exp(b_last + m0_h - m_new)
        n_new_h = a_c * n0_h
        for t in range(n_new):
            n_new_h = n_new_h + a_s_t[t] * tblk(k, t)
        n_out[:, sl] = n_new_h
        m_out[:, h:h + 1] = m_new
        ac_ref[...] = jnp.broadcast_to(a_c, (n_seq, LANES))
        av_t = (jnp.concatenate(a_s_t, axis=0) * v).T

        def per_seq(b, cq):
            c_prev = c0_ref[b, h]
            part = _mm_nt(q_bf, c_prev.astype(BF16))
            cq = jnp.where(row_seq == b, part, cq)
            upd = _mm(jnp.where(col_seq == b, av_t, 0.0).astype(BF16), k_bf)
            c_out[b, h] = ac_ref[pl.ds(b, 1), :] * c_prev + upd
            return cq

        cq = lax.fori_loop(0, n_seq, per_seq, jnp.zeros((L, ML_DH), F32))
        heads.append(dict(q=q_bf, k=k_bf, v=v.astype(BF16), o=o_ref[:, sl], hg=hg_ref[:, sl], bc=bc,
                          br=bT[ML_HEADS + h:ML_HEADS + h + 1, :], li_r=glT[h:h + 1, :],
                          m_rows=m_rows, cq=cq, nq=nq))
    h_ref[...] = jnp.concatenate(_mlstm_heads(heads, same_seq_causal), axis=-1).astype(h_ref.dtype)


def _mlstm_sample(qk, v, o, gates, st, cw, cb, gb, hg, c0, n0, m0, *, n_seq, n_new):
    L = n_seq * n_new
    return pl.pallas_call(
        functools.partial(_mlstm_sample_kernel, n_seq=n_seq, n_new=n_new),
        out_shape=[jax.ShapeDtypeStruct((L, ML_WIDTH), BF16),
                   jax.ShapeDtypeStruct(c0.shape, F32),
                   jax.ShapeDtypeStruct(n0.shape, F32),
                   jax.ShapeDtypeStruct(m0.shape, F32)],
        scratch_shapes=[pltpu.VMEM((n_seq, LANES), F32)],
        compiler_params=pltpu.CompilerParams(vmem_limit_bytes=VMEM_LIMIT), name="mlstm_sample",
    )(qk, v, o, gates, st, cw, cb, gb, hg, c0, n0, m0)


def _neg_upper(n):
    rows = lax.broadcasted_iota(jnp.int32, (n, n), 0)
    cols = lax.broadcasted_iota(jnp.int32, (n, n), 1)
    return jnp.where(rows > cols, -1.0, 0.0).astype(BF16)


def _sb_weights(zs, neg_upper, rests, mask, chained=False):
    log_betas, totals, his, los = [], [], [], []
    for z in zs:
        z2 = z * LOG2E
        sp = jnp.maximum(z2, 0.0) + jnp.log2(1.0 + jnp.exp2(-jnp.abs(z2)))
        if mask is not None:
            sp = jnp.where(mask, sp, 0.0)
        hi, lo = _split_hi_lo(sp)
        log_betas.append(z2 - sp), totals.append(sp[:, 0:1]), his.append(hi), los.append(lo)
    upper2 = jnp.concatenate([neg_upper, neg_upper], axis=0)
    css = [_mm(jnp.concatenate([hi, lo], axis=1), upper2) for hi, lo in zip(his, los)]
    weights, new_rests = [], []
    for i, (lb, cs, tot) in enumerate(zip(log_betas, css, totals)):
        rest = new_rests[-1] if (chained and i) else rests[i]
        a = jnp.exp2(lb + cs + rest)
        if mask is not None:
            a = jnp.where(mask, a, 0.0)
        weights.append(a.astype(BF16))
        new_rests.append(rest + cs[:, 0:1] - tot)
    return weights, (new_rests[-1:] if chained else new_rests)


def _sb_prompt_kernel(bias_ref, q_ref, kt_ref, vt_ref, o_ref, kb_ref, vb_ref, z_ref, a_ref, *, blk):
    hg = q_ref.shape[1]
    hi_ = pl.program_id(1)
    qi = pl.program_id(2)
    n_blk = kb_ref.shape[1]
    heads = range(hg)

    @pl.when(qi == 0)
    def _():
        for hh in heads:
            for j in range(n_blk):
                kb_ref[hh, j] = kt_ref[0, hh, :, j * blk:(j + 1) * blk].astype(BF16)
                vb_ref[hh, j] = vt_ref[0, hh, :, j * blk:(j + 1) * blk].astype(BF16)

    neg_upper = _neg_upper(blk)
    strict_causal = (lax.broadcasted_iota(jnp.int32, (blk, blk), 1)
                     < lax.broadcasted_iota(jnp.int32, (blk, blk), 0))

    def logits(kb):
        return [_mm(q_ref[0, hh], kb_ref[hh, kb]) + bias_ref[hi_ * hg + hh] for hh in heads]

    def weighted_values(kb):
        return [_mm_nt(a_ref[hh], vb_ref[hh, kb]) for hh in heads]

    def visit(kb, mask, carry, first):
        accs, rests = carry
        zs = [z_ref[hh] for hh in heads]
        z_next = logits(jnp.maximum(kb - 1, 0))
        if not first:
            accs = tuple(a + p for a, p in zip(accs, weighted_values(kb + 1)))
        for hh in heads:
            z_ref[hh] = z_next[hh]
        weights, rests = _sb_weights(zs, neg_upper, rests, mask)
        for hh in heads:
            a_ref[hh] = weights[hh]
        return accs, tuple(rests)

    for hh, z in enumerate(logits(qi)):
        z_ref[hh] = z
    carry = ((jnp.zeros((blk, SB_DH), F32),) * hg, (jnp.zeros((blk, 1), F32),) * hg)
    carry = visit(qi, strict_causal, carry, True)
    accs, _ = lax.fori_loop(0, qi, lambda i, c: visit(qi - 1 - i, None, c, False), carry)
    accs = [a + p for a, p in zip(accs, weighted_values(0))]
    o_ref[...] = jnp.concatenate(accs, axis=-1).astype(o_ref.dtype)


def _sb_prompt(bias, qh, kt, vt, *, blk, heads_per_step):
    n_seq, n_heads, seq_len, dh = qh.shape
    nq = seq_len // blk
    hg = heads_per_step
    kv_spec = pl.BlockSpec((1, hg, dh, seq_len), lambda b, h, qi: (b, h, 0, 0))
    return pl.pallas_call(
        functools.partial(_sb_prompt_kernel, blk=blk),
        grid=(n_seq, n_heads // hg, nq),
        in_specs=[pl.BlockSpec(memory_space=pltpu.SMEM),
                  pl.BlockSpec((1, hg, blk, dh), lambda b, h, qi: (b, h, qi, 0)),
                  kv_spec, kv_spec],
        out_specs=pl.BlockSpec((blk, hg * dh), lambda b, h, qi: (b * nq + qi, h)),
        out_shape=jax.ShapeDtypeStruct((n_seq * seq_len, n_heads * dh), BF16),
        scratch_shapes=[pltpu.VMEM((hg, nq, dh, blk), BF16), pltpu.VMEM((hg, nq, dh, blk), BF16),
                        pltpu.VMEM((hg, blk, blk), F32), pltpu.VMEM((hg, blk, blk), BF16)],
        compiler_params=_params(3), name="sb_prompt",
    )(bias, qh, kt, vt)


def _sb_sample_kernel(pt_ref, q_ref, bias_ref, kn_ref, vn_ref, *refs, pages_per_step, n_new):
    del pt_ref
    k_refs = refs[:pages_per_step]
    v_refs = refs[pages_per_step:2 * pages_per_step]
    o_ref, acc_ref, rest_ref = refs[2 * pages_per_step:]
    j = pl.program_id(1)
    n_rows = n_new * SB_HEADS
    page = kn_ref.shape[1]
    head_of_col = lax.broadcasted_iota(jnp.int32, (n_rows, SB_WIDTH), 1) // SB_DH
    head_of_row = lax.broadcasted_iota(jnp.int32, (n_rows, SB_WIDTH), 0) % SB_HEADS
    own_head = head_of_col == head_of_row
    q = jnp.where(own_head, q_ref[0], 0.0).astype(BF16)
    bias = bias_ref[...]

    def visit(kts, vts, mask):
        zs = [_mm(q, kt) + bias for kt in kts]
        weights, rests = _sb_weights(zs, _neg_upper(kts[0].shape[1]), [rest_ref[:, 0:1]], mask, chained=True)
        acc_ref[...] += functools.reduce(lambda a, b: a + b, [_mm_nt(a, vt) for a, vt in zip(weights, vts)])
        rest_ref[...] = jnp.broadcast_to(rests[0], rest_ref.shape)

    @pl.when(j == 0)
    def _():
        acc_ref[...] = jnp.zeros_like(acc_ref)
        rest_ref[...] = jnp.zeros_like(rest_ref)
        t_of_row = lax.broadcasted_iota(jnp.int32, (n_rows, page), 0) // SB_HEADS
        s_of_col = lax.broadcasted_iota(jnp.int32, (n_rows, page), 1)
        visit([kn_ref[0].T.astype(BF16)], [vn_ref[0].T.astype(BF16)], s_of_col < t_of_row)

    def page_pair(p_refs, i):
        lo_hi = [p_refs[i + 1][0, 0].reshape(SB_WIDTH, page), p_refs[i][0, 0].reshape(SB_WIDTH, page)]
        return jnp.concatenate(lo_hi, axis=1).astype(BF16)

    pairs = range(0, pages_per_step, 2)
    visit([page_pair(k_refs, i) for i in pairs], [page_pair(v_refs, i) for i in pairs], None)

    @pl.when(j == pl.num_programs(1) - 1)
    def _():
        acc = jnp.where(own_head, acc_ref[...], 0.0)
        for t in range(n_new):
            o_ref[0, t:t + 1, :] = jnp.sum(acc[t * SB_HEADS:(t + 1) * SB_HEADS, :], axis=0, keepdims=True)


def _sb_sample(page_table, q_rows, bias_rows, k_new, v_new, cache_kt, cache_vt, *, layer, pages_per_step, n_new):
    n_seq, n_pages = page_table.shape
    page = cache_kt.shape[4]
    n_rows = n_new * SB_HEADS
    steps = n_pages // pages_per_step

    def page_spec(i):
        return pl.BlockSpec((1, 1, SB_HEADS, SB_DH, page),
                            lambda b, j, pt: (layer, pt[b, n_pages - 1 - (j * pages_per_step + i)], 0, 0, 0))

    per_seq = lambda r, c: pl.BlockSpec((1, r, c), lambda b, j, pt: (b, 0, 0))
    grid_spec = pltpu.PrefetchScalarGridSpec(
        num_scalar_prefetch=1, grid=(n_seq, steps),
        in_specs=[per_seq(n_rows, SB_WIDTH),
                  pl.BlockSpec((n_rows, 1), lambda b, j, pt: (0, 0)),
                  per_seq(page, SB_WIDTH), per_seq(page, SB_WIDTH)]
                 + [page_spec(i) for i in range(pages_per_step)] * 2,
        out_specs=per_seq(n_new, SB_WIDTH),
        scratch_shapes=[pltpu.VMEM((n_rows, SB_WIDTH), F32), pltpu.VMEM((n_rows, LANES), F32)])
    return pl.pallas_call(
        functools.partial(_sb_sample_kernel, pages_per_step=pages_per_step, n_new=n_new),
        grid_spec=grid_spec,
        out_shape=jax.ShapeDtypeStruct((n_seq, n_new, SB_WIDTH), F32),
        compiler_params=_params(2), name="sb_sample",
    )(page_table, q_rows, bias_rows, k_new, v_new,
      *([cache_kt] * pages_per_step), *([cache_vt] * pages_per_step))


def _proj_res_kernel(x_ref, a_ref, b_ref, w_ref, o_ref):
    ka = a_ref.shape[1]
    o_ref[...] = (x_ref[...] + _mm(a_ref[...].astype(BF16), w_ref[0:ka, :])
                  + _mm(b_ref[...].astype(BF16), w_ref[ka:, :]))


def _proj_res(x, a, b, w, *, tm):
    m, d = x.shape
    row = lambda width: pl.BlockSpec((tm, width), lambda i: (i, 0))
    return pl.pallas_call(
        _proj_res_kernel, grid=(m // tm,),
        in_specs=[row(d), row(a.shape[1]), row(b.shape[1]), _const_spec(w.shape)],
        out_specs=row(d), out_shape=jax.ShapeDtypeStruct((m, d), F32),
        compiler_params=_params(1), name="proj_res",
    )(x, a, b, w)


def _ffn_chunks(xn, win_ref, wout_ref, cw_ref, cb_ref, o_ref, conv_fn, d_ff):
    bounds = [(c0, min(c0 + FFN_CHUNK, d_ff)) for c0 in range(0, d_ff, FFN_CHUNK)]

    def up(c):
        lo, hi = bounds[c]
        return _mm(xn, win_ref[:, lo:hi]), _mm(xn, win_ref[:, d_ff + lo:d_ff + hi])

    nxt = up(0)
    for c in range(len(bounds)):
        sl = slice(*bounds[c])
        g, u = nxt
        if c + 1 < len(bounds):
            nxt = up(c + 1)
        g_m2, g_m1 = conv_fn(g, sl)
        y = cb_ref[:, sl] + g_m2 * cw_ref[0:1, sl] + g_m1 * cw_ref[1:2, sl] + g * cw_ref[2:3, sl]
        hmid = (_gelu(y) * u).astype(BF16)
        o_ref[...] += _mm(hmid, wout_ref[sl, :])


def _ffn_prompt_kernel(x_ref, g_ref, win_ref, cw_ref, cb_ref, wout_ref, fg_ref, *rest,
                       tiles_per_seq, d_ff, final_norm, mixer_proj):
    if mixer_proj:
        a_ref, b_ref, wmix_ref, o_ref, tail_out, xn_ref, prev_ref, st_ref = rest
    else:
        o_ref, tail_out, xn_ref, prev_ref, st_ref = rest
    i = pl.program_id(0)
    tm = x_ref.shape[0]
    halo = SUBLANES

    @pl.when(i % tiles_per_seq == 0)
    def _():
        prev_ref[...] = jnp.zeros_like(prev_ref)

    x = x_ref[...]
    if mixer_proj:
        ka = a_ref.shape[1]
        x = x + _mm(a_ref[...], wmix_ref[0:ka, :]) + _mm(b_ref[...], wmix_ref[ka:, :])
    xn_ref[...] = _rms(x, g_ref[...]).astype(BF16)
    o_ref[...] = x

    def conv_fn(g, sl):
        w = sl.stop - sl.start
        st_ref[0:halo, 0:w] = prev_ref[:, sl]
        st_ref[halo:halo + tm, 0:w] = g
        prev_ref[:, sl] = g[tm - halo:tm, :]
        return st_ref[halo - 2:halo - 2 + tm, 0:w], st_ref[halo - 1:halo - 1 + tm, 0:w]

    _ffn_chunks(xn_ref[...], win_ref, wout_ref, cw_ref, cb_ref, o_ref, conv_fn, d_ff)
    if final_norm:
        o_ref[...] = _rms(o_ref[...], fg_ref[...])

    @pl.when(i % tiles_per_seq == tiles_per_seq - 1)
    def _():
        tail_out[0] = prev_ref[...]


def _ffn_prompt(x, g, win, cw, cb, wout, fg, mixer=None, *, n_seq, seq_len, tm, final_norm):
    m, d = x.shape
    d_ff = wout.shape[0]
    tps = seq_len // tm
    row = lambda width: pl.BlockSpec((tm, width), lambda i: (i, 0))
    operands = [x, g, win, cw, cb, wout, fg]
    in_specs = [row(d)] + [_const_spec(a.shape) for a in operands[1:]]
    if mixer is not None:
        a, b, w_mix = mixer
        operands += [a, b, w_mix]
        in_specs += [row(a.shape[1]), row(b.shape[1]), _const_spec(w_mix.shape)]
    return pl.pallas_call(
        functools.partial(_ffn_prompt_kernel, tiles_per_seq=tps, d_ff=d_ff, final_norm=final_norm,
                          mixer_proj=mixer is not None),
        grid=(m // tm,),
        in_specs=in_specs,
        out_specs=[row(d), pl.BlockSpec((1, SUBLANES, d_ff), lambda i: (i // tps, 0, 0))],
        out_shape=[jax.ShapeDtypeStruct((m, d), F32), jax.ShapeDtypeStruct((n_seq, SUBLANES, d_ff), F32)],
        scratch_shapes=[pltpu.VMEM((tm, d), BF16), pltpu.VMEM((SUBLANES, d_ff), F32),
                        pltpu.VMEM((SUBLANES + tm, FFN_CHUNK), F32)],
        compiler_params=_params(1), name="ffn_prompt",
    )(*operands)


def _ffn_sample_kernel(x_ref, g_ref, win_ref, cw_ref, cb_ref, wout_ref, fg_ref, st_ref, o_ref, tail_out,
                       *, n_seq, d_ff, final_norm):
    rows = x_ref.shape[0]
    x = x_ref[...]
    xn = _rms(x, g_ref[...]).astype(BF16)
    o_ref[...] = x

    def conv_fn(g, sl):
        ext = jnp.concatenate([st_ref[:, sl], g], axis=0)
        tail_out[:, sl] = g[rows - 2 * n_seq:rows, :]
        return ext[0:rows, :], ext[n_seq:n_seq + rows, :]

    _ffn_chunks(xn, win_ref, wout_ref, cw_ref, cb_ref, o_ref, conv_fn, d_ff)
    if final_norm:
        o_ref[...] = _rms(o_ref[...], fg_ref[...])


def _ffn_sample(x, g, win, cw, cb, wout, fg, st, *, n_seq, final_norm):
    d_ff = wout.shape[0]
    return pl.pallas_call(
        functools.partial(_ffn_sample_kernel, n_seq=n_seq, d_ff=d_ff, final_norm=final_norm),
        out_shape=[jax.ShapeDtypeStruct(x.shape, F32), jax.ShapeDtypeStruct(st.shape, F32)],
        compiler_params=pltpu.CompilerParams(vmem_limit_bytes=VMEM_LIMIT), name="ffn_sample",
    )(x, g, win, cw, cb, wout, fg, st)


def _layernorm_stats(v_ref, width, n_groups):
    gw = width // n_groups
    s1 = 0.0
    for gi in range(n_groups):
        s1 = s1 + jnp.sum(v_ref[:, gi * gw:(gi + 1) * gw], axis=-1, keepdims=True)
    mean = s1 / width
    s2 = 0.0
    for gi in range(n_groups):
        xc = v_ref[:, gi * gw:(gi + 1) * gw] - mean
        s2 = s2 + jnp.sum(xc * xc, axis=-1, keepdims=True)
    return mean, lax.rsqrt(s2 / width + EPS)


def _cm_prompt_kernel(x_ref, g_ref, win_ref, lng_ref, lnb_ref, ws_ref, bst_ref, wout_ref, o_ref,
                      xn_ref, v_ref, *, width):
    tm = x_ref.shape[0]
    gw = width // CM_GROUPS
    x = x_ref[...]
    xn_ref[...] = _rms(x, g_ref[...]).astype(BF16)
    o_ref[...] = x
    proj = lambda col0, gi: _mm(xn_ref[...], win_ref[:, col0 + gi * gw:col0 + (gi + 1) * gw])
    nxt = proj(width, 0)
    for gi in range(CM_GROUPS):
        cur, nxt = nxt, (proj(width, gi + 1) if gi + 1 < CM_GROUPS else proj(0, 0))
        v_ref[:, gi * gw:(gi + 1) * gw] = _gelu(cur)
    mean, rstd = _layernorm_stats(v_ref, width, CM_GROUPS)
    rows = lax.broadcasted_iota(jnp.int32, (CM_CHUNK, CM_CHUNK), 0)
    cols = lax.broadcasted_iota(jnp.int32, (CM_CHUNK, CM_CHUNK), 1)
    tril = rows >= cols
    gated = []
    for gi in range(CM_GROUPS):
        sl = slice(gi * gw, (gi + 1) * gw)
        u_pre, nxt = nxt, (proj(0, gi + 1) if gi + 1 < CM_GROUPS else None)
        vn = ((v_ref[:, sl] - mean) * rstd * lng_ref[:, sl] + lnb_ref[:, sl]).astype(BF16)
        wsg = jnp.where(tril, ws_ref[gi], 0.0).astype(BF16)
        bias = bst_ref[:, gi:gi + 1]
        mixed = jnp.concatenate(
            [_mm(wsg, vn[c * CM_CHUNK:(c + 1) * CM_CHUNK, :]) + bias for c in range(tm // CM_CHUNK)], axis=0)
        gated.append((_gelu(u_pre) * mixed).astype(BF16))
        if len(gated) == CM_OUT_GROUPS:
            lo = (gi + 1 - CM_OUT_GROUPS) * gw
            o_ref[...] += _mm(jnp.concatenate(gated, axis=1), wout_ref[lo:(gi + 1) * gw, :])
            gated = []


def _cm_prompt(x, g, win, lng, lnb, ws, bst, wout, *, tm):
    m, d = x.shape
    width = wout.shape[0]
    row = pl.BlockSpec((tm, d), lambda i: (i, 0))
    return pl.pallas_call(
        functools.partial(_cm_prompt_kernel, width=width),
        grid=(m // tm,),
        in_specs=[row] + [_const_spec(a.shape) for a in (g, win, lng, lnb, ws, bst, wout)],
        out_specs=row, out_shape=jax.ShapeDtypeStruct((m, d), F32),
        scratch_shapes=[pltpu.VMEM((tm, d), BF16), pltpu.VMEM((tm, width), F32)],
        compiler_params=_params(1), name="cm_prompt",
    )(x, g, win, lng, lnb, ws, bst, wout)


def _cm_sample_kernel(x_ref, g_ref, win_ref, lng_ref, lnb_ref, wexp_ref, bexp_ref, wout_ref, o_ref, v_out,
                      *, n_seq, n_new, width):
    gw = width // CM_GROUPS
    x = x_ref[...]
    xn = _rms(x, g_ref[...]).astype(BF16)
    for gi in range(CM_GROUPS):
        v_out[:, gi * gw:(gi + 1) * gw] = _gelu(_mm(xn, win_ref[:, width + gi * gw:width + (gi + 1) * gw]))
    mean, rstd = _layernorm_stats(v_out, width, CM_GROUPS)
    acc = x
    for gi in range(CM_GROUPS):
        sl = slice(gi * gw, (gi + 1) * gw)
        vn = (v_out[:, sl] - mean) * rstd * lng_ref[:, sl] + lnb_ref[:, sl]
        v_out[:, sl] = vn
        mixed = []
        for t in range(n_new):
            mt = bexp_ref[t:t + 1, sl]
            for s in range(t + 1):
                mt = mt + wexp_ref[t * n_new + s:t * n_new + s + 1, sl] * vn[s * n_seq:(s + 1) * n_seq, :]
            mixed.append(mt)
        u = _gelu(_mm(xn, win_ref[:, sl]))
        acc = acc + _mm((u * jnp.concatenate(mixed, axis=0)).astype(BF16), wout_ref[sl, :])
    o_ref[...] = acc


def _cm_sample(x, g, win, lng, lnb, wexp, bexp, wout, *, n_seq, n_new):
    width = wout.shape[0]
    return pl.pallas_call(
        functools.partial(_cm_sample_kernel, n_seq=n_seq, n_new=n_new, width=width),
        out_shape=[jax.ShapeDtypeStruct(x.shape, F32), jax.ShapeDtypeStruct((x.shape[0], width), F32)],
        compiler_params=pltpu.CompilerParams(vmem_limit_bytes=VMEM_LIMIT), name="cm_sample",
    )(x, g, win, lng, lnb, wexp, bexp, wout)


def _time_major(a):
    return jnp.swapaxes(a, 0, 1).reshape((a.shape[0] * a.shape[1],) + a.shape[2:])


def _batch_major(a, n_seq):
    return jnp.swapaxes(a.reshape((a.shape[0] // n_seq, n_seq) + a.shape[1:]), 0, 1)


def _pad_lanes(a):
    return jnp.pad(a, [(0, 0)] * (a.ndim - 1) + [(0, LANES - a.shape[-1])])


def kernel(x_prompt, x_sample, cache_k, cache_v, page_table, state_mlstm_c, state_mlstm_n, state_mlstm_m, state_mlstm_conv, state_ffn_conv, norm_mix_g, norm_ffn_g, norm_final_g, w_in_even, ml_conv_w, ml_conv_b, ml_gate_b, ml_head_g, sb_logit_b, w_out_even, w_in_odd, cm_ln_g, cm_ln_b, cm_spatial_w, cm_spatial_b, w_out_odd, ffn_w_in, ffn_conv_w, ffn_conv_b, ffn_w_out):
    n_pb, seq_p, d = x_prompt.shape
    n_sb, seq_s, _ = x_sample.shape
    depth = norm_mix_g.shape[0]
    d_ff = ffn_w_out.shape[1]
    hp = x_prompt.reshape(n_pb * seq_p, d)
    hs = _time_major(x_sample)
    fg = norm_final_g.reshape(1, d)

    kp_l, vp_l, ks_l, vs_l = [], [], [], []
    cp_l, np_l, mp_l, bp_l = [], [], [], []
    cs_l, ns_l, ms_l, bs_l = [], [], [], []
    cmv_l, fp_l, fs_l = [], [], []
    for layer in range(depth):
        mix_g = norm_mix_g[layer].reshape(1, d)
        if layer % 2 == 0:
            e = layer // 2
            w = w_in_even[e]
            gate_lo = _C_O + ML_WIDTH
            gate_hi = gate_lo + 2 * ML_HEADS
            w_all = jnp.concatenate([w[:, :gate_lo], w[:, gate_hi:], _pad_lanes(w[:, gate_lo:gate_hi])],
                                    axis=1).astype(BF16)
            w_kvt = w[:, gate_hi + SB_WIDTH:].T.astype(BF16)
            cw, cb = ml_conv_w[e], ml_conv_b[e].reshape(1, -1)
            gb = _pad_lanes(ml_gate_b[e].reshape(1, -1))
            hg = ml_head_g[e].reshape(1, -1)
            w_out = w_out_even[e].astype(BF16)
            qk, vm, om, gates, qh, kt, vt = _even_in(
                hp, mix_g, w_all, w_kvt, n_seq=n_pb, seq_len=seq_p, tm=512, head_major=True)
            h_ml, c_p, n_p, m_p, tail_p = _mlstm_prompt(qk, vm, om, gates, cw, cb, gb, hg, n_seq=n_pb, seq_len=seq_p)
            h_sb = _sb_prompt(sb_logit_b[e], qh, kt, vt, blk=256, heads_per_step=4)
            prompt_mixer = (h_ml, h_sb, w_out)
            kp_l.append(jnp.transpose(kt, (0, 3, 1, 2)))
            vp_l.append(jnp.transpose(vt, (0, 3, 1, 2)))
            cp_l.append(c_p)
            np_l.append(n_p[:, :ML_HEADS])
            mp_l.append(m_p[:, :ML_HEADS, 0])
            bp_l.append(tail_p[:, SUBLANES - (ML_CONV - 1):])
            qk, vm, om, gates, sq, sk, sv = _even_in(
                hs, mix_g, w_all, w_kvt, n_seq=n_sb, seq_len=seq_s, tm=n_sb * seq_s, head_major=False)
            h_ml, c_s, n_s, m_s = _mlstm_sample(
                qk, vm, om, gates, _time_major(state_mlstm_conv[e]), cw, cb, gb, hg,
                state_mlstm_c[e], state_mlstm_n[e].reshape(n_sb, ML_WIDTH), _pad_lanes(state_mlstm_m[e]),
                n_seq=n_sb, n_new=seq_s)
            page = cache_k.shape[2]
            pad_keys = lambda a: jnp.pad(_batch_major(a, n_sb), ((0, 0), (0, page - seq_s), (0, 0)))
            q_rows = jnp.broadcast_to(_batch_major(sq, n_sb)[:, :, None, :],
                                      (n_sb, seq_s, SB_HEADS, SB_WIDTH)).reshape(n_sb, seq_s * SB_HEADS, SB_WIDTH)
            bias_rows = jnp.tile(sb_logit_b[e], seq_s).reshape(seq_s * SB_HEADS, 1)
            pool_view = lambda c: jnp.transpose(c, (0, 1, 3, 4, 2))
            h_sb = _sb_sample(page_table, q_rows, bias_rows, pad_keys(sk), pad_keys(sv),
                              pool_view(cache_k), pool_view(cache_v), layer=e, pages_per_step=32, n_new=seq_s)
            hs = _proj_res(hs, h_ml, _time_major(h_sb), w_out, tm=n_sb * seq_s)
            ks_l.append(_batch_major(sk, n_sb).reshape(n_sb, seq_s, SB_HEADS, SB_DH))
            vs_l.append(_batch_major(sv, n_sb).reshape(n_sb, seq_s, SB_HEADS, SB_DH))
            cs_l.append(c_s)
            ns_l.append(n_s.reshape(n_sb, ML_HEADS, ML_DH))
            ms_l.append(m_s[:, :ML_HEADS])
            bs_l.append(_batch_major(qk[(seq_s - (ML_CONV - 1)) * n_sb:], n_sb))
        else:
            o = layer // 2
            win = w_in_odd[o].astype(BF16)
            wout = w_out_odd[o].astype(BF16)
            lng, lnb = cm_ln_g[o].reshape(1, -1), cm_ln_b[o].reshape(1, -1)
            width = wout.shape[0]
            gw = width // CM_GROUPS
            prompt_mixer = None
            hp = _cm_prompt(hp, mix_g, win, lng, lnb, cm_spatial_w[o], cm_spatial_b[o].T, wout, tm=512)
            wexp = jnp.repeat(cm_spatial_w[o][:, :seq_s, :seq_s].reshape(CM_GROUPS, seq_s * seq_s).T, gw, axis=1)
            bexp = jnp.repeat(cm_spatial_b[o][:, :seq_s].T, gw, axis=1)
            hs, v_rows = _cm_sample(hs, mix_g, win, lng, lnb, wexp, bexp, wout, n_seq=n_sb, n_new=seq_s)
            cmv_l.append(_batch_major(v_rows, n_sb))
        ffn_g = norm_ffn_g[layer].reshape(1, d)
        win = ffn_w_in[layer].astype(BF16)
        wout = ffn_w_out[layer].astype(BF16)
        cw, cb = ffn_conv_w[layer], ffn_conv_b[layer].reshape(1, -1)
        last = layer == depth - 1
        hp, tail_p = _ffn_prompt(hp, ffn_g, win, cw, cb, wout, fg, prompt_mixer, n_seq=n_pb, seq_len=seq_p,
                                 tm=1024, final_norm=last)
        hs, tail_s = _ffn_sample(hs, ffn_g, win, cw, cb, wout, fg, _time_major(state_ffn_conv[layer]),
                                 n_seq=n_sb, final_norm=last)
        fp_l.append(tail_p[:, SUBLANES - (FFN_CONV - 1):])
        fs_l.append(_batch_major(tail_s, n_sb))

    return (hp.reshape(n_pb, seq_p, d), _batch_major(hs, n_sb),
            jnp.stack(kp_l), jnp.stack(vp_l), jnp.stack(ks_l), jnp.stack(vs_l),
            jnp.stack(cp_l), jnp.stack(np_l), jnp.stack(mp_l), jnp.stack(bp_l),
            jnp.stack(cs_l), jnp.stack(ns_l), jnp.stack(ms_l), jnp.stack(bs_l),
            jnp.stack(cmv_l), jnp.stack(fp_l), jnp.stack(fs_l))
```

```python
import functools

import jax
import jax.numpy as jnp
from jax import lax
from jax.experimental import pallas as pl
from jax.experimental.pallas import tpu as pltpu

F32 = jnp.float32
BF16 = jnp.bfloat16
EPS = 1e-6
LOG2E = 1.4426950408889634

ML_HEADS = 4
ML_DH = 128
ML_WIDTH = ML_HEADS * ML_DH
ML_CONV = 4
ML_CHUNK = 128
SB_HEADS = 8
SB_DH = 64
SB_WIDTH = SB_HEADS * SB_DH
SB_SCALE = SB_DH ** -0.5
CM_GROUPS = 8
CM_CHUNK = 128
CM_OUT_GROUPS = 2
FFN_CONV = 3
LANES = 128
SUBLANES = 8
MXU_N = 256
FFN_CHUNK = 2 * MXU_N
VMEM_LIMIT = 56 * 1024 * 1024


def _mm(a, b):
    return jnp.dot(a, b, preferred_element_type=F32)


def _mm_nt(a, b):
    return lax.dot_general(a, b, (((1,), (1,)), ((), ())), preferred_element_type=F32)


def _rms(x, g):
    return x * lax.rsqrt(jnp.mean(x * x, axis=-1, keepdims=True) + EPS) * g


def _sigmoid(x):
    return 1.0 / (1.0 + jnp.exp(-x))


def _softplus(x):
    return jnp.maximum(x, 0.0) + jnp.log1p(jnp.exp(-jnp.abs(x)))


def _gelu(x):
    c = 0.7978845608028654
    return 0.5 * x * (1.0 + jnp.tanh(c * (x + 0.044715 * (x * x * x))))


def _split_hi_lo(x):
    hi = x.astype(BF16)
    lo = (x - hi.astype(F32)).astype(BF16)
    return hi, lo


def _params(n_axes):
    return pltpu.CompilerParams(dimension_semantics=("arbitrary",) * n_axes,
                                vmem_limit_bytes=VMEM_LIMIT)


def _const_spec(shape):
    nd = len(shape)
    return pl.BlockSpec(shape, lambda *_: (0,) * nd, pipeline_mode=pl.Buffered(1))


_C_QK = 0
_C_V = 2 * ML_WIDTH
_C_O = _C_V + ML_WIDTH
_C_SQ = _C_O + ML_WIDTH
_C_SK = _C_SQ + SB_WIDTH
_C_SV = _C_SK + SB_WIDTH
_C_G = _C_SV + SB_WIDTH
_C_END = _C_G + LANES


def _even_in_kernel(x_ref, g_ref, w_ref, wkvt_ref, *out_refs, head_major):
    xn = _rms(x_ref[...], g_ref[...]).astype(BF16)
    qk_ref, v_ref, o_ref, gates_ref, sq_ref, sk_ref, sv_ref = out_refs
    qk_ref[...] = _mm(xn, w_ref[:, _C_QK:_C_V])
    v_ref[...] = _mm(xn, w_ref[:, _C_V:_C_O])
    o_ref[...] = _mm(xn, w_ref[:, _C_O:_C_SQ])
    gates_ref[...] = _mm(xn, w_ref[:, _C_G:_C_END])
    sq = _mm(xn, w_ref[:, _C_SQ:_C_SK]) * SB_SCALE
    if head_major:
        tm = sq.shape[0]
        for h in range(SB_HEADS):
            sq_ref[0, h] = sq[:, h * SB_DH:(h + 1) * SB_DH].astype(BF16)
        sk_ref[0] = _mm_nt(wkvt_ref[0:SB_WIDTH, :], xn).reshape(SB_HEADS, SB_DH, tm)
        sv_ref[0] = _mm_nt(wkvt_ref[SB_WIDTH:, :], xn).reshape(SB_HEADS, SB_DH, tm)
    else:
        sq_ref[...] = sq
        sk_ref[...] = _mm(xn, w_ref[:, _C_SK:_C_SV])
        sv_ref[...] = _mm(xn, w_ref[:, _C_SV:_C_G])


def _even_in(x, g, w, wkvt, *, n_seq, seq_len, tm, head_major):
    m, d = x.shape
    nt = m // tm
    row = lambda width: pl.BlockSpec((tm, width), lambda i: (i, 0))
    out_shape = [jax.ShapeDtypeStruct((m, 2 * ML_WIDTH), F32),
                 jax.ShapeDtypeStruct((m, ML_WIDTH), F32),
                 jax.ShapeDtypeStruct((m, ML_WIDTH), F32),
                 jax.ShapeDtypeStruct((m, LANES), F32)]
    out_specs = [row(2 * ML_WIDTH), row(ML_WIDTH), row(ML_WIDTH), row(LANES)]
    if head_major:
        tps = seq_len // tm
        out_shape += [jax.ShapeDtypeStruct((n_seq, SB_HEADS, seq_len, SB_DH), BF16)]
        out_specs += [pl.BlockSpec((1, SB_HEADS, tm, SB_DH), lambda i: (i // tps, 0, i % tps, 0))]
        out_shape += [jax.ShapeDtypeStruct((n_seq, SB_HEADS, SB_DH, seq_len), F32)] * 2
        out_specs += [pl.BlockSpec((1, SB_HEADS, SB_DH, tm), lambda i: (i // tps, 0, 0, i % tps))] * 2
    else:
        out_shape += [jax.ShapeDtypeStruct((m, SB_WIDTH), F32)] * 3
        out_specs += [row(SB_WIDTH)] * 3
    return pl.pallas_call(
        functools.partial(_even_in_kernel, head_major=head_major),
        grid=(nt,),
        in_specs=[row(d), _const_spec((1, d)), _const_spec(w.shape), _const_spec(wkvt.shape)],
        out_specs=out_specs, out_shape=out_shape,
        compiler_params=_params(1), name="even_in",
    )(x, g, w, wkvt)


def _log_sigmoid(x):
    return jnp.minimum(x, 0.0) - jnp.log1p(jnp.exp(-jnp.abs(x)))


def _mlstm_heads(heads, mask):
    qk = [_mm_nt(h["q"], h["k"]) for h in heads]
    ss, w_inters, floors = [], [], []
    for h, qk_h in zip(heads, qk):
        dmat = jnp.where(mask, h["bc"] - h["br"] + h["li_r"], -jnp.inf)
        inter = h["bc"] + h["m_rows"]
        m_t = jnp.maximum(inter, jnp.max(dmat, axis=-1, keepdims=True))
        ss.append(qk_h * jnp.exp(dmat - m_t))
        w_inters.append(jnp.exp(inter - m_t))
        floors.append(jnp.exp(-m_t))
    svs = [_mm(s.astype(BF16), h["v"]) for s, h in zip(ss, heads)]
    outs = []
    for h, s, sv, w_inter, floor in zip(heads, ss, svs, w_inters, floors):
        num = w_inter * h["cq"] + sv
        den = w_inter * h["nq"] + jnp.sum(s, axis=-1, keepdims=True)
        ho = _sigmoid(h["o"]) * (num / jnp.maximum(jnp.abs(den), floor))
        outs.append(ho * lax.rsqrt(jnp.mean(ho * ho, axis=-1, keepdims=True) + EPS) * h["hg"])
    return outs


def _mlstm_prompt_kernel(qk_ref, v_ref, o_ref, gates_ref, cw_ref, cb_ref, gb_ref, hg_ref,
                         h_ref, c_out, n_out, m_out, tail_out,
                         xp_ref, c_ref, n_ref, m_ref, *, n_chunks):
    ci = pl.program_id(1)
    L = ML_CHUNK
    halo = SUBLANES

    @pl.when(ci == 0)
    def _():
        xp_ref[0:halo, :] = jnp.zeros((halo, 2 * ML_WIDTH), F32)
        c_ref[...] = jnp.zeros_like(c_ref)
        n_ref[...] = jnp.zeros_like(n_ref)
        m_ref[...] = jnp.zeros_like(m_ref)

    xp_ref[halo:halo + L, :] = qk_ref[...]
    y = cb_ref[...]
    for i in range(ML_CONV):
        off = halo - (ML_CONV - 1) + i
        y = y + xp_ref[off:off + L, :] * cw_ref[i:i + 1, :]
    xp_ref[0:halo, :] = qk_ref[L - halo:L, :]
    qk = y * _sigmoid(y)

    gl = gates_ref[...] + gb_ref[...]
    logf = _log_sigmoid(gl)
    rows = lax.broadcasted_iota(jnp.int32, (L, L), 0)
    cols = lax.broadcasted_iota(jnp.int32, (L, L), 1)
    causal = rows >= cols
    tri = jnp.where(causal, 1.0, 0.0).astype(BF16)
    bcol = _cumsum_rows(tri, logf)
    bT = bcol.T
    glT = gl.T

    outs = []
    for h in range(ML_HEADS):
        sl = slice(h * ML_DH, (h + 1) * ML_DH)
        q = qk[:, sl]
        k = qk[:, ML_WIDTH + h * ML_DH:ML_WIDTH + (h + 1) * ML_DH] * (ML_DH ** -0.5)
        v = v_ref[:, sl]
        bc = bcol[:, ML_HEADS + h:ML_HEADS + h + 1]
        m_prev = m_ref[h:h + 1, 0:1]
        c_prev = c_ref[h]
        n_prev = n_ref[h:h + 1, :]
        q_bf, k_bf = q.astype(BF16), k.astype(BF16)
        head = dict(
            q=q_bf, k=k_bf, v=v.astype(BF16), o=o_ref[:, sl], hg=hg_ref[:, sl], bc=bc,
            br=bT[ML_HEADS + h:ML_HEADS + h + 1, :], li_r=glT[h:h + 1, :],
            m_rows=jnp.broadcast_to(m_prev, (L, 1)),
            cq=_mm_nt(q_bf, c_prev.astype(BF16)), nq=jnp.sum(q * n_prev, axis=-1, keepdims=True))
        outs += _mlstm_heads([head], causal)
        b_last = bc[L - 1:L, :]
        g = b_last - bc + gl[:, h:h + 1]
        m_new = jnp.maximum(b_last + m_prev, jnp.max(g, axis=0, keepdims=True))
        a_s = jnp.exp(g - m_new)
        a_c = jnp.exp(b_last + m_prev - m_new)
        c_ref[h] = a_c * c_prev + _mm((a_s * v).T.astype(BF16), k_bf)
        n_ref[h:h + 1, :] = a_c * n_prev + jnp.sum(a_s * k, axis=0, keepdims=True)
        m_ref[h:h + 1, :] = jnp.broadcast_to(m_new, (1, LANES))
    h_ref[...] = jnp.concatenate(outs, axis=-1).astype(h_ref.dtype)

    @pl.when(ci == n_chunks - 1)
    def _():
        c_out[0] = c_ref[...]
        n_out[0] = n_ref[...]
        m_out[0] = m_ref[...]
        tail_out[0] = qk_ref[L - halo:L, :]


def _cumsum_rows(tri, x):
    hi = x.astype(BF16)
    r1 = x - hi.astype(F32)
    mid = r1.astype(BF16)
    lo = (r1 - mid.astype(F32)).astype(BF16)
    return _mm(tri, hi) + _mm(tri, mid) + _mm(tri, lo)


def _mlstm_prompt(qk, v, o, gates, cw, cb, gb, hg, *, n_seq, seq_len):
    nc = seq_len // ML_CHUNK
    L = ML_CHUNK
    row = lambda width: pl.BlockSpec((L, width), lambda b, c: (b * nc + c, 0))
    per_seq = lambda *dims: pl.BlockSpec((1,) + dims, lambda b, c: (b,) + (0,) * len(dims))
    return pl.pallas_call(
        functools.partial(_mlstm_prompt_kernel, n_chunks=nc),
        grid=(n_seq, nc),
        in_specs=[row(2 * ML_WIDTH), row(ML_WIDTH), row(ML_WIDTH), row(LANES),
                  _const_spec(cw.shape), _const_spec(cb.shape), _const_spec(gb.shape), _const_spec(hg.shape)],
        out_specs=[row(ML_WIDTH), per_seq(ML_HEADS, ML_DH, ML_DH), per_seq(SUBLANES, ML_DH),
                   per_seq(SUBLANES, LANES), per_seq(SUBLANES, 2 * ML_WIDTH)],
        out_shape=[jax.ShapeDtypeStruct((n_seq * seq_len, ML_WIDTH), BF16),
                   jax.ShapeDtypeStruct((n_seq, ML_HEADS, ML_DH, ML_DH), F32),
                   jax.ShapeDtypeStruct((n_seq, SUBLANES, ML_DH), F32),
                   jax.ShapeDtypeStruct((n_seq, SUBLANES, LANES), F32),
                   jax.ShapeDtypeStruct((n_seq, SUBLANES, 2 * ML_WIDTH), F32)],
        scratch_shapes=[pltpu.VMEM((SUBLANES + L, 2 * ML_WIDTH), F32),
                        pltpu.VMEM((ML_HEADS, ML_DH, ML_DH), F32),
                        pltpu.VMEM((SUBLANES, ML_DH), F32),
                        pltpu.VMEM((SUBLANES, LANES), F32)],
        compiler_params=_params(2), name="mlstm_prompt",
    )(qk, v, o, gates, cw, cb, gb, hg)


def _mlstm_sample_kernel(qk_ref, v_ref, o_ref, gates_ref, st_ref, cw_ref, cb_ref, gb_ref, hg_ref,
                         c0_ref, n0_ref, m0_ref,
                         h_ref, c_out, n_out, m_out, ac_ref, *, n_seq, n_new):
    L = n_seq * n_new
    ext = jnp.concatenate([st_ref[...], qk_ref[...]], axis=0)
    y = cb_ref[...]
    for i in range(ML_CONV):
        y = y + ext[i * n_seq:i * n_seq + L, :] * cw_ref[i:i + 1, :]
    qk = y * _sigmoid(y)

    gl = gates_ref[...] + gb_ref[...]
    logf = _log_sigmoid(gl)
    tblk = lambda a, t: a[t * n_seq:(t + 1) * n_seq, :]
    b_t = [tblk(logf, 0)]
    for t in range(1, n_new):
        b_t.append(b_t[-1] + tblk(logf, t))
    bcol = jnp.concatenate(b_t, axis=0)
    bT = bcol.T
    glT = gl.T
    rows = lax.broadcasted_iota(jnp.int32, (L, L), 0)
    cols = lax.broadcasted_iota(jnp.int32, (L, L), 1)
    same_seq_causal = (rows >= cols) & (((rows - cols) % n_seq) == 0)
    row_seq = lax.broadcasted_iota(jnp.int32, (L, ML_DH), 0) % n_seq
    col_seq = lax.broadcasted_iota(jnp.int32, (ML_DH, L), 1) % n_seq
    m0 = m0_ref[...]
    m_out[...] = m0

    heads = []
    for h in range(ML_HEADS):
        sl = slice(h * ML_DH, (h + 1) * ML_DH)
        q = qk[:, sl]
        k = qk[:, ML_WIDTH + h * ML_DH:ML_WIDTH + (h + 1) * ML_DH] * (ML_DH ** -0.5)
        v = v_ref[:, sl]
        q_bf = q.astype(BF16)
        k_bf = k.astype(BF16)
        bc = bcol[:, ML_HEADS + h:ML_HEADS + h + 1]
        li_c = gl[:, h:h + 1]
        m0_h = m0[:, h:h + 1]
        m_rows = jnp.concatenate([m0_h] * n_new, axis=0)
        n0_h = n0_ref[:, sl]
        nq = jnp.sum(q * jnp.concatenate([n0_h] * n_new, axis=0), axis=-1, keepdims=True)

        b_last = tblk(bc, n_new - 1)
        g_t = [b_last - tblk(bc, t) + tblk(li_c, t) for t in range(n_new)]
        m_new = b_last + m0_h
        for t in range(n_new):
            m_new = jnp.maximum(m_new, g_t[t])
        a_s_t = [jnp.exp(g_t[t] - m_new) for t in range(n_new)]
        a_c = jnp.exp(b_last + m0_h - m_new)
        n_new_h = a_c * n0_h
        for t in range(n_new):
            n_new_h = n_new_h + a_s_t[t] * tblk(k, t)
        n_out[:, sl] = n_new_h
        m_out[:, h:h + 1] = m_new
        ac_ref[...] = jnp.broadcast_to(a_c, (n_seq, LANES))
        av_t = (jnp.concatenate(a_s_t, axis=0) * v).T

        def per_seq(b, cq):
            c_prev = c0_ref[b, h]
            part = _mm_nt(q_bf, c_prev.astype(BF16))
            cq = jnp.where(row_seq == b, part, cq)
            upd = _mm(jnp.where(col_seq == b, av_t, 0.0).astype(BF16), k_bf)
            c_out[b, h] = ac_ref[pl.ds(b, 1), :] * c_prev + upd
            return cq

        cq = lax.fori_loop(0, n_seq, per_seq, jnp.zeros((L, ML_DH), F32))
        heads.append(dict(q=q_bf, k=k_bf, v=v.astype(BF16), o=o_ref[:, sl], hg=hg_ref[:, sl], bc=bc,
                          br=bT[ML_HEADS + h:ML_HEADS + h + 1, :], li_r=glT[h:h + 1, :],
                          m_rows=m_rows, cq=cq, nq=nq))
    h_ref[...] = jnp.concatenate(_mlstm_heads(heads, same_seq_causal), axis=-1).astype(h_ref.dtype)


def _mlstm_sample(qk, v, o, gates, st, cw, cb, gb, hg, c0, n0, m0, *, n_seq, n_new):
    L = n_seq * n_new
    return pl.pallas_call(
        functools.partial(_mlstm_sample_kernel, n_seq=n_seq, n_new=n_new),
        out_shape=[jax.ShapeDtypeStruct((L, ML_WIDTH), BF16),
                   jax.ShapeDtypeStruct(c0.shape, F32),
                   jax.ShapeDtypeStruct(n0.shape, F32),
                   jax.ShapeDtypeStruct(m0.shape, F32)],
        scratch_shapes=[pltpu.VMEM((n_seq, LANES), F32)],
        compiler_params=pltpu.CompilerParams(vmem_limit_bytes=VMEM_LIMIT), name="mlstm_sample",
    )(qk, v, o, gates, st, cw, cb, gb, hg, c0, n0, m0)


def _neg_upper(n):
    rows = lax.broadcasted_iota(jnp.int32, (n, n), 0)
    cols = lax.broadcasted_iota(jnp.int32, (n, n), 1)
    return jnp.where(rows > cols, -1.0, 0.0).astype(BF16)


def _sb_weights(zs, neg_upper, rests, mask, chained=False, lo_pass=True):
    log_betas, totals, terms = [], [], []
    for z in zs:
        z2 = z * LOG2E
        sp = jnp.maximum(z2, 0.0) + jnp.log2(1.0 + jnp.exp2(-jnp.abs(z2)))
        if mask is not None:
            sp = jnp.where(mask, sp, 0.0)
        log_betas.append(z2 - sp), totals.append(sp[:, 0:1])
        terms.append(jnp.concatenate(_split_hi_lo(sp), axis=1) if lo_pass else sp.astype(BF16))
    upper = jnp.concatenate([neg_upper, neg_upper], axis=0) if lo_pass else neg_upper
    css = [_mm(t, upper) for t in terms]
    weights, new_rests = [], []
    for i, (lb, cs, tot) in enumerate(zip(log_betas, css, totals)):
        rest = new_rests[-1] if (chained and i) else rests[i]
        a = jnp.exp2(lb + cs + rest)
        if mask is not None:
            a = jnp.where(mask, a, 0.0)
        weights.append(a.astype(BF16))
        new_rests.append(rest + cs[:, 0:1] - tot)
    return weights, (new_rests[-1:] if chained else new_rests)


def _sb_prompt_kernel(bias_ref, q_ref, kt_ref, vt_ref, o_ref, kb_ref, vb_ref, z_ref, a_ref, *, blk):
    hg = q_ref.shape[1]
    hi_ = pl.program_id(1)
    qi = pl.program_id(2)
    n_blk = kb_ref.shape[1]
    heads = range(hg)

    @pl.when(qi == 0)
    def _():
        for hh in heads:
            for j in range(n_blk):
                kb_ref[hh, j] = kt_ref[0, hh, :, j * blk:(j + 1) * blk].astype(BF16)
                vb_ref[hh, j] = vt_ref[0, hh, :, j * blk:(j + 1) * blk].astype(BF16)

    neg_upper = _neg_upper(blk)
    strict_causal = (lax.broadcasted_iota(jnp.int32, (blk, blk), 1)
                     < lax.broadcasted_iota(jnp.int32, (blk, blk), 0))

    def logits(kb):
        return [_mm(q_ref[0, hh], kb_ref[hh, kb]) + bias_ref[hi_ * hg + hh] for hh in heads]

    def weighted_values(kb):
        return [_mm_nt(a_ref[hh], vb_ref[hh, kb]) for hh in heads]

    def visit(kb, mask, carry, first):
        accs, rests = carry
        zs = [z_ref[hh] for hh in heads]
        z_next = logits(jnp.maximum(kb - 1, 0))
        if not first:
            accs = tuple(a + p for a, p in zip(accs, weighted_values(kb + 1)))
        for hh in heads:
            z_ref[hh] = z_next[hh]
        weights, rests = _sb_weights(zs, neg_upper, rests, mask, lo_pass=False)
        for hh in heads:
            a_ref[hh] = weights[hh]
        return accs, tuple(rests)

    for hh, z in enumerate(logits(qi)):
        z_ref[hh] = z
    carry = ((jnp.zeros((blk, SB_DH), F32),) * hg, (jnp.zeros((blk, 1), F32),) * hg)
    carry = visit(qi, strict_causal, carry, True)
    accs, _ = lax.fori_loop(0, qi, lambda i, c: visit(qi - 1 - i, None, c, False), carry)
    accs = [a + p for a, p in zip(accs, weighted_values(0))]
    o_ref[...] = jnp.concatenate(accs, axis=-1).astype(o_ref.dtype)


def _sb_prompt(bias, qh, kt, vt, *, blk, heads_per_step):
    n_seq, n_heads, seq_len, dh = qh.shape
    nq = seq_len // blk
    hg = heads_per_step
    kv_spec = pl.BlockSpec((1, hg, dh, seq_len), lambda b, h, qi: (b, h, 0, 0))
    return pl.pallas_call(
        functools.partial(_sb_prompt_kernel, blk=blk),
        grid=(n_seq, n_heads // hg, nq),
        in_specs=[pl.BlockSpec(memory_space=pltpu.SMEM),
                  pl.BlockSpec((1, hg, blk, dh), lambda b, h, qi: (b, h, qi, 0)),
                  kv_spec, kv_spec],
        out_specs=pl.BlockSpec((blk, hg * dh), lambda b, h, qi: (b * nq + qi, h)),
        out_shape=jax.ShapeDtypeStruct((n_seq * seq_len, n_heads * dh), BF16),
        scratch_shapes=[pltpu.VMEM((hg, nq, dh, blk), BF16), pltpu.VMEM((hg, nq, dh, blk), BF16),
                        pltpu.VMEM((hg, blk, blk), F32), pltpu.VMEM((hg, blk, blk), BF16)],
        compiler_params=_params(3), name="sb_prompt",
    )(bias, qh, kt, vt)


def _sb_sample_kernel(pt_ref, q_ref, bias_ref, kn_ref, vn_ref, *refs, pages_per_step, n_new):
    del pt_ref
    k_refs = refs[:pages_per_step]
    v_refs = refs[pages_per_step:2 * pages_per_step]
    o_ref, acc_ref, rest_ref = refs[2 * pages_per_step:]
    j = pl.program_id(1)
    n_rows = n_new * SB_HEADS
    page = kn_ref.shape[1]
    head_of_col = lax.broadcasted_iota(jnp.int32, (n_rows, SB_WIDTH), 1) // SB_DH
    head_of_row = lax.broadcasted_iota(jnp.int32, (n_rows, SB_WIDTH), 0) % SB_HEADS
    own_head = head_of_col == head_of_row
    q = jnp.where(own_head, q_ref[0], 0.0).astype(BF16)
    bias = bias_ref[...]

    def visit(kts, vts, mask):
        zs = [_mm(q, kt) + bias for kt in kts]
        weights, rests = _sb_weights(zs, _neg_upper(kts[0].shape[1]), [rest_ref[:, 0:1]], mask, chained=True)
        acc_ref[...] += functools.reduce(lambda a, b: a + b, [_mm_nt(a, vt) for a, vt in zip(weights, vts)])
        rest_ref[...] = jnp.broadcast_to(rests[0], rest_ref.shape)

    @pl.when(j == 0)
    def _():
        acc_ref[...] = jnp.zeros_like(acc_ref)
        rest_ref[...] = jnp.zeros_like(rest_ref)
        t_of_row = lax.broadcasted_iota(jnp.int32, (n_rows, page), 0) // SB_HEADS
        s_of_col = lax.broadcasted_iota(jnp.int32, (n_rows, page), 1)
        visit([kn_ref[0].T.astype(BF16)], [vn_ref[0].T.astype(BF16)], s_of_col < t_of_row)

    def page_pair(p_refs, i):
        lo_hi = [p_refs[i + 1][0, 0].reshape(SB_WIDTH, page), p_refs[i][0, 0].reshape(SB_WIDTH, page)]
        return jnp.concatenate(lo_hi, axis=1).astype(BF16)

    pairs = range(0, pages_per_step, 2)
    visit([page_pair(k_refs, i) for i in pairs], [page_pair(v_refs, i) for i in pairs], None)

    @pl.when(j == pl.num_programs(1) - 1)
    def _():
        acc = jnp.where(own_head, acc_ref[...], 0.0)
        for t in range(n_new):
            o_ref[0, t:t + 1, :] = jnp.sum(acc[t * SB_HEADS:(t + 1) * SB_HEADS, :], axis=0, keepdims=True)


def _sb_sample(page_table, q_rows, bias_rows, k_new, v_new, cache_kt, cache_vt, *, layer, pages_per_step, n_new):
    n_seq, n_pages = page_table.shape
    page = cache_kt.shape[4]
    n_rows = n_new * SB_HEADS
    steps = n_pages // pages_per_step

    def page_spec(i):
        return pl.BlockSpec((1, 1, SB_HEADS, SB_DH, page),
                            lambda b, j, pt: (layer, pt[b, n_pages - 1 - (j * pages_per_step + i)], 0, 0, 0))

    per_seq = lambda r, c: pl.BlockSpec((1, r, c), lambda b, j, pt: (b, 0, 0))
    grid_spec = pltpu.PrefetchScalarGridSpec(
        num_scalar_prefetch=1, grid=(n_seq, steps),
        in_specs=[per_seq(n_rows, SB_WIDTH),
                  pl.BlockSpec((n_rows, 1), lambda b, j, pt: (0, 0)),
                  per_seq(page, SB_WIDTH), per_seq(page, SB_WIDTH)]
                 + [page_spec(i) for i in range(pages_per_step)] * 2,
        out_specs=per_seq(n_new, SB_WIDTH),
        scratch_shapes=[pltpu.VMEM((n_rows, SB_WIDTH), F32), pltpu.VMEM((n_rows, LANES), F32)])
    return pl.pallas_call(
        functools.partial(_sb_sample_kernel, pages_per_step=pages_per_step, n_new=n_new),
        grid_spec=grid_spec,
        out_shape=jax.ShapeDtypeStruct((n_seq, n_new, SB_WIDTH), F32),
        compiler_params=_params(2), name="sb_sample",
    )(page_table, q_rows, bias_rows, k_new, v_new,
      *([cache_kt] * pages_per_step), *([cache_vt] * pages_per_step))


def _proj_res_kernel(x_ref, a_ref, b_ref, w_ref, o_ref):
    ka = a_ref.shape[1]
    o_ref[...] = (x_ref[...] + _mm(a_ref[...].astype(BF16), w_ref[0:ka, :])
                  + _mm(b_ref[...].astype(BF16), w_ref[ka:, :]))


def _proj_res(x, a, b, w, *, tm):
    m, d = x.shape
    row = lambda width: pl.BlockSpec((tm, width), lambda i: (i, 0))
    return pl.pallas_call(
        _proj_res_kernel, grid=(m // tm,),
        in_specs=[row(d), row(a.shape[1]), row(b.shape[1]), _const_spec(w.shape)],
        out_specs=row(d), out_shape=jax.ShapeDtypeStruct((m, d), F32),
        compiler_params=_params(1), name="proj_res",
    )(x, a, b, w)


def _ffn_chunks(xn, win_ref, wout_ref, cw_ref, cb_ref, o_ref, conv_fn, d_ff):
    bounds = [(c0, min(c0 + FFN_CHUNK, d_ff)) for c0 in range(0, d_ff, FFN_CHUNK)]

    def up(c):
        lo, hi = bounds[c]
        return _mm(xn, win_ref[:, lo:hi]), _mm(xn, win_ref[:, d_ff + lo:d_ff + hi])

    nxt = up(0)
    for c in range(len(bounds)):
        sl = slice(*bounds[c])
        g, u = nxt
        if c + 1 < len(bounds):
            nxt = up(c + 1)
        g_m2, g_m1 = conv_fn(g, sl)
        y = cb_ref[:, sl] + g_m2 * cw_ref[0:1, sl] + g_m1 * cw_ref[1:2, sl] + g * cw_ref[2:3, sl]
        hmid = (_gelu(y) * u).astype(BF16)
        o_ref[...] += _mm(hmid, wout_ref[sl, :])


def _ffn_prompt_kernel(x_ref, g_ref, win_ref, cw_ref, cb_ref, wout_ref, fg_ref, *rest,
                       tiles_per_seq, d_ff, final_norm, mixer_proj):
    if mixer_proj:
        a_ref, b_ref, wmix_ref, o_ref, tail_out, xn_ref, prev_ref, st_ref = rest
    else:
        o_ref, tail_out, xn_ref, prev_ref, st_ref = rest
    i = pl.program_id(0)
    tm = x_ref.shape[0]
    halo = SUBLANES

    @pl.when(i % tiles_per_seq == 0)
    def _():
        prev_ref[...] = jnp.zeros_like(prev_ref)

    x = x_ref[...]
    if mixer_proj:
        ka = a_ref.shape[1]
        x = x + _mm(a_ref[...], wmix_ref[0:ka, :]) + _mm(b_ref[...], wmix_ref[ka:, :])
    xn_ref[...] = _rms(x, g_ref[...]).astype(BF16)
    o_ref[...] = x

    def conv_fn(g, sl):
        w = sl.stop - sl.start
        st_ref[0:halo, 0:w] = prev_ref[:, sl]
        st_ref[halo:halo + tm, 0:w] = g
        prev_ref[:, sl] = g[tm - halo:tm, :]
        return st_ref[halo - 2:halo - 2 + tm, 0:w], st_ref[halo - 1:halo - 1 + tm, 0:w]

    _ffn_chunks(xn_ref[...], win_ref, wout_ref, cw_ref, cb_ref, o_ref, conv_fn, d_ff)
    if final_norm:
        o_ref[...] = _rms(o_ref[...], fg_ref[...])

    @pl.when(i % tiles_per_seq == tiles_per_seq - 1)
    def _():
        tail_out[0] = prev_ref[...]


def _ffn_prompt(x, g, win, cw, cb, wout, fg, mixer=None, *, n_seq, seq_len, tm, final_norm):
    m, d = x.shape
    d_ff = wout.shape[0]
    tps = seq_len // tm
    row = lambda width: pl.BlockSpec((tm, width), lambda i: (i, 0))
    operands = [x, g, win, cw, cb, wout, fg]
    in_specs = [row(d)] + [_const_spec(a.shape) for a in operands[1:]]
    if mixer is not None:
        a, b, w_mix = mixer
        operands += [a, b, w_mix]
        in_specs += [row(a.shape[1]), row(b.shape[1]), _const_spec(w_mix.shape)]
    return pl.pallas_call(
        functools.partial(_ffn_prompt_kernel, tiles_per_seq=tps, d_ff=d_ff, final_norm=final_norm,
                          mixer_proj=mixer is not None),
        grid=(m // tm,),
        in_specs=in_specs,
        out_specs=[row(d), pl.BlockSpec((1, SUBLANES, d_ff), lambda i: (i // tps, 0, 0))],
        out_shape=[jax.ShapeDtypeStruct((m, d), F32), jax.ShapeDtypeStruct((n_seq, SUBLANES, d_ff), F32)],
        scratch_shapes=[pltpu.VMEM((tm, d), BF16), pltpu.VMEM((SUBLANES, d_ff), F32),
                        pltpu.VMEM((SUBLANES + tm, FFN_CHUNK), F32)],
        compiler_params=_params(1), name="ffn_prompt",
    )(*operands)


def _ffn_sample_kernel(x_ref, g_ref, win_ref, cw_ref, cb_ref, wout_ref, fg_ref, st_ref, o_ref, tail_out,
                       *, n_seq, d_ff, final_norm):
    rows = x_ref.shape[0]
    x = x_ref[...]
    xn = _rms(x, g_ref[...]).astype(BF16)
    o_ref[...] = x

    def conv_fn(g, sl):
        ext = jnp.concatenate([st_ref[:, sl], g], axis=0)
        tail_out[:, sl] = g[rows - 2 * n_seq:rows, :]
        return ext[0:rows, :], ext[n_seq:n_seq + rows, :]

    _ffn_chunks(xn, win_ref, wout_ref, cw_ref, cb_ref, o_ref, conv_fn, d_ff)
    if final_norm:
        o_ref[...] = _rms(o_ref[...], fg_ref[...])


def _ffn_sample(x, g, win, cw, cb, wout, fg, st, *, n_seq, final_norm):
    d_ff = wout.shape[0]
    return pl.pallas_call(
        functools.partial(_ffn_sample_kernel, n_seq=n_seq, d_ff=d_ff, final_norm=final_norm),
        out_shape=[jax.ShapeDtypeStruct(x.shape, F32), jax.ShapeDtypeStruct(st.shape, F32)],
        compiler_params=pltpu.CompilerParams(vmem_limit_bytes=VMEM_LIMIT), name="ffn_sample",
    )(x, g, win, cw, cb, wout, fg, st)


def _layernorm_stats(v_ref, width, n_groups):
    gw = width // n_groups
    s1 = 0.0
    for gi in range(n_groups):
        s1 = s1 + jnp.sum(v_ref[:, gi * gw:(gi + 1) * gw], axis=-1, keepdims=True)
    mean = s1 / width
    s2 = 0.0
    for gi in range(n_groups):
        xc = v_ref[:, gi * gw:(gi + 1) * gw] - mean
        s2 = s2 + jnp.sum(xc * xc, axis=-1, keepdims=True)
    return mean, lax.rsqrt(s2 / width + EPS)


def _cm_prompt_kernel(x_ref, g_ref, win_ref, lng_ref, lnb_ref, ws_ref, bst_ref, wout_ref, o_ref,
                      xn_ref, v_ref, *, width):
    tm = x_ref.shape[0]
    gw = width // CM_GROUPS
    x = x_ref[...]
    xn_ref[...] = _rms(x, g_ref[...]).astype(BF16)
    o_ref[...] = x
    proj = lambda col0, gi: _mm(xn_ref[...], win_ref[:, col0 + gi * gw:col0 + (gi + 1) * gw])
    nxt = proj(width, 0)
    for gi in range(CM_GROUPS):
        cur, nxt = nxt, (proj(width, gi + 1) if gi + 1 < CM_GROUPS else proj(0, 0))
        v_ref[:, gi * gw:(gi + 1) * gw] = _gelu(cur)
    mean, rstd = _layernorm_stats(v_ref, width, CM_GROUPS)
    rows = lax.broadcasted_iota(jnp.int32, (CM_CHUNK, CM_CHUNK), 0)
    cols = lax.broadcasted_iota(jnp.int32, (CM_CHUNK, CM_CHUNK), 1)
    tril = rows >= cols
    gated = []
    for gi in range(CM_GROUPS):
        sl = slice(gi * gw, (gi + 1) * gw)
        u_pre, nxt = nxt, (proj(0, gi + 1) if gi + 1 < CM_GROUPS else None)
        vn = ((v_ref[:, sl] - mean) * rstd * lng_ref[:, sl] + lnb_ref[:, sl]).astype(BF16)
        wsg = jnp.where(tril, ws_ref[gi], 0.0).astype(BF16)
        bias = bst_ref[:, gi:gi + 1]
        mixed = jnp.concatenate(
            [_mm(wsg, vn[c * CM_CHUNK:(c + 1) * CM_CHUNK, :]) + bias for c in range(tm // CM_CHUNK)], axis=0)
        gated.append((_gelu(u_pre) * mixed).astype(BF16))
        if len(gated) == CM_OUT_GROUPS:
            lo = (gi + 1 - CM_OUT_GROUPS) * gw
            o_ref[...] += _mm(jnp.concatenate(gated, axis=1), wout_ref[lo:(gi + 1) * gw, :])
            gated = []


def _cm_prompt(x, g, win, lng, lnb, ws, bst, wout, *, tm):
    m, d = x.shape
    width = wout.shape[0]
    row = pl.BlockSpec((tm, d), lambda i: (i, 0))
    return pl.pallas_call(
        functools.partial(_cm_prompt_kernel, width=width),
        grid=(m // tm,),
        in_specs=[row] + [_const_spec(a.shape) for a in (g, win, lng, lnb, ws, bst, wout)],
        out_specs=row, out_shape=jax.ShapeDtypeStruct((m, d), F32),
        scratch_shapes=[pltpu.VMEM((tm, d), BF16), pltpu.VMEM((tm, width), F32)],
        compiler_params=_params(1), name="cm_prompt",
    )(x, g, win, lng, lnb, ws, bst, wout)


def _cm_sample_kernel(x_ref, g_ref, win_ref, lng_ref, lnb_ref, wexp_ref, bexp_ref, wout_ref, o_ref, v_out,
                      *, n_seq, n_new, width):
    gw = width // CM_GROUPS
    x = x_ref[...]
    xn = _rms(x, g_ref[...]).astype(BF16)
    for gi in range(CM_GROUPS):
        v_out[:, gi * gw:(gi + 1) * gw] = _gelu(_mm(xn, win_ref[:, width + gi * gw:width + (gi + 1) * gw]))
    mean, rstd = _layernorm_stats(v_out, width, CM_GROUPS)
    acc = x
    for gi in range(CM_GROUPS):
        sl = slice(gi * gw, (gi + 1) * gw)
        vn = (v_out[:, sl] - mean) * rstd * lng_ref[:, sl] + lnb_ref[:, sl]
        v_out[:, sl] = vn
        mixed = []
        for t in range(n_new):
            mt = bexp_ref[t:t + 1, sl]
            for s in range(t + 1):
                mt = mt + wexp_ref[t * n_new + s:t * n_new + s + 1, sl] * vn[s * n_seq:(s + 1) * n_seq, :]
            mixed.append(mt)
        u = _gelu(_mm(xn, win_ref[:, sl]))
        acc = acc + _mm((u * jnp.concatenate(mixed, axis=0)).astype(BF16), wout_ref[sl, :])
    o_ref[...] = acc


def _cm_sample(x, g, win, lng, lnb, wexp, bexp, wout, *, n_seq, n_new):
    width = wout.shape[0]
    return pl.pallas_call(
        functools.partial(_cm_sample_kernel, n_seq=n_seq, n_new=n_new, width=width),
        out_shape=[jax.ShapeDtypeStruct(x.shape, F32), jax.ShapeDtypeStruct((x.shape[0], width), F32)],
        compiler_params=pltpu.CompilerParams(vmem_limit_bytes=VMEM_LIMIT), name="cm_sample",
    )(x, g, win, lng, lnb, wexp, bexp, wout)


def _time_major(a):
    return jnp.swapaxes(a, 0, 1).reshape((a.shape[0] * a.shape[1],) + a.shape[2:])


def _batch_major(a, n_seq):
    return jnp.swapaxes(a.reshape((a.shape[0] // n_seq, n_seq) + a.shape[1:]), 0, 1)


def _pad_lanes(a):
    return jnp.pad(a, [(0, 0)] * (a.ndim - 1) + [(0, LANES - a.shape[-1])])


def kernel(x_prompt, x_sample, cache_k, cache_v, page_table, state_mlstm_c, state_mlstm_n, state_mlstm_m, state_mlstm_conv, state_ffn_conv, norm_mix_g, norm_ffn_g, norm_final_g, w_in_even, ml_conv_w, ml_conv_b, ml_gate_b, ml_head_g, sb_logit_b, w_out_even, w_in_odd, cm_ln_g, cm_ln_b, cm_spatial_w, cm_spatial_b, w_out_odd, ffn_w_in, ffn_conv_w, ffn_conv_b, ffn_w_out):
    n_pb, seq_p, d = x_prompt.shape
    n_sb, seq_s, _ = x_sample.shape
    depth = norm_mix_g.shape[0]
    d_ff = ffn_w_out.shape[1]
    hp = x_prompt.reshape(n_pb * seq_p, d)
    hs = _time_major(x_sample)
    fg = norm_final_g.reshape(1, d)

    kp_l, vp_l, ks_l, vs_l = [], [], [], []
    cp_l, np_l, mp_l, bp_l = [], [], [], []
    cs_l, ns_l, ms_l, bs_l = [], [], [], []
    cmv_l, fp_l, fs_l = [], [], []
    for layer in range(depth):
        mix_g = norm_mix_g[layer].reshape(1, d)
        if layer % 2 == 0:
            e = layer // 2
            w = w_in_even[e]
            gate_lo = _C_O + ML_WIDTH
            gate_hi = gate_lo + 2 * ML_HEADS
            w_all = jnp.concatenate([w[:, :gate_lo], w[:, gate_hi:], _pad_lanes(w[:, gate_lo:gate_hi])],
                                    axis=1).astype(BF16)
            w_kvt = w[:, gate_hi + SB_WIDTH:].T.astype(BF16)
            cw, cb = ml_conv_w[e], ml_conv_b[e].reshape(1, -1)
            gb = _pad_lanes(ml_gate_b[e].reshape(1, -1))
            hg = ml_head_g[e].reshape(1, -1)
            w_out = w_out_even[e].astype(BF16)
            qk, vm, om, gates, qh, kt, vt = _even_in(
                hp, mix_g, w_all, w_kvt, n_seq=n_pb, seq_len=seq_p, tm=512, head_major=True)
            h_ml, c_p, n_p, m_p, tail_p = _mlstm_prompt(qk, vm, om, gates, cw, cb, gb, hg, n_seq=n_pb, seq_len=seq_p)
            h_sb = _sb_prompt(sb_logit_b[e], qh, kt, vt, blk=256, heads_per_step=4)
            prompt_mixer = (h_ml, h_sb, w_out)
            kp_l.append(jnp.transpose(kt, (0, 3, 1, 2)))
            vp_l.append(jnp.transpose(vt, (0, 3, 1, 2)))
            cp_l.append(c_p)
            np_l.append(n_p[:, :ML_HEADS])
            mp_l.append(m_p[:, :ML_HEADS, 0])
            bp_l.append(tail_p[:, SUBLANES - (ML_CONV - 1):])
            qk, vm, om, gates, sq, sk, sv = _even_in(
                hs, mix_g, w_all, w_kvt, n_seq=n_sb, seq_len=seq_s, tm=n_sb * seq_s, head_major=False)
            h_ml, c_s, n_s, m_s = _mlstm_sample(
                qk, vm, om, gates, _time_major(state_mlstm_conv[e]), cw, cb, gb, hg,
                state_mlstm_c[e], state_mlstm_n[e].reshape(n_sb, ML_WIDTH), _pad_lanes(state_mlstm_m[e]),
                n_seq=n_sb, n_new=seq_s)
            page = cache_k.shape[2]
            pad_keys = lambda a: jnp.pad(_batch_major(a, n_sb), ((0, 0), (0, page - seq_s), (0, 0)))
            q_rows = jnp.broadcast_to(_batch_major(sq, n_sb)[:, :, None, :],
                                      (n_sb, seq_s, SB_HEADS, SB_WIDTH)).reshape(n_sb, seq_s * SB_HEADS, SB_WIDTH)
            bias_rows = jnp.tile(sb_logit_b[e], seq_s).reshape(seq_s * SB_HEADS, 1)
            pool_view = lambda c: jnp.transpose(c, (0, 1, 3, 4, 2))
            h_sb = _sb_sample(page_table, q_rows, bias_rows, pad_keys(sk), pad_keys(sv),
                              pool_view(cache_k), pool_view(cache_v), layer=e, pages_per_step=32, n_new=seq_s)
            hs = _proj_res(hs, h_ml, _time_major(h_sb), w_out, tm=n_sb * seq_s)
            ks_l.append(_batch_major(sk, n_sb).reshape(n_sb, seq_s, SB_HEADS, SB_DH))
            vs_l.append(_batch_major(sv, n_sb).reshape(n_sb, seq_s, SB_HEADS, SB_DH))
            cs_l.append(c_s)
            ns_l.append(n_s.reshape(n_sb, ML_HEADS, ML_DH))
            ms_l.append(m_s[:, :ML_HEADS])
            bs_l.append(_batch_major(qk[(seq_s - (ML_CONV - 1)) * n_sb:], n_sb))
        else:
            o = layer // 2
            win = w_in_odd[o].astype(BF16)
            wout = w_out_odd[o].astype(BF16)
            lng, lnb = cm_ln_g[o].reshape(1, -1), cm_ln_b[o].reshape(1, -1)
            width = wout.shape[0]
            gw = width // CM_GROUPS
            prompt_mixer = None
            hp = _cm_prompt(hp, mix_g, win, lng, lnb, cm_spatial_w[o], cm_spatial_b[o].T, wout, tm=512)
            wexp = jnp.repeat(cm_spatial_w[o][:, :seq_s, :seq_s].reshape(CM_GROUPS, seq_s * seq_s).T, gw, axis=1)
            bexp = jnp.repeat(cm_spatial_b[o][:, :seq_s].T, gw, axis=1)
            hs, v_rows = _cm_sample(hs, mix_g, win, lng, lnb, wexp, bexp, wout, n_seq=n_sb, n_new=seq_s)
            cmv_l.append(_batch_major(v_rows, n_sb))
        ffn_g = norm_ffn_g[layer].reshape(1, d)
        win = ffn_w_in[layer].astype(BF16)
        wout = ffn_w_out[layer].astype(BF16)
        cw, cb = ffn_conv_w[layer], ffn_conv_b[layer].reshape(1, -1)
        last = layer == depth - 1
        hp, tail_p = _ffn_prompt(hp, ffn_g, win, cw, cb, wout, fg, prompt_mixer, n_seq=n_pb, seq_len=seq_p,
                                 tm=1024, final_norm=last)
        hs, tail_s = _ffn_sample(hs, ffn_g, win, cw, cb, wout, fg, _time_major(state_ffn_conv[layer]),
                                 n_seq=n_sb, final_norm=last)
        fp_l.append(tail_p[:, SUBLANES - (FFN_CONV - 1):])
        fs_l.append(_batch_major(tail_s, n_sb))

    return (hp.reshape(n_pb, seq_p, d), _batch_major(hs, n_sb),
            jnp.stack(kp_l), jnp.stack(vp_l), jnp.stack(ks_l), jnp.stack(vs_l),
            jnp.stack(cp_l), jnp.stack(np_l), jnp.stack(mp_l), jnp.stack(bp_l),
            jnp.stack(cs_l), jnp.stack(ns_l), jnp.stack(ms_l), jnp.stack(bs_l),
            jnp.stack(cmv_l), jnp.stack(fp_l), jnp.stack(fs_l))
```

```python
import functools

import jax
import jax.numpy as jnp
from jax import lax
from jax.experimental import pallas as pl
from jax.experimental.pallas import tpu as pltpu

F32 = jnp.float32
BF16 = jnp.bfloat16
EPS = 1e-6
LOG2E = 1.4426950408889634

ML_HEADS = 4
ML_DH = 128
ML_WIDTH = ML_HEADS * ML_DH
ML_CONV = 4
ML_CHUNK = 128
SB_HEADS = 8
SB_DH = 64
SB_WIDTH = SB_HEADS * SB_DH
SB_SCALE = SB_DH ** -0.5
CM_GROUPS = 8
CM_CHUNK = 128
CM_OUT_GROUPS = 2
FFN_CONV = 3
LANES = 128
SUBLANES = 8
MXU_N = 256
FFN_CHUNK = 2 * MXU_N
VMEM_LIMIT = 56 * 1024 * 1024

EVEN_IN_ROWS = 512
CM_ROWS = 512
FFN_ROWS = 1024
SB_BLOCK = 256
SB_HEADS_PER_STEP = 4
SB_PAGES_PER_STEP = 32


def _mm(a, b):
    return jnp.dot(a, b, preferred_element_type=F32)


def _mm_nt(a, b):
    return lax.dot_general(a, b, (((1,), (1,)), ((), ())), preferred_element_type=F32)


def _rms(x, g):
    return x * lax.rsqrt(jnp.mean(x * x, axis=-1, keepdims=True) + EPS) * g


def _sigmoid(x):
    return 1.0 / (1.0 + jnp.exp(-x))


def _gelu(x):
    c = 0.7978845608028654
    return 0.5 * x * (1.0 + jnp.tanh(c * (x + 0.044715 * (x * x * x))))


def _split_hi_lo(x):
    hi = x.astype(BF16)
    lo = (x - hi.astype(F32)).astype(BF16)
    return hi, lo


def _params(n_axes):
    return pltpu.CompilerParams(dimension_semantics=("arbitrary",) * n_axes,
                                vmem_limit_bytes=VMEM_LIMIT)


def _const_spec(shape):
    nd = len(shape)
    return pl.BlockSpec(shape, lambda *_: (0,) * nd, pipeline_mode=pl.Buffered(1))


_C_QK = 0
_C_V = 2 * ML_WIDTH
_C_O = _C_V + ML_WIDTH
_C_SQ = _C_O + ML_WIDTH
_C_SK = _C_SQ + SB_WIDTH
_C_SV = _C_SK + SB_WIDTH
_C_G = _C_SV + SB_WIDTH
_C_END = _C_G + LANES


def _even_in_kernel(x_ref, g_ref, w_ref, wkvt_ref, *out_refs, head_major):
    xn = _rms(x_ref[...], g_ref[...]).astype(BF16)
    qk_ref, v_ref, o_ref, gates_ref, sq_ref, sk_ref, sv_ref = out_refs
    qk_ref[...] = _mm(xn, w_ref[:, _C_QK:_C_V])
    v_ref[...] = _mm(xn, w_ref[:, _C_V:_C_O])
    o_ref[...] = _mm(xn, w_ref[:, _C_O:_C_SQ])
    gates_ref[...] = _mm(xn, w_ref[:, _C_G:_C_END])
    sq = _mm(xn, w_ref[:, _C_SQ:_C_SK]) * SB_SCALE
    if head_major:
        tm = sq.shape[0]
        for h in range(SB_HEADS):
            sq_ref[0, h] = sq[:, h * SB_DH:(h + 1) * SB_DH].astype(BF16)
        sk_ref[0] = _mm_nt(wkvt_ref[0:SB_WIDTH, :], xn).reshape(SB_HEADS, SB_DH, tm)
        sv_ref[0] = _mm_nt(wkvt_ref[SB_WIDTH:, :], xn).reshape(SB_HEADS, SB_DH, tm)
    else:
        sq_ref[...] = sq
        sk_ref[...] = _mm(xn, w_ref[:, _C_SK:_C_SV])
        sv_ref[...] = _mm(xn, w_ref[:, _C_SV:_C_G])


def _even_in(x, g, w, wkvt, *, n_seq, seq_len, tm, head_major):
    m, d = x.shape
    nt = m // tm
    row = lambda width: pl.BlockSpec((tm, width), lambda i: (i, 0))
    out_shape = [jax.ShapeDtypeStruct((m, 2 * ML_WIDTH), F32),
                 jax.ShapeDtypeStruct((m, ML_WIDTH), F32),
                 jax.ShapeDtypeStruct((m, ML_WIDTH), F32),
                 jax.ShapeDtypeStruct((m, LANES), F32)]
    out_specs = [row(2 * ML_WIDTH), row(ML_WIDTH), row(ML_WIDTH), row(LANES)]
    if head_major:
        tps = seq_len // tm
        out_shape += [jax.ShapeDtypeStruct((n_seq, SB_HEADS, seq_len, SB_DH), BF16)]
        out_specs += [pl.BlockSpec((1, SB_HEADS, tm, SB_DH), lambda i: (i // tps, 0, i % tps, 0))]
        out_shape += [jax.ShapeDtypeStruct((n_seq, SB_HEADS, SB_DH, seq_len), F32)] * 2
        out_specs += [pl.BlockSpec((1, SB_HEADS, SB_DH, tm), lambda i: (i // tps, 0, 0, i % tps))] * 2
    else:
        out_shape += [jax.ShapeDtypeStruct((m, SB_WIDTH), F32)] * 3
        out_specs += [row(SB_WIDTH)] * 3
    return pl.pallas_call(
        functools.partial(_even_in_kernel, head_major=head_major),
        grid=(nt,),
        in_specs=[row(d), _const_spec((1, d)), _const_spec(w.shape), _const_spec(wkvt.shape)],
        out_specs=out_specs, out_shape=out_shape,
        compiler_params=_params(1), name="even_in",
    )(x, g, w, wkvt)


def _log_sigmoid(x):
    return jnp.minimum(x, 0.0) - jnp.log1p(jnp.exp(-jnp.abs(x)))


def _mlstm_heads(heads, mask):
    qk = [_mm_nt(h["q"], h["k"]) for h in heads]
    ss, w_inters, floors = [], [], []
    for h, qk_h in zip(heads, qk):
        dmat = jnp.where(mask, h["bc"] - h["br"] + h["li_r"], -jnp.inf)
        inter = h["bc"] + h["m_rows"]
        m_t = jnp.maximum(inter, jnp.max(dmat, axis=-1, keepdims=True))
        ss.append(qk_h * jnp.exp(dmat - m_t))
        w_inters.append(jnp.exp(inter - m_t))
        floors.append(jnp.exp(-m_t))
    svs = [_mm(s.astype(BF16), h["v"]) for s, h in zip(ss, heads)]
    outs = []
    for h, s, sv, w_inter, floor in zip(heads, ss, svs, w_inters, floors):
        num = w_inter * h["cq"] + sv
        den = w_inter * h["nq"] + jnp.sum(s, axis=-1, keepdims=True)
        ho = _sigmoid(h["o"]) * (num / jnp.maximum(jnp.abs(den), floor))
        outs.append(ho * lax.rsqrt(jnp.mean(ho * ho, axis=-1, keepdims=True) + EPS) * h["hg"])
    return outs


def _mlstm_prompt_kernel(qk_ref, v_ref, o_ref, gates_ref, cw_ref, cb_ref, gb_ref, hg_ref,
                         h_ref, c_out, n_out, m_out, tail_out,
                         xp_ref, c_ref, n_ref, m_ref, *, n_chunks):
    ci = pl.program_id(1)
    L = ML_CHUNK
    halo = SUBLANES

    @pl.when(ci == 0)
    def _():
        xp_ref[0:halo, :] = jnp.zeros((halo, 2 * ML_WIDTH), F32)
        c_ref[...] = jnp.zeros_like(c_ref)
        n_ref[...] = jnp.zeros_like(n_ref)
        m_ref[...] = jnp.zeros_like(m_ref)

    xp_ref[halo:halo + L, :] = qk_ref[...]
    y = cb_ref[...]
    for i in range(ML_CONV):
        off = halo - (ML_CONV - 1) + i
        y = y + xp_ref[off:off + L, :] * cw_ref[i:i + 1, :]
    xp_ref[0:halo, :] = qk_ref[L - halo:L, :]
    qk = y * _sigmoid(y)

    gl = gates_ref[...] + gb_ref[...]
    logf = _log_sigmoid(gl)
    rows = lax.broadcasted_iota(jnp.int32, (L, L), 0)
    cols = lax.broadcasted_iota(jnp.int32, (L, L), 1)
    causal = rows >= cols
    tri = jnp.where(causal, 1.0, 0.0).astype(BF16)
    bcol = _cumsum_rows(tri, logf)
    bT = bcol.T
    glT = gl.T

    outs, pending = [], []
    for h in range(ML_HEADS):
        sl = slice(h * ML_DH, (h + 1) * ML_DH)
        q = qk[:, sl]
        k = qk[:, ML_WIDTH + h * ML_DH:ML_WIDTH + (h + 1) * ML_DH] * (ML_DH ** -0.5)
        v = v_ref[:, sl]
        bc = bcol[:, ML_HEADS + h:ML_HEADS + h + 1]
        m_prev = m_ref[h:h + 1, 0:1]
        c_prev = c_ref[h]
        n_prev = n_ref[h:h + 1, :]
        q_bf, k_bf = q.astype(BF16), k.astype(BF16)
        head = dict(
            q=q_bf, k=k_bf, v=v.astype(BF16), o=o_ref[:, sl], hg=hg_ref[:, sl], bc=bc,
            br=bT[ML_HEADS + h:ML_HEADS + h + 1, :], li_r=glT[h:h + 1, :],
            m_rows=jnp.broadcast_to(m_prev, (L, 1)),
            cq=_mm_nt(q_bf, c_prev.astype(BF16)), nq=jnp.sum(q * n_prev, axis=-1, keepdims=True))
        pending.append(head)
        if len(pending) == 2:
            outs += _mlstm_heads(pending, causal)
            pending = []
        b_last = bc[L - 1:L, :]
        g = b_last - bc + gl[:, h:h + 1]
        m_new = jnp.maximum(b_last + m_prev, jnp.max(g, axis=0, keepdims=True))
        a_s = jnp.exp(g - m_new)
        a_c = jnp.exp(b_last + m_prev - m_new)
        c_ref[h] = a_c * c_prev + _mm((a_s * v).T.astype(BF16), k_bf)
        n_ref[h:h + 1, :] = a_c * n_prev + jnp.sum(a_s * k, axis=0, keepdims=True)
        m_ref[h:h + 1, :] = jnp.broadcast_to(m_new, (1, LANES))
    h_ref[...] = jnp.concatenate(outs, axis=-1).astype(h_ref.dtype)

    @pl.when(ci == n_chunks - 1)
    def _():
        c_out[0] = c_ref[...]
        n_out[0] = n_ref[...]
        m_out[0] = m_ref[...]
        tail_out[0] = qk_ref[L - halo:L, :]


def _cumsum_rows(tri, x):
    hi = x.astype(BF16)
    r1 = x - hi.astype(F32)
    mid = r1.astype(BF16)
    lo = (r1 - mid.astype(F32)).astype(BF16)
    return _mm(tri, hi) + _mm(tri, mid) + _mm(tri, lo)


def _mlstm_prompt(qk, v, o, gates, cw, cb, gb, hg, *, n_seq, seq_len):
    nc = seq_len // ML_CHUNK
    L = ML_CHUNK
    row = lambda width: pl.BlockSpec((L, width), lambda b, c: (b * nc + c, 0))
    per_seq = lambda *dims: pl.BlockSpec((1,) + dims, lambda b, c: (b,) + (0,) * len(dims))
    return pl.pallas_call(
        functools.partial(_mlstm_prompt_kernel, n_chunks=nc),
        grid=(n_seq, nc),
        in_specs=[row(2 * ML_WIDTH), row(ML_WIDTH), row(ML_WIDTH), row(LANES),
                  _const_spec(cw.shape), _const_spec(cb.shape), _const_spec(gb.shape), _const_spec(hg.shape)],
        out_specs=[row(ML_WIDTH), per_seq(ML_HEADS, ML_DH, ML_DH), per_seq(SUBLANES, ML_DH),
                   per_seq(SUBLANES, LANES), per_seq(SUBLANES, 2 * ML_WIDTH)],
        out_shape=[jax.ShapeDtypeStruct((n_seq * seq_len, ML_WIDTH), BF16),
                   jax.ShapeDtypeStruct((n_seq, ML_HEADS, ML_DH, ML_DH), F32),
                   jax.ShapeDtypeStruct((n_seq, SUBLANES, ML_DH), F32),
                   jax.ShapeDtypeStruct((n_seq, SUBLANES, LANES), F32),
                   jax.ShapeDtypeStruct((n_seq, SUBLANES, 2 * ML_WIDTH), F32)],
        scratch_shapes=[pltpu.VMEM((SUBLANES + L, 2 * ML_WIDTH), F32),
                        pltpu.VMEM((ML_HEADS, ML_DH, ML_DH), F32),
                        pltpu.VMEM((SUBLANES, ML_DH), F32),
                        pltpu.VMEM((SUBLANES, LANES), F32)],
        compiler_params=_params(2), name="mlstm_prompt",
    )(qk, v, o, gates, cw, cb, gb, hg)


def _mlstm_sample_kernel(qk_ref, v_ref, o_ref, gates_ref, st_ref, cw_ref, cb_ref, gb_ref, hg_ref,
                         c0_ref, n0_ref, m0_ref,
                         h_ref, c_out, n_out, m_out, ac_ref, *, n_seq, n_new):
    L = n_seq * n_new
    ext = jnp.concatenate([st_ref[...], qk_ref[...]], axis=0)
    y = cb_ref[...]
    for i in range(ML_CONV):
        y = y + ext[i * n_seq:i * n_seq + L, :] * cw_ref[i:i + 1, :]
    qk = y * _sigmoid(y)

    gl = gates_ref[...] + gb_ref[...]
    logf = _log_sigmoid(gl)
    tblk = lambda a, t: a[t * n_seq:(t + 1) * n_seq, :]
    b_t = [tblk(logf, 0)]
    for t in range(1, n_new):
        b_t.append(b_t[-1] + tblk(logf, t))
    bcol = jnp.concatenate(b_t, axis=0)
    bT = bcol.T
    glT = gl.T
    rows = lax.broadcasted_iota(jnp.int32, (L, L), 0)
    cols = lax.broadcasted_iota(jnp.int32, (L, L), 1)
    same_seq_causal = (rows >= cols) & (((rows - cols) % n_seq) == 0)
    row_seq = lax.broadcasted_iota(jnp.int32, (L, ML_DH), 0) % n_seq
    col_seq = lax.broadcasted_iota(jnp.int32, (ML_DH, L), 1) % n_seq
    m0 = m0_ref[...]
    m_out[...] = m0

    heads = []
    for h in range(ML_HEADS):
        sl = slice(h * ML_DH, (h + 1) * ML_DH)
        q = qk[:, sl]
        k = qk[:, ML_WIDTH + h * ML_DH:ML_WIDTH + (h + 1) * ML_DH] * (ML_DH ** -0.5)
        v = v_ref[:, sl]
        q_bf = q.astype(BF16)
        k_bf = k.astype(BF16)
        bc = bcol[:, ML_HEADS + h:ML_HEADS + h + 1]
        li_c = gl[:, h:h + 1]
        m0_h = m0[:, h:h + 1]
        m_rows = jnp.concatenate([m0_h] * n_new, axis=0)
        n0_h = n0_ref[:, sl]
        nq = jnp.sum(q * jnp.concatenate([n0_h] * n_new, axis=0), axis=-1, keepdims=True)

        b_last = tblk(bc, n_new - 1)
        g_t = [b_last - tblk(bc, t) + tblk(li_c, t) for t in range(n_new)]
        m_new = b_last + m0_h
        for t in range(n_new):
            m_new = jnp.maximum(m_new, g_t[t])
        a_s_t = [jnp.exp(g_t[t] - m_new) for t in range(n_new)]
        a_c = jnp.exp(b_last + m0_h - m_new)
        n_new_h = a_c * n0_h
        for t in range(n_new):
            n_new_h = n_new_h + a_s_t[t] * tblk(k, t)
        n_out[:, sl] = n_new_h
        m_out[:, h:h + 1] = m_new
        ac_ref[...] = jnp.broadcast_to(a_c, (n_seq, LANES))
        av_t = (jnp.concatenate(a_s_t, axis=0) * v).T

        def per_seq(b, cq):
            c_prev = c0_ref[b, h]
            part = _mm_nt(q_bf, c_prev.astype(BF16))
            cq = jnp.where(row_seq == b, part, cq)
            upd = _mm(jnp.where(col_seq == b, av_t, 0.0).astype(BF16), k_bf)
            c_out[b, h] = ac_ref[pl.ds(b, 1), :] * c_prev + upd
            return cq

        cq = lax.fori_loop(0, n_seq, per_seq, jnp.zeros((L, ML_DH), F32))
        heads.append(dict(q=q_bf, k=k_bf, v=v.astype(BF16), o=o_ref[:, sl], hg=hg_ref[:, sl], bc=bc,
                          br=bT[ML_HEADS + h:ML_HEADS + h + 1, :], li_r=glT[h:h + 1, :],
                          m_rows=m_rows, cq=cq, nq=nq))
    h_ref[...] = jnp.concatenate(_mlstm_heads(heads, same_seq_causal), axis=-1).astype(h_ref.dtype)


def _mlstm_sample(qk, v, o, gates, st, cw, cb, gb, hg, c0, n0, m0, *, n_seq, n_new):
    L = n_seq * n_new
    return pl.pallas_call(
        functools.partial(_mlstm_sample_kernel, n_seq=n_seq, n_new=n_new),
        out_shape=[jax.ShapeDtypeStruct((L, ML_WIDTH), BF16),
                   jax.ShapeDtypeStruct(c0.shape, F32),
                   jax.ShapeDtypeStruct(n0.shape, F32),
                   jax.ShapeDtypeStruct(m0.shape, F32)],
        scratch_shapes=[pltpu.VMEM((n_seq, LANES), F32)],
        compiler_params=pltpu.CompilerParams(vmem_limit_bytes=VMEM_LIMIT), name="mlstm_sample",
    )(qk, v, o, gates, st, cw, cb, gb, hg, c0, n0, m0)


def _neg_upper(n):
    rows = lax.broadcasted_iota(jnp.int32, (n, n), 0)
    cols = lax.broadcasted_iota(jnp.int32, (n, n), 1)
    return jnp.where(rows > cols, -1.0, 0.0).astype(BF16)


def _sb_weights(zs, neg_upper, rests, mask, chained=False, lo_pass=True):
    log_betas, totals, terms = [], [], []
    for z in zs:
        z2 = z * LOG2E
        sp = jnp.maximum(z2, 0.0) + jnp.log2(1.0 + jnp.exp2(-jnp.abs(z2)))
        if mask is not None:
            sp = jnp.where(mask, sp, 0.0)
        log_betas.append(z2 - sp), totals.append(sp[:, 0:1])
        terms.append(jnp.concatenate(_split_hi_lo(sp), axis=1) if lo_pass else sp.astype(BF16))
    upper = jnp.concatenate([neg_upper, neg_upper], axis=0) if lo_pass else neg_upper
    css = [_mm(t, upper) for t in terms]
    weights, new_rests = [], []
    for i, (lb, cs, tot) in enumerate(zip(log_betas, css, totals)):
        rest = new_rests[-1] if (chained and i) else rests[i]
        a = jnp.exp2(lb + cs + rest)
        if mask is not None:
            a = jnp.where(mask, a, 0.0)
        weights.append(a.astype(BF16))
        new_rests.append(rest + cs[:, 0:1] - tot)
    return weights, (new_rests[-1:] if chained else new_rests)


def _sb_prompt_kernel(bias_ref, q_ref, kt_ref, vt_ref, o_ref, kb_ref, vb_ref, z_ref, a_ref, *, blk):
    hg = q_ref.shape[1]
    hi_ = pl.program_id(1)
    qi = pl.program_id(2)
    n_blk = kb_ref.shape[1]
    heads = range(hg)

    @pl.when(qi == 0)
    def _():
        for hh in heads:
            for j in range(n_blk):
                kb_ref[hh, j] = kt_ref[0, hh, :, j * blk:(j + 1) * blk].astype(BF16)
                vb_ref[hh, j] = vt_ref[0, hh, :, j * blk:(j + 1) * blk].astype(BF16)

    neg_upper = _neg_upper(blk)
    strict_causal = (lax.broadcasted_iota(jnp.int32, (blk, blk), 1)
                     < lax.broadcasted_iota(jnp.int32, (blk, blk), 0))

    def logits(kb):
        return [_mm(q_ref[0, hh], kb_ref[hh, kb]) + bias_ref[hi_ * hg + hh] for hh in heads]

    def weighted_values(kb):
        return [_mm_nt(a_ref[hh], vb_ref[hh, kb]) for hh in heads]

    def visit(kb, mask, carry, first):
        accs, rests = carry
        zs = [z_ref[hh] for hh in heads]
        z_next = logits(jnp.maximum(kb - 1, 0))
        if not first:
            accs = tuple(a + p for a, p in zip(accs, weighted_values(kb + 1)))
        for hh in heads:
            z_ref[hh] = z_next[hh]
        weights, rests = _sb_weights(zs, neg_upper, rests, mask, lo_pass=False)
        for hh in heads:
            a_ref[hh] = weights[hh]
        return accs, tuple(rests)

    for hh, z in enumerate(logits(qi)):
        z_ref[hh] = z
    carry = ((jnp.zeros((blk, SB_DH), F32),) * hg, (jnp.zeros((blk, 1), F32),) * hg)
    carry = visit(qi, strict_causal, carry, True)
    accs, _ = lax.fori_loop(0, qi, lambda i, c: visit(qi - 1 - i, None, c, False), carry)
    accs = [a + p for a, p in zip(accs, weighted_values(0))]
    o_ref[...] = jnp.concatenate(accs, axis=-1).astype(o_ref.dtype)


def _sb_prompt(bias, qh, kt, vt, *, blk, heads_per_step):
    n_seq, n_heads, seq_len, dh = qh.shape
    nq = seq_len // blk
    hg = heads_per_step
    kv_spec = pl.BlockSpec((1, hg, dh, seq_len), lambda b, h, qi: (b, h, 0, 0))
    return pl.pallas_call(
        functools.partial(_sb_prompt_kernel, blk=blk),
        grid=(n_seq, n_heads // hg, nq),
        in_specs=[pl.BlockSpec(memory_space=pltpu.SMEM),
                  pl.BlockSpec((1, hg, blk, dh), lambda b, h, qi: (b, h, qi, 0)),
                  kv_spec, kv_spec],
        out_specs=pl.BlockSpec((blk, hg * dh), lambda b, h, qi: (b * nq + qi, h)),
        out_shape=jax.ShapeDtypeStruct((n_seq * seq_len, n_heads * dh), BF16),
        scratch_shapes=[pltpu.VMEM((hg, nq, dh, blk), BF16), pltpu.VMEM((hg, nq, dh, blk), BF16),
                        pltpu.VMEM((hg, blk, blk), F32), pltpu.VMEM((hg, blk, blk), BF16)],
        compiler_params=_params(3), name="sb_prompt",
    )(bias, qh, kt, vt)


def _sb_sample_kernel(pt_ref, q_ref, bias_ref, kn_ref, vn_ref, *refs, pages_per_step, n_new):
    del pt_ref
    k_refs = refs[:pages_per_step]
    v_refs = refs[pages_per_step:2 * pages_per_step]
    o_ref, acc_ref, rest_ref = refs[2 * pages_per_step:]
    j = pl.program_id(1)
    n_rows = n_new * SB_HEADS
    page = kn_ref.shape[1]
    head_of_col = lax.broadcasted_iota(jnp.int32, (n_rows, SB_WIDTH), 1) // SB_DH
    head_of_row = lax.broadcasted_iota(jnp.int32, (n_rows, SB_WIDTH), 0) % SB_HEADS
    own_head = head_of_col == head_of_row
    q = jnp.where(own_head, q_ref[0], 0.0).astype(BF16)
    bias = bias_ref[...]

    def visit(kts, vts, mask):
        zs = [_mm(q, kt) + bias for kt in kts]
        weights, rests = _sb_weights(zs, _neg_upper(kts[0].shape[1]), [rest_ref[:, 0:1]], mask, chained=True)
        acc_ref[...] += functools.reduce(lambda a, b: a + b, [_mm_nt(a, vt) for a, vt in zip(weights, vts)])
        rest_ref[...] = jnp.broadcast_to(rests[0], rest_ref.shape)

    @pl.when(j == 0)
    def _():
        acc_ref[...] = jnp.zeros_like(acc_ref)
        rest_ref[...] = jnp.zeros_like(rest_ref)
        t_of_row = lax.broadcasted_iota(jnp.int32, (n_rows, page), 0) // SB_HEADS
        s_of_col = lax.broadcasted_iota(jnp.int32, (n_rows, page), 1)
        visit([kn_ref[0].T.astype(BF16)], [vn_ref[0].T.astype(BF16)], s_of_col < t_of_row)

    def page_pair(p_refs, i):
        lo_hi = [p_refs[i + 1][0, 0].reshape(SB_WIDTH, page), p_refs[i][0, 0].reshape(SB_WIDTH, page)]
        return jnp.concatenate(lo_hi, axis=1).astype(BF16)

    pairs = range(0, pages_per_step, 2)
    visit([page_pair(k_refs, i) for i in pairs], [page_pair(v_refs, i) for i in pairs], None)

    @pl.when(j == pl.num_programs(1) - 1)
    def _():
        acc = jnp.where(own_head, acc_ref[...], 0.0)
        for t in range(n_new):
            o_ref[0, t:t + 1, :] = jnp.sum(acc[t * SB_HEADS:(t + 1) * SB_HEADS, :], axis=0, keepdims=True)


def _sb_sample(page_table, q_rows, bias_rows, k_new, v_new, cache_kt, cache_vt, *, layer, pages_per_step, n_new):
    n_seq, n_pages = page_table.shape
    page = cache_kt.shape[4]
    n_rows = n_new * SB_HEADS
    steps = n_pages // pages_per_step

    def page_spec(i):
        return pl.BlockSpec((1, 1, SB_HEADS, SB_DH, page),
                            lambda b, j, pt: (layer, pt[b, n_pages - 1 - (j * pages_per_step + i)], 0, 0, 0))

    per_seq = lambda r, c: pl.BlockSpec((1, r, c), lambda b, j, pt: (b, 0, 0))
    grid_spec = pltpu.PrefetchScalarGridSpec(
        num_scalar_prefetch=1, grid=(n_seq, steps),
        in_specs=[per_seq(n_rows, SB_WIDTH),
                  pl.BlockSpec((n_rows, 1), lambda b, j, pt: (0, 0)),
                  per_seq(page, SB_WIDTH), per_seq(page, SB_WIDTH)]
                 + [page_spec(i) for i in range(pages_per_step)] * 2,
        out_specs=per_seq(n_new, SB_WIDTH),
        scratch_shapes=[pltpu.VMEM((n_rows, SB_WIDTH), F32), pltpu.VMEM((n_rows, LANES), F32)])
    return pl.pallas_call(
        functools.partial(_sb_sample_kernel, pages_per_step=pages_per_step, n_new=n_new),
        grid_spec=grid_spec,
        out_shape=jax.ShapeDtypeStruct((n_seq, n_new, SB_WIDTH), F32),
        compiler_params=_params(2), name="sb_sample",
    )(page_table, q_rows, bias_rows, k_new, v_new,
      *([cache_kt] * pages_per_step), *([cache_vt] * pages_per_step))


def _proj_res_kernel(x_ref, a_ref, b_ref, w_ref, o_ref):
    ka = a_ref.shape[1]
    o_ref[...] = (x_ref[...] + _mm(a_ref[...].astype(BF16), w_ref[0:ka, :])
                  + _mm(b_ref[...].astype(BF16), w_ref[ka:, :]))


def _proj_res(x, a, b, w, *, tm):
    m, d = x.shape
    row = lambda width: pl.BlockSpec((tm, width), lambda i: (i, 0))
    return pl.pallas_call(
        _proj_res_kernel, grid=(m // tm,),
        in_specs=[row(d), row(a.shape[1]), row(b.shape[1]), _const_spec(w.shape)],
        out_specs=row(d), out_shape=jax.ShapeDtypeStruct((m, d), F32),
        compiler_params=_params(1), name="proj_res",
    )(x, a, b, w)


def _ffn_chunks(xn, win_ref, wout_ref, cw_ref, cb_ref, o_ref, conv_fn, d_ff):
    bounds = [(c0, min(c0 + FFN_CHUNK, d_ff)) for c0 in range(0, d_ff, FFN_CHUNK)]

    def up(c):
        lo, hi = bounds[c]
        return _mm(xn, win_ref[:, lo:hi]), _mm(xn, win_ref[:, d_ff + lo:d_ff + hi])

    nxt = up(0)
    for c in range(len(bounds)):
        sl = slice(*bounds[c])
        g, u = nxt
        if c + 1 < len(bounds):
            nxt = up(c + 1)
        g_m2, g_m1 = conv_fn(g, sl)
        y = cb_ref[:, sl] + g_m2 * cw_ref[0:1, sl] + g_m1 * cw_ref[1:2, sl] + g * cw_ref[2:3, sl]
        hmid = (_gelu(y) * u).astype(BF16)
        o_ref[...] += _mm(hmid, wout_ref[sl, :])


def _ffn_prompt_kernel(x_ref, g_ref, win_ref, cw_ref, cb_ref, wout_ref, fg_ref, *rest,
                       tiles_per_seq, d_ff, final_norm, mixer_proj):
    if mixer_proj:
        a_ref, b_ref, wmix_ref, o_ref, tail_out, xn_ref, prev_ref, st_ref = rest
    else:
        o_ref, tail_out, xn_ref, prev_ref, st_ref = rest
    i = pl.program_id(0)
    tm = x_ref.shape[0]
    halo = SUBLANES

    @pl.when(i % tiles_per_seq == 0)
    def _():
        prev_ref[...] = jnp.zeros_like(prev_ref)

    x = x_ref[...]
    if mixer_proj:
        ka = a_ref.shape[1]
        x = x + _mm(a_ref[...], wmix_ref[0:ka, :]) + _mm(b_ref[...], wmix_ref[ka:, :])
    xn_ref[...] = _rms(x, g_ref[...]).astype(BF16)
    o_ref[...] = x

    def conv_fn(g, sl):
        w = sl.stop - sl.start
        st_ref[0:halo, 0:w] = prev_ref[:, sl]
        st_ref[halo:halo + tm, 0:w] = g
        prev_ref[:, sl] = g[tm - halo:tm, :]
        return st_ref[halo - 2:halo - 2 + tm, 0:w], st_ref[halo - 1:halo - 1 + tm, 0:w]

    _ffn_chunks(xn_ref[...], win_ref, wout_ref, cw_ref, cb_ref, o_ref, conv_fn, d_ff)
    if final_norm:
        o_ref[...] = _rms(o_ref[...], fg_ref[...])

    @pl.when(i % tiles_per_seq == tiles_per_seq - 1)
    def _():
        tail_out[0] = prev_ref[...]


def _ffn_prompt(x, g, win, cw, cb, wout, fg, mixer=None, *, n_seq, seq_len, tm, final_norm):
    m, d = x.shape
    d_ff = wout.shape[0]
    tps = seq_len // tm
    row = lambda width: pl.BlockSpec((tm, width), lambda i: (i, 0))
    operands = [x, g, win, cw, cb, wout, fg]
    in_specs = [row(d)] + [_const_spec(a.shape) for a in operands[1:]]
    if mixer is not None:
        a, b, w_mix = mixer
        operands += [a, b, w_mix]
        in_specs += [row(a.shape[1]), row(b.shape[1]), _const_spec(w_mix.shape)]
    return pl.pallas_call(
        functools.partial(_ffn_prompt_kernel, tiles_per_seq=tps, d_ff=d_ff, final_norm=final_norm,
                          mixer_proj=mixer is not None),
        grid=(m // tm,),
        in_specs=in_specs,
        out_specs=[row(d), pl.BlockSpec((1, SUBLANES, d_ff), lambda i: (i // tps, 0, 0))],
        out_shape=[jax.ShapeDtypeStruct((m, d), F32), jax.ShapeDtypeStruct((n_seq, SUBLANES, d_ff), F32)],
        scratch_shapes=[pltpu.VMEM((tm, d), BF16), pltpu.VMEM((SUBLANES, d_ff), F32),
                        pltpu.VMEM((SUBLANES + tm, FFN_CHUNK), F32)],
        compiler_params=_params(1), name="ffn_prompt",
    )(*operands)


def _ffn_sample_kernel(x_ref, g_ref, win_ref, cw_ref, cb_ref, wout_ref, fg_ref, st_ref, o_ref, tail_out,
                       *, n_seq, d_ff, final_norm):
    rows = x_ref.shape[0]
    x = x_ref[...]
    xn = _rms(x, g_ref[...]).astype(BF16)
    o_ref[...] = x

    def conv_fn(g, sl):
        ext = jnp.concatenate([st_ref[:, sl], g], axis=0)
        tail_out[:, sl] = g[rows - 2 * n_seq:rows, :]
        return ext[0:rows, :], ext[n_seq:n_seq + rows, :]

    _ffn_chunks(xn, win_ref, wout_ref, cw_ref, cb_ref, o_ref, conv_fn, d_ff)
    if final_norm:
        o_ref[...] = _rms(o_ref[...], fg_ref[...])


def _ffn_sample(x, g, win, cw, cb, wout, fg, st, *, n_seq, final_norm):
    d_ff = wout.shape[0]
    return pl.pallas_call(
        functools.partial(_ffn_sample_kernel, n_seq=n_seq, d_ff=d_ff, final_norm=final_norm),
        out_shape=[jax.ShapeDtypeStruct(x.shape, F32), jax.ShapeDtypeStruct(st.shape, F32)],
        compiler_params=pltpu.CompilerParams(vmem_limit_bytes=VMEM_LIMIT), name="ffn_sample",
    )(x, g, win, cw, cb, wout, fg, st)


def _layernorm_stats(v_ref, width, n_groups):
    gw = width // n_groups
    s1 = 0.0
    for gi in range(n_groups):
        s1 = s1 + jnp.sum(v_ref[:, gi * gw:(gi + 1) * gw], axis=-1, keepdims=True)
    mean = s1 / width
    s2 = 0.0
    for gi in range(n_groups):
        xc = v_ref[:, gi * gw:(gi + 1) * gw] - mean
        s2 = s2 + jnp.sum(xc * xc, axis=-1, keepdims=True)
    return mean, lax.rsqrt(s2 / width + EPS)


def _cm_prompt_kernel(x_ref, g_ref, win_ref, lng_ref, lnb_ref, ws_ref, bst_ref, wout_ref, o_ref,
                      xn_ref, v_ref, *, width):
    tm = x_ref.shape[0]
    gw = width // CM_GROUPS
    x = x_ref[...]
    xn_ref[...] = _rms(x, g_ref[...]).astype(BF16)
    o_ref[...] = x
    proj = lambda col0, gi: _mm(xn_ref[...], win_ref[:, col0 + gi * gw:col0 + (gi + 1) * gw])
    nxt = proj(width, 0)
    for gi in range(CM_GROUPS):
        cur, nxt = nxt, (proj(width, gi + 1) if gi + 1 < CM_GROUPS else proj(0, 0))
        v_ref[:, gi * gw:(gi + 1) * gw] = _gelu(cur)
    mean, rstd = _layernorm_stats(v_ref, width, CM_GROUPS)
    rows = lax.broadcasted_iota(jnp.int32, (CM_CHUNK, CM_CHUNK), 0)
    cols = lax.broadcasted_iota(jnp.int32, (CM_CHUNK, CM_CHUNK), 1)
    tril = rows >= cols
    gated = []
    for gi in range(CM_GROUPS):
        sl = slice(gi * gw, (gi + 1) * gw)
        u_pre, nxt = nxt, (proj(0, gi + 1) if gi + 1 < CM_GROUPS else None)
        vn = ((v_ref[:, sl] - mean) * rstd * lng_ref[:, sl] + lnb_ref[:, sl]).astype(BF16)
        wsg = jnp.where(tril, ws_ref[gi], 0.0).astype(BF16)
        bias = bst_ref[:, gi:gi + 1]
        mixed = jnp.concatenate(
            [_mm(wsg, vn[c * CM_CHUNK:(c + 1) * CM_CHUNK, :]) + bias for c in range(tm // CM_CHUNK)], axis=0)
        gated.append((_gelu(u_pre) * mixed).astype(BF16))
        if len(gated) == CM_OUT_GROUPS:
            lo = (gi + 1 - CM_OUT_GROUPS) * gw
            o_ref[...] += _mm(jnp.concatenate(gated, axis=1), wout_ref[lo:(gi + 1) * gw, :])
            gated = []


def _cm_prompt(x, g, win, lng, lnb, ws, bst, wout, *, tm):
    m, d = x.shape
    width = wout.shape[0]
    row = pl.BlockSpec((tm, d), lambda i: (i, 0))
    return pl.pallas_call(
        functools.partial(_cm_prompt_kernel, width=width),
        grid=(m // tm,),
        in_specs=[row] + [_const_spec(a.shape) for a in (g, win, lng, lnb, ws, bst, wout)],
        out_specs=row, out_shape=jax.ShapeDtypeStruct((m, d), F32),
        scratch_shapes=[pltpu.VMEM((tm, d), BF16), pltpu.VMEM((tm, width), F32)],
        compiler_params=_params(1), name="cm_prompt",
    )(x, g, win, lng, lnb, ws, bst, wout)


def _cm_sample_kernel(x_ref, g_ref, win_ref, lng_ref, lnb_ref, wexp_ref, bexp_ref, wout_ref, o_ref, v_out,
                      *, n_seq, n_new, width):
    gw = width // CM_GROUPS
    x = x_ref[...]
    xn = _rms(x, g_ref[...]).astype(BF16)
    for gi in range(CM_GROUPS):
        v_out[:, gi * gw:(gi + 1) * gw] = _gelu(_mm(xn, win_ref[:, width + gi * gw:width + (gi + 1) * gw]))
    mean, rstd = _layernorm_stats(v_out, width, CM_GROUPS)
    acc = x
    for gi in range(CM_GROUPS):
        sl = slice(gi * gw, (gi + 1) * gw)
        vn = (v_out[:, sl] - mean) * rstd * lng_ref[:, sl] + lnb_ref[:, sl]
        v_out[:, sl] = vn
        mixed = []
        for t in range(n_new):
            mt = bexp_ref[t:t + 1, sl]
            for s in range(t + 1):
                mt = mt + wexp_ref[t * n_new + s:t * n_new + s + 1, sl] * vn[s * n_seq:(s + 1) * n_seq, :]
            mixed.append(mt)
        u = _gelu(_mm(xn, win_ref[:, sl]))
        acc = acc + _mm((u * jnp.concatenate(mixed, axis=0)).astype(BF16), wout_ref[sl, :])
    o_ref[...] = acc


def _cm_sample(x, g, win, lng, lnb, wexp, bexp, wout, *, n_seq, n_new):
    width = wout.shape[0]
    return pl.pallas_call(
        functools.partial(_cm_sample_kernel, n_seq=n_seq, n_new=n_new, width=width),
        out_shape=[jax.ShapeDtypeStruct(x.shape, F32), jax.ShapeDtypeStruct((x.shape[0], width), F32)],
        compiler_params=pltpu.CompilerParams(vmem_limit_bytes=VMEM_LIMIT), name="cm_sample",
    )(x, g, win, lng, lnb, wexp, bexp, wout)


def _time_major(a):
    return jnp.swapaxes(a, 0, 1).reshape((a.shape[0] * a.shape[1],) + a.shape[2:])


def _batch_major(a, n_seq):
    return jnp.swapaxes(a.reshape((a.shape[0] // n_seq, n_seq) + a.shape[1:]), 0, 1)


def _pad_lanes(a):
    return jnp.pad(a, [(0, 0)] * (a.ndim - 1) + [(0, LANES - a.shape[-1])])


def kernel(x_prompt, x_sample, cache_k, cache_v, page_table, state_mlstm_c, state_mlstm_n, state_mlstm_m, state_mlstm_conv, state_ffn_conv, norm_mix_g, norm_ffn_g, norm_final_g, w_in_even, ml_conv_w, ml_conv_b, ml_gate_b, ml_head_g, sb_logit_b, w_out_even, w_in_odd, cm_ln_g, cm_ln_b, cm_spatial_w, cm_spatial_b, w_out_odd, ffn_w_in, ffn_conv_w, ffn_conv_b, ffn_w_out):
    n_pb, seq_p, d = x_prompt.shape
    n_sb, seq_s, _ = x_sample.shape
    depth = norm_mix_g.shape[0]
    d_ff = ffn_w_out.shape[1]
    hp = x_prompt.reshape(n_pb * seq_p, d)
    hs = _time_major(x_sample)
    fg = norm_final_g.reshape(1, d)

    kp_l, vp_l, ks_l, vs_l = [], [], [], []
    cp_l, np_l, mp_l, bp_l = [], [], [], []
    cs_l, ns_l, ms_l, bs_l = [], [], [], []
    cmv_l, fp_l, fs_l = [], [], []
    for layer in range(depth):
        mix_g = norm_mix_g[layer].reshape(1, d)
        if layer % 2 == 0:
            e = layer // 2
            w = w_in_even[e]
            gate_lo = _C_O + ML_WIDTH
            gate_hi = gate_lo + 2 * ML_HEADS
            w_all = jnp.concatenate([w[:, :gate_lo], w[:, gate_hi:], _pad_lanes(w[:, gate_lo:gate_hi])],
                                    axis=1).astype(BF16)
            w_kvt = w[:, gate_hi + SB_WIDTH:].T.astype(BF16)
            cw, cb = ml_conv_w[e], ml_conv_b[e].reshape(1, -1)
            gb = _pad_lanes(ml_gate_b[e].reshape(1, -1))
            hg = ml_head_g[e].reshape(1, -1)
            w_out = w_out_even[e].astype(BF16)
            qk, vm, om, gates, qh, kt, vt = _even_in(
                hp, mix_g, w_all, w_kvt, n_seq=n_pb, seq_len=seq_p, tm=EVEN_IN_ROWS, head_major=True)
            h_ml, c_p, n_p, m_p, tail_p = _mlstm_prompt(qk, vm, om, gates, cw, cb, gb, hg, n_seq=n_pb, seq_len=seq_p)
            h_sb = _sb_prompt(sb_logit_b[e], qh, kt, vt, blk=SB_BLOCK, heads_per_step=SB_HEADS_PER_STEP)
            prompt_mixer = (h_ml, h_sb, w_out)
            kp_l.append(jnp.transpose(kt, (0, 3, 1, 2)))
            vp_l.append(jnp.transpose(vt, (0, 3, 1, 2)))
            cp_l.append(c_p)
            np_l.append(n_p[:, :ML_HEADS])
            mp_l.append(m_p[:, :ML_HEADS, 0])
            bp_l.append(tail_p[:, SUBLANES - (ML_CONV - 1):])
            qk, vm, om, gates, sq, sk, sv = _even_in(
                hs, mix_g, w_all, w_kvt, n_seq=n_sb, seq_len=seq_s, tm=n_sb * seq_s, head_major=False)
            h_ml, c_s, n_s, m_s = _mlstm_sample(
                qk, vm, om, gates, _time_major(state_mlstm_conv[e]), cw, cb, gb, hg,
                state_mlstm_c[e], state_mlstm_n[e].reshape(n_sb, ML_WIDTH), _pad_lanes(state_mlstm_m[e]),
                n_seq=n_sb, n_new=seq_s)
            page = cache_k.shape[2]
            pad_keys = lambda a: jnp.pad(_batch_major(a, n_sb), ((0, 0), (0, page - seq_s), (0, 0)))
            q_rows = jnp.broadcast_to(_batch_major(sq, n_sb)[:, :, None, :],
                                      (n_sb, seq_s, SB_HEADS, SB_WIDTH)).reshape(n_sb, seq_s * SB_HEADS, SB_WIDTH)
            bias_rows = jnp.tile(sb_logit_b[e], seq_s).reshape(seq_s * SB_HEADS, 1)
            pool_view = lambda c: jnp.transpose(c, (0, 1, 3, 4, 2))
            h_sb = _sb_sample(page_table, q_rows, bias_rows, pad_keys(sk), pad_keys(sv),
                              pool_view(cache_k), pool_view(cache_v), layer=e,
                              pages_per_step=SB_PAGES_PER_STEP, n_new=seq_s)
            hs = _proj_res(hs, h_ml, _time_major(h_sb), w_out, tm=n_sb * seq_s)
            ks_l.append(_batch_major(sk, n_sb).reshape(n_sb, seq_s, SB_HEADS, SB_DH))
            vs_l.append(_batch_major(sv, n_sb).reshape(n_sb, seq_s, SB_HEADS, SB_DH))
            cs_l.append(c_s)
            ns_l.append(n_s.reshape(n_sb, ML_HEADS, ML_DH))
            ms_l.append(m_s[:, :ML_HEADS])
            bs_l.append(_batch_major(qk[(seq_s - (ML_CONV - 1)) * n_sb:], n_sb))
        else:
            o = layer // 2
            win = w_in_odd[o].astype(BF16)
            wout = w_out_odd[o].astype(BF16)
            lng, lnb = cm_ln_g[o].reshape(1, -1), cm_ln_b[o].reshape(1, -1)
            width = wout.shape[0]
            gw = width // CM_GROUPS
            prompt_mixer = None
            hp = _cm_prompt(hp, mix_g, win, lng, lnb, cm_spatial_w[o], cm_spatial_b[o].T, wout, tm=CM_ROWS)
            wexp = jnp.repeat(cm_spatial_w[o][:, :seq_s, :seq_s].reshape(CM_GROUPS, seq_s * seq_s).T, gw, axis=1)
            bexp = jnp.repeat(cm_spatial_b[o][:, :seq_s].T, gw, axis=1)
            hs, v_rows = _cm_sample(hs, mix_g, win, lng, lnb, wexp, bexp, wout, n_seq=n_sb, n_new=seq_s)
            cmv_l.append(_batch_major(v_rows, n_sb))
        ffn_g = norm_ffn_g[layer].reshape(1, d)
        win = ffn_w_in[layer].astype(BF16)
        wout = ffn_w_out[layer].astype(BF16)
        cw, cb = ffn_conv_w[layer], ffn_conv_b[layer].reshape(1, -1)
        last = layer == depth - 1
        hp, tail_p = _ffn_prompt(hp, ffn_g, win, cw, cb, wout, fg, prompt_mixer, n_seq=n_pb, seq_len=seq_p,
                                 tm=FFN_ROWS, final_norm=last)
        hs, tail_s = _ffn_sample(hs, ffn_g, win, cw, cb, wout, fg, _time_major(state_ffn_conv[layer]),
                                 n_seq=n_sb, final_norm=last)
        fp_l.append(tail_p[:, SUBLANES - (FFN_CONV - 1):])
        fs_l.append(_batch_major(tail_s, n_sb))

    return (hp.reshape(n_pb, seq_p, d), _batch_major(hs, n_sb),
            jnp.stack(kp_l), jnp.stack(vp_l), jnp.stack(ks_l), jnp.stack(vs_l),
            jnp.stack(cp_l), jnp.stack(np_l), jnp.stack(mp_l), jnp.stack(bp_l),
            jnp.stack(cs_l), jnp.stack(ns_l), jnp.stack(ms_l), jnp.stack(bs_l),
            jnp.stack(cmv_l), jnp.stack(fp_l), jnp.stack(fs_l))
```

```python
import functools

import jax
import jax.numpy as jnp
from jax import lax
from jax.experimental import pallas as pl
from jax.experimental.pallas import tpu as pltpu

F32 = jnp.float32
BF16 = jnp.bfloat16
EPS = 1e-6
LOG2E = 1.4426950408889634

ML_HEADS = 4
ML_DH = 128
ML_WIDTH = ML_HEADS * ML_DH
ML_CONV = 4
ML_CHUNK = 128
SB_HEADS = 8
SB_DH = 64
SB_WIDTH = SB_HEADS * SB_DH
SB_SCALE = SB_DH ** -0.5
CM_GROUPS = 8
CM_CHUNK = 128
CM_OUT_GROUPS = 2
FFN_CONV = 3
LANES = 128
SUBLANES = 8
MXU_N = 256
FFN_CHUNK = 2 * MXU_N
VMEM_LIMIT = 56 * 1024 * 1024

EVEN_IN_ROWS = 512
CM_ROWS = 512
FFN_ROWS = 1024
SB_BLOCK = 256
SB_HEADS_PER_STEP = 8
SB_PAGES_PER_STEP = 32


def _mm(a, b):
    return jnp.dot(a, b, preferred_element_type=F32)


def _mm_nt(a, b):
    return lax.dot_general(a, b, (((1,), (1,)), ((), ())), preferred_element_type=F32)


def _rms(x, g):
    return x * lax.rsqrt(jnp.mean(x * x, axis=-1, keepdims=True) + EPS) * g


def _sigmoid(x):
    return 1.0 / (1.0 + jnp.exp(-x))


def _gelu(x):
    c = 0.7978845608028654
    return 0.5 * x * (1.0 + jnp.tanh(c * (x + 0.044715 * (x * x * x))))


def _split_hi_lo(x):
    hi = x.astype(BF16)
    lo = (x - hi.astype(F32)).astype(BF16)
    return hi, lo


def _params(n_axes):
    return pltpu.CompilerParams(dimension_semantics=("arbitrary",) * n_axes,
                                vmem_limit_bytes=VMEM_LIMIT)


def _const_spec(shape):
    nd = len(shape)
    return pl.BlockSpec(shape, lambda *_: (0,) * nd, pipeline_mode=pl.Buffered(1))


_C_QK = 0
_C_V = 2 * ML_WIDTH
_C_O = _C_V + ML_WIDTH
_C_SQ = _C_O + ML_WIDTH
_C_SK = _C_SQ + SB_WIDTH
_C_SV = _C_SK + SB_WIDTH
_C_G = _C_SV + SB_WIDTH
_C_END = _C_G + LANES


def _even_in_kernel(x_ref, g_ref, w_ref, wkvt_ref, *out_refs, head_major):
    xn = _rms(x_ref[...], g_ref[...]).astype(BF16)
    qk_ref, v_ref, o_ref, gates_ref, sq_ref, sk_ref, sv_ref = out_refs
    qk_ref[...] = _mm(xn, w_ref[:, _C_QK:_C_V])
    v_ref[...] = _mm(xn, w_ref[:, _C_V:_C_O])
    o_ref[...] = _mm(xn, w_ref[:, _C_O:_C_SQ])
    gates_ref[...] = _mm(xn, w_ref[:, _C_G:_C_END])
    sq = _mm(xn, w_ref[:, _C_SQ:_C_SK]) * SB_SCALE
    if head_major:
        tm = sq.shape[0]
        for h in range(SB_HEADS):
            sq_ref[0, h] = sq[:, h * SB_DH:(h + 1) * SB_DH].astype(BF16)
        sk_ref[0] = _mm_nt(wkvt_ref[0:SB_WIDTH, :], xn).reshape(SB_HEADS, SB_DH, tm)
        sv_ref[0] = _mm_nt(wkvt_ref[SB_WIDTH:, :], xn).reshape(SB_HEADS, SB_DH, tm)
    else:
        sq_ref[...] = sq
        sk_ref[...] = _mm(xn, w_ref[:, _C_SK:_C_SV])
        sv_ref[...] = _mm(xn, w_ref[:, _C_SV:_C_G])


def _even_in(x, g, w, wkvt, *, n_seq, seq_len, tm, head_major):
    m, d = x.shape
    nt = m // tm
    row = lambda width: pl.BlockSpec((tm, width), lambda i: (i, 0))
    out_shape = [jax.ShapeDtypeStruct((m, 2 * ML_WIDTH), F32),
                 jax.ShapeDtypeStruct((m, ML_WIDTH), F32),
                 jax.ShapeDtypeStruct((m, ML_WIDTH), F32),
                 jax.ShapeDtypeStruct((m, LANES), F32)]
    out_specs = [row(2 * ML_WIDTH), row(ML_WIDTH), row(ML_WIDTH), row(LANES)]
    if head_major:
        tps = seq_len // tm
        out_shape += [jax.ShapeDtypeStruct((n_seq, SB_HEADS, seq_len, SB_DH), BF16)]
        out_specs += [pl.BlockSpec((1, SB_HEADS, tm, SB_DH), lambda i: (i // tps, 0, i % tps, 0))]
        out_shape += [jax.ShapeDtypeStruct((n_seq, SB_HEADS, SB_DH, seq_len), F32)] * 2
        out_specs += [pl.BlockSpec((1, SB_HEADS, SB_DH, tm), lambda i: (i // tps, 0, 0, i % tps))] * 2
    else:
        out_shape += [jax.ShapeDtypeStruct((m, SB_WIDTH), F32)] * 3
        out_specs += [row(SB_WIDTH)] * 3
    return pl.pallas_call(
        functools.partial(_even_in_kernel, head_major=head_major),
        grid=(nt,),
        in_specs=[row(d), _const_spec((1, d)), _const_spec(w.shape), _const_spec(wkvt.shape)],
        out_specs=out_specs, out_shape=out_shape,
        compiler_params=_params(1), name="even_in",
    )(x, g, w, wkvt)


def _log_sigmoid(x):
    return jnp.minimum(x, 0.0) - jnp.log1p(jnp.exp(-jnp.abs(x)))


def _mlstm_heads(heads, mask):
    qk = [_mm_nt(h["q"], h["k"]) for h in heads]
    ss, w_inters, floors = [], [], []
    for h, qk_h in zip(heads, qk):
        dmat = jnp.where(mask, h["bc"] - h["br"] + h["li_r"], -jnp.inf)
        inter = h["bc"] + h["m_rows"]
        m_t = jnp.maximum(inter, jnp.max(dmat, axis=-1, keepdims=True))
        ss.append(qk_h * jnp.exp(dmat - m_t))
        w_inters.append(jnp.exp(inter - m_t))
        floors.append(jnp.exp(-m_t))
    svs = [_mm(s.astype(BF16), h["v"]) for s, h in zip(ss, heads)]
    outs = []
    for h, s, sv, w_inter, floor in zip(heads, ss, svs, w_inters, floors):
        num = w_inter * h["cq"] + sv
        den = w_inter * h["nq"] + jnp.sum(s, axis=-1, keepdims=True)
        ho = _sigmoid(h["o"]) * (num / jnp.maximum(jnp.abs(den), floor))
        outs.append(ho * lax.rsqrt(jnp.mean(ho * ho, axis=-1, keepdims=True) + EPS) * h["hg"])
    return outs


def _mlstm_prompt_kernel(qk_ref, v_ref, o_ref, gates_ref, cw_ref, cb_ref, gb_ref, hg_ref,
                         h_ref, c_out, n_out, m_out, tail_out,
                         xp_ref, c_ref, n_ref, m_ref, *, n_chunks):
    ci = pl.program_id(1)
    L = ML_CHUNK
    halo = SUBLANES

    @pl.when(ci == 0)
    def _():
        xp_ref[0:halo, :] = jnp.zeros((halo, 2 * ML_WIDTH), F32)
        c_ref[...] = jnp.zeros_like(c_ref)
        n_ref[...] = jnp.zeros_like(n_ref)
        m_ref[...] = jnp.zeros_like(m_ref)

    xp_ref[halo:halo + L, :] = qk_ref[...]
    y = cb_ref[...]
    for i in range(ML_CONV):
        off = halo - (ML_CONV - 1) + i
        y = y + xp_ref[off:off + L, :] * cw_ref[i:i + 1, :]
    xp_ref[0:halo, :] = qk_ref[L - halo:L, :]
    qk = y * _sigmoid(y)

    gl = gates_ref[...] + gb_ref[...]
    logf = _log_sigmoid(gl)
    rows = lax.broadcasted_iota(jnp.int32, (L, L), 0)
    cols = lax.broadcasted_iota(jnp.int32, (L, L), 1)
    causal = rows >= cols
    tri = jnp.where(causal, 1.0, 0.0).astype(BF16)
    bcol = _cumsum_rows(tri, logf)
    bT = bcol.T
    glT = gl.T

    outs, pending = [], []
    for h in range(ML_HEADS):
        sl = slice(h * ML_DH, (h + 1) * ML_DH)
        q = qk[:, sl]
        k = qk[:, ML_WIDTH + h * ML_DH:ML_WIDTH + (h + 1) * ML_DH] * (ML_DH ** -0.5)
        v = v_ref[:, sl]
        bc = bcol[:, ML_HEADS + h:ML_HEADS + h + 1]
        m_prev = m_ref[h:h + 1, 0:1]
        c_prev = c_ref[h]
        n_prev = n_ref[h:h + 1, :]
        q_bf, k_bf = q.astype(BF16), k.astype(BF16)
        head = dict(
            q=q_bf, k=k_bf, v=v.astype(BF16), o=o_ref[:, sl], hg=hg_ref[:, sl], bc=bc,
            br=bT[ML_HEADS + h:ML_HEADS + h + 1, :], li_r=glT[h:h + 1, :],
            m_rows=jnp.broadcast_to(m_prev, (L, 1)),
            cq=_mm_nt(q_bf, c_prev.astype(BF16)), nq=jnp.sum(q * n_prev, axis=-1, keepdims=True))
        pending.append(head)
        if len(pending) == 2:
            outs += _mlstm_heads(pending, causal)
            pending = []
        b_last = bc[L - 1:L, :]
        g = b_last - bc + gl[:, h:h + 1]
        m_new = jnp.maximum(b_last + m_prev, jnp.max(g, axis=0, keepdims=True))
        a_s = jnp.exp(g - m_new)
        a_c = jnp.exp(b_last + m_prev - m_new)
        c_ref[h] = a_c * c_prev + _mm((a_s * v).T.astype(BF16), k_bf)
        n_ref[h:h + 1, :] = a_c * n_prev + jnp.sum(a_s * k, axis=0, keepdims=True)
        m_ref[h:h + 1, :] = jnp.broadcast_to(m_new, (1, LANES))
    h_ref[...] = jnp.concatenate(outs, axis=-1).astype(h_ref.dtype)

    @pl.when(ci == n_chunks - 1)
    def _():
        c_out[0] = c_ref[...]
        n_out[0] = n_ref[...]
        m_out[0] = m_ref[...]
        tail_out[0] = qk_ref[L - halo:L, :]


def _cumsum_rows(tri, x):
    hi = x.astype(BF16)
    r1 = x - hi.astype(F32)
    mid = r1.astype(BF16)
    lo = (r1 - mid.astype(F32)).astype(BF16)
    return _mm(tri, hi) + _mm(tri, mid) + _mm(tri, lo)


def _mlstm_prompt(qk, v, o, gates, cw, cb, gb, hg, *, n_seq, seq_len):
    nc = seq_len // ML_CHUNK
    L = ML_CHUNK
    row = lambda width: pl.BlockSpec((L, width), lambda b, c: (b * nc + c, 0))
    per_seq = lambda *dims: pl.BlockSpec((1,) + dims, lambda b, c: (b,) + (0,) * len(dims))
    return pl.pallas_call(
        functools.partial(_mlstm_prompt_kernel, n_chunks=nc),
        grid=(n_seq, nc),
        in_specs=[row(2 * ML_WIDTH), row(ML_WIDTH), row(ML_WIDTH), row(LANES),
                  _const_spec(cw.shape), _const_spec(cb.shape), _const_spec(gb.shape), _const_spec(hg.shape)],
        out_specs=[row(ML_WIDTH), per_seq(ML_HEADS, ML_DH, ML_DH), per_seq(SUBLANES, ML_DH),
                   per_seq(SUBLANES, LANES), per_seq(SUBLANES, 2 * ML_WIDTH)],
        out_shape=[jax.ShapeDtypeStruct((n_seq * seq_len, ML_WIDTH), BF16),
                   jax.ShapeDtypeStruct((n_seq, ML_HEADS, ML_DH, ML_DH), F32),
                   jax.ShapeDtypeStruct((n_seq, SUBLANES, ML_DH), F32),
                   jax.ShapeDtypeStruct((n_seq, SUBLANES, LANES), F32),
                   jax.ShapeDtypeStruct((n_seq, SUBLANES, 2 * ML_WIDTH), F32)],
        scratch_shapes=[pltpu.VMEM((SUBLANES + L, 2 * ML_WIDTH), F32),
                        pltpu.VMEM((ML_HEADS, ML_DH, ML_DH), F32),
                        pltpu.VMEM((SUBLANES, ML_DH), F32),
                        pltpu.VMEM((SUBLANES, LANES), F32)],
        compiler_params=_params(2), name="mlstm_prompt",
    )(qk, v, o, gates, cw, cb, gb, hg)


def _mlstm_sample_kernel(qk_ref, v_ref, o_ref, gates_ref, st_ref, cw_ref, cb_ref, gb_ref, hg_ref,
                         c0_ref, n0_ref, m0_ref,
                         h_ref, c_out, n_out, m_out, ac_ref, *, n_seq, n_new):
    L = n_seq * n_new
    ext = jnp.concatenate([st_ref[...], qk_ref[...]], axis=0)
    y = cb_ref[...]
    for i in range(ML_CONV):
        y = y + ext[i * n_seq:i * n_seq + L, :] * cw_ref[i:i + 1, :]
    qk = y * _sigmoid(y)

    gl = gates_ref[...] + gb_ref[...]
    logf = _log_sigmoid(gl)
    tblk = lambda a, t: a[t * n_seq:(t + 1) * n_seq, :]
    b_t = [tblk(logf, 0)]
    for t in range(1, n_new):
        b_t.append(b_t[-1] + tblk(logf, t))
    bcol = jnp.concatenate(b_t, axis=0)
    bT = bcol.T
    glT = gl.T
    rows = lax.broadcasted_iota(jnp.int32, (L, L), 0)
    cols = lax.broadcasted_iota(jnp.int32, (L, L), 1)
    same_seq_causal = (rows >= cols) & (((rows - cols) % n_seq) == 0)
    row_seq = lax.broadcasted_iota(jnp.int32, (L, ML_DH), 0) % n_seq
    col_seq = lax.broadcasted_iota(jnp.int32, (ML_DH, L), 1) % n_seq
    m0 = m0_ref[...]
    m_out[...] = m0

    heads = []
    for h in range(ML_HEADS):
        sl = slice(h * ML_DH, (h + 1) * ML_DH)
        q = qk[:, sl]
        k = qk[:, ML_WIDTH + h * ML_DH:ML_WIDTH + (h + 1) * ML_DH] * (ML_DH ** -0.5)
        v = v_ref[:, sl]
        q_bf = q.astype(BF16)
        k_bf = k.astype(BF16)
        bc = bcol[:, ML_HEADS + h:ML_HEADS + h + 1]
        li_c = gl[:, h:h + 1]
        m0_h = m0[:, h:h + 1]
        m_rows = jnp.concatenate([m0_h] * n_new, axis=0)
        n0_h = n0_ref[:, sl]
        nq = jnp.sum(q * jnp.concatenate([n0_h] * n_new, axis=0), axis=-1, keepdims=True)

        b_last = tblk(bc, n_new - 1)
        g_t = [b_last - tblk(bc, t) + tblk(li_c, t) for t in range(n_new)]
        m_new = b_last + m0_h
        for t in range(n_new):
            m_new = jnp.maximum(m_new, g_t[t])
        a_s_t = [jnp.exp(g_t[t] - m_new) for t in range(n_new)]
        a_c = jnp.exp(b_last + m0_h - m_new)
        n_new_h = a_c * n0_h
        for t in range(n_new):
            n_new_h = n_new_h + a_s_t[t] * tblk(k, t)
        n_out[:, sl] = n_new_h
        m_out[:, h:h + 1] = m_new
        ac_ref[...] = jnp.broadcast_to(a_c, (n_seq, LANES))
        av_t = (jnp.concatenate(a_s_t, axis=0) * v).T

        def per_seq(b, cq):
            c_prev = c0_ref[b, h]
            part = _mm_nt(q_bf, c_prev.astype(BF16))
            cq = jnp.where(row_seq == b, part, cq)
            upd = _mm(jnp.where(col_seq == b, av_t, 0.0).astype(BF16), k_bf)
            c_out[b, h] = ac_ref[pl.ds(b, 1), :] * c_prev + upd
            return cq

        cq = lax.fori_loop(0, n_seq, per_seq, jnp.zeros((L, ML_DH), F32))
        heads.append(dict(q=q_bf, k=k_bf, v=v.astype(BF16), o=o_ref[:, sl], hg=hg_ref[:, sl], bc=bc,
                          br=bT[ML_HEADS + h:ML_HEADS + h + 1, :], li_r=glT[h:h + 1, :],
                          m_rows=m_rows, cq=cq, nq=nq))
    h_ref[...] = jnp.concatenate(_mlstm_heads(heads, same_seq_causal), axis=-1).astype(h_ref.dtype)


def _mlstm_sample(qk, v, o, gates, st, cw, cb, gb, hg, c0, n0, m0, *, n_seq, n_new):
    L = n_seq * n_new
    return pl.pallas_call(
        functools.partial(_mlstm_sample_kernel, n_seq=n_seq, n_new=n_new),
        out_shape=[jax.ShapeDtypeStruct((L, ML_WIDTH), BF16),
                   jax.ShapeDtypeStruct(c0.shape, F32),
                   jax.ShapeDtypeStruct(n0.shape, F32),
                   jax.ShapeDtypeStruct(m0.shape, F32)],
        scratch_shapes=[pltpu.VMEM((n_seq, LANES), F32)],
        compiler_params=pltpu.CompilerParams(vmem_limit_bytes=VMEM_LIMIT), name="mlstm_sample",
    )(qk, v, o, gates, st, cw, cb, gb, hg, c0, n0, m0)


def _neg_upper(n):
    rows = lax.broadcasted_iota(jnp.int32, (n, n), 0)
    cols = lax.broadcasted_iota(jnp.int32, (n, n), 1)
    return jnp.where(rows > cols, -1.0, 0.0).astype(BF16)


def _sb_weights(zs, neg_upper, rests, mask, chained=False, lo_pass=True):
    log_betas, totals, terms = [], [], []
    for z in zs:
        z2 = z * LOG2E
        sp = jnp.maximum(z2, 0.0) + jnp.log2(1.0 + jnp.exp2(-jnp.abs(z2)))
        if mask is not None:
            sp = jnp.where(mask, sp, 0.0)
        log_betas.append(z2 - sp), totals.append(sp[:, 0:1])
        terms.append(jnp.concatenate(_split_hi_lo(sp), axis=1) if lo_pass else sp.astype(BF16))
    upper = jnp.concatenate([neg_upper, neg_upper], axis=0) if lo_pass else neg_upper
    css = [_mm(t, upper) for t in terms]
    weights, new_rests = [], []
    for i, (lb, cs, tot) in enumerate(zip(log_betas, css, totals)):
        rest = new_rests[-1] if (chained and i) else rests[i]
        a = jnp.exp2(lb + cs + rest)
        if mask is not None:
            a = jnp.where(mask, a, 0.0)
        weights.append(a.astype(BF16))
        new_rests.append(rest + cs[:, 0:1] - tot)
    return weights, (new_rests[-1:] if chained else new_rests)


def _sb_prompt_kernel(bias_ref, q_ref, kt_ref, vt_ref, o_ref, kb_ref, vb_ref, z_ref, a_ref, *, blk):
    hg = q_ref.shape[1]
    hi_ = pl.program_id(1)
    qi = pl.program_id(2)
    n_blk = kb_ref.shape[1]
    heads = range(hg)

    @pl.when(qi == 0)
    def _():
        for hh in heads:
            for j in range(n_blk):
                kb_ref[hh, j] = kt_ref[0, hh, :, j * blk:(j + 1) * blk].astype(BF16)
                vb_ref[hh, j] = vt_ref[0, hh, :, j * blk:(j + 1) * blk].astype(BF16)

    neg_upper = _neg_upper(blk)
    strict_causal = (lax.broadcasted_iota(jnp.int32, (blk, blk), 1)
                     < lax.broadcasted_iota(jnp.int32, (blk, blk), 0))

    def logits(kb):
        return [_mm(q_ref[0, hh], kb_ref[hh, kb]) + bias_ref[hi_ * hg + hh] for hh in heads]

    def weighted_values(kb):
        return [_mm_nt(a_ref[hh], vb_ref[hh, kb]) for hh in heads]

    def visit(kb, mask, carry, first):
        accs, rests = carry
        zs = [z_ref[hh] for hh in heads]
        z_next = logits(jnp.maximum(kb - 1, 0))
        if not first:
            accs = tuple(a + p for a, p in zip(accs, weighted_values(kb + 1)))
        for hh in heads:
            z_ref[hh] = z_next[hh]
        weights, rests = _sb_weights(zs, neg_upper, rests, mask, lo_pass=False)
        for hh in heads:
            a_ref[hh] = weights[hh]
        return accs, tuple(rests)

    for hh, z in enumerate(logits(qi)):
        z_ref[hh] = z
    carry = ((jnp.zeros((blk, SB_DH), F32),) * hg, (jnp.zeros((blk, 1), F32),) * hg)
    carry = visit(qi, strict_causal, carry, True)
    accs, _ = lax.fori_loop(0, qi, lambda i, c: visit(qi - 1 - i, None, c, False), carry)
    accs = [a + p for a, p in zip(accs, weighted_values(0))]
    o_ref[...] = jnp.concatenate(accs, axis=-1).astype(o_ref.dtype)


def _sb_prompt(bias, qh, kt, vt, *, blk, heads_per_step):
    n_seq, n_heads, seq_len, dh = qh.shape
    nq = seq_len // blk
    hg = heads_per_step
    kv_spec = pl.BlockSpec((1, hg, dh, seq_len), lambda b, h, qi: (b, h, 0, 0))
    return pl.pallas_call(
        functools.partial(_sb_prompt_kernel, blk=blk),
        grid=(n_seq, n_heads // hg, nq),
        in_specs=[pl.BlockSpec(memory_space=pltpu.SMEM),
                  pl.BlockSpec((1, hg, blk, dh), lambda b, h, qi: (b, h, qi, 0)),
                  kv_spec, kv_spec],
        out_specs=pl.BlockSpec((blk, hg * dh), lambda b, h, qi: (b * nq + qi, h)),
        out_shape=jax.ShapeDtypeStruct((n_seq * seq_len, n_heads * dh), BF16),
        scratch_shapes=[pltpu.VMEM((hg, nq, dh, blk), BF16), pltpu.VMEM((hg, nq, dh, blk), BF16),
                        pltpu.VMEM((hg, blk, blk), F32), pltpu.VMEM((hg, blk, blk), BF16)],
        compiler_params=_params(3), name="sb_prompt",
    )(bias, qh, kt, vt)


def _sb_sample_kernel(pt_ref, q_ref, bias_ref, kn_ref, vn_ref, *refs, pages_per_step, n_new):
    del pt_ref
    k_refs = refs[:pages_per_step]
    v_refs = refs[pages_per_step:2 * pages_per_step]
    o_ref, acc_ref, rest_ref = refs[2 * pages_per_step:]
    j = pl.program_id(1)
    n_rows = n_new * SB_HEADS
    page = kn_ref.shape[1]
    head_of_col = lax.broadcasted_iota(jnp.int32, (n_rows, SB_WIDTH), 1) // SB_DH
    head_of_row = lax.broadcasted_iota(jnp.int32, (n_rows, SB_WIDTH), 0) % SB_HEADS
    own_head = head_of_col == head_of_row
    q = jnp.where(own_head, q_ref[0], 0.0).astype(BF16)
    bias = bias_ref[...]

    def visit(kts, vts, mask):
        zs = [_mm(q, kt) + bias for kt in kts]
        weights, rests = _sb_weights(zs, _neg_upper(kts[0].shape[1]), [rest_ref[:, 0:1]], mask, chained=True)
        acc_ref[...] += functools.reduce(lambda a, b: a + b, [_mm_nt(a, vt) for a, vt in zip(weights, vts)])
        rest_ref[...] = jnp.broadcast_to(rests[0], rest_ref.shape)

    @pl.when(j == 0)
    def _():
        acc_ref[...] = jnp.zeros_like(acc_ref)
        rest_ref[...] = jnp.zeros_like(rest_ref)
        t_of_row = lax.broadcasted_iota(jnp.int32, (n_rows, page), 0) // SB_HEADS
        s_of_col = lax.broadcasted_iota(jnp.int32, (n_rows, page), 1)
        visit([kn_ref[0].T.astype(BF16)], [vn_ref[0].T.astype(BF16)], s_of_col < t_of_row)

    def page_pair(p_refs, i):
        lo_hi = [p_refs[i + 1][0, 0].reshape(SB_WIDTH, page), p_refs[i][0, 0].reshape(SB_WIDTH, page)]
        return jnp.concatenate(lo_hi, axis=1).astype(BF16)

    pairs = range(0, pages_per_step, 2)
    visit([page_pair(k_refs, i) for i in pairs], [page_pair(v_refs, i) for i in pairs], None)

    @pl.when(j == pl.num_programs(1) - 1)
    def _():
        acc = jnp.where(own_head, acc_ref[...], 0.0)
        for t in range(n_new):
            o_ref[0, t:t + 1, :] = jnp.sum(acc[t * SB_HEADS:(t + 1) * SB_HEADS, :], axis=0, keepdims=True)


def _sb_sample(page_table, q_rows, bias_rows, k_new, v_new, cache_kt, cache_vt, *, layer, pages_per_step, n_new):
    n_seq, n_pages = page_table.shape
    page = cache_kt.shape[4]
    n_rows = n_new * SB_HEADS
    steps = n_pages // pages_per_step

    def page_spec(i):
        return pl.BlockSpec((1, 1, SB_HEADS, SB_DH, page),
                            lambda b, j, pt: (layer, pt[b, n_pages - 1 - (j * pages_per_step + i)], 0, 0, 0))

    per_seq = lambda r, c: pl.BlockSpec((1, r, c), lambda b, j, pt: (b, 0, 0))
    grid_spec = pltpu.PrefetchScalarGridSpec(
        num_scalar_prefetch=1, grid=(n_seq, steps),
        in_specs=[per_seq(n_rows, SB_WIDTH),
                  pl.BlockSpec((n_rows, 1), lambda b, j, pt: (0, 0)),
                  per_seq(page, SB_WIDTH), per_seq(page, SB_WIDTH)]
                 + [page_spec(i) for i in range(pages_per_step)] * 2,
        out_specs=per_seq(n_new, SB_WIDTH),
        scratch_shapes=[pltpu.VMEM((n_rows, SB_WIDTH), F32), pltpu.VMEM((n_rows, LANES), F32)])
    return pl.pallas_call(
        functools.partial(_sb_sample_kernel, pages_per_step=pages_per_step, n_new=n_new),
        grid_spec=grid_spec,
        out_shape=jax.ShapeDtypeStruct((n_seq, n_new, SB_WIDTH), F32),
        compiler_params=_params(2), name="sb_sample",
    )(page_table, q_rows, bias_rows, k_new, v_new,
      *([cache_kt] * pages_per_step), *([cache_vt] * pages_per_step))


def _proj_res_kernel(x_ref, a_ref, b_ref, w_ref, o_ref):
    ka = a_ref.shape[1]
    o_ref[...] = (x_ref[...] + _mm(a_ref[...].astype(BF16), w_ref[0:ka, :])
                  + _mm(b_ref[...].astype(BF16), w_ref[ka:, :]))


def _proj_res(x, a, b, w, *, tm):
    m, d = x.shape
    row = lambda width: pl.BlockSpec((tm, width), lambda i: (i, 0))
    return pl.pallas_call(
        _proj_res_kernel, grid=(m // tm,),
        in_specs=[row(d), row(a.shape[1]), row(b.shape[1]), _const_spec(w.shape)],
        out_specs=row(d), out_shape=jax.ShapeDtypeStruct((m, d), F32),
        compiler_params=_params(1), name="proj_res",
    )(x, a, b, w)


def _ffn_chunks(xn, win_ref, wout_ref, cw_ref, cb_ref, o_ref, conv_fn, d_ff):
    bounds = [(c0, min(c0 + FFN_CHUNK, d_ff)) for c0 in range(0, d_ff, FFN_CHUNK)]

    def up(c):
        lo, hi = bounds[c]
        return _mm(xn, win_ref[:, lo:hi]), _mm(xn, win_ref[:, d_ff + lo:d_ff + hi])

    nxt = up(0)
    for c in range(len(bounds)):
        sl = slice(*bounds[c])
        g, u = nxt
        if c + 1 < len(bounds):
            nxt = up(c + 1)
        g_m2, g_m1 = conv_fn(g, sl)
        y = cb_ref[:, sl] + g_m2 * cw_ref[0:1, sl] + g_m1 * cw_ref[1:2, sl] + g * cw_ref[2:3, sl]
        hmid = (_gelu(y) * u).astype(BF16)
        o_ref[...] += _mm(hmid, wout_ref[sl, :])


def _ffn_prompt_kernel(x_ref, g_ref, win_ref, cw_ref, cb_ref, wout_ref, fg_ref, *rest,
                       tiles_per_seq, d_ff, final_norm, mixer_proj):
    if mixer_proj:
        a_ref, b_ref, wmix_ref, o_ref, tail_out, xn_ref, prev_ref, st_ref = rest
    else:
        o_ref, tail_out, xn_ref, prev_ref, st_ref = rest
    i = pl.program_id(0)
    tm = x_ref.shape[0]
    halo = SUBLANES

    @pl.when(i % tiles_per_seq == 0)
    def _():
        prev_ref[...] = jnp.zeros_like(prev_ref)

    x = x_ref[...]
    if mixer_proj:
        ka = a_ref.shape[1]
        x = x + _mm(a_ref[...], wmix_ref[0:ka, :]) + _mm(b_ref[...], wmix_ref[ka:, :])
    xn_ref[...] = _rms(x, g_ref[...]).astype(BF16)
    o_ref[...] = x

    def conv_fn(g, sl):
        w = sl.stop - sl.start
        st_ref[0:halo, 0:w] = prev_ref[:, sl]
        st_ref[halo:halo + tm, 0:w] = g
        prev_ref[:, sl] = g[tm - halo:tm, :]
        return st_ref[halo - 2:halo - 2 + tm, 0:w], st_ref[halo - 1:halo - 1 + tm, 0:w]

    _ffn_chunks(xn_ref[...], win_ref, wout_ref, cw_ref, cb_ref, o_ref, conv_fn, d_ff)
    if final_norm:
        o_ref[...] = _rms(o_ref[...], fg_ref[...])

    @pl.when(i % tiles_per_seq == tiles_per_seq - 1)
    def _():
        tail_out[0] = prev_ref[...]


def _ffn_prompt(x, g, win, cw, cb, wout, fg, mixer=None, *, n_seq, seq_len, tm, final_norm):
    m, d = x.shape
    d_ff = wout.shape[0]
    tps = seq_len // tm
    row = lambda width: pl.BlockSpec((tm, width), lambda i: (i, 0))
    operands = [x, g, win, cw, cb, wout, fg]
    in_specs = [row(d)] + [_const_spec(a.shape) for a in operands[1:]]
    if mixer is not None:
        a, b, w_mix = mixer
        operands += [a, b, w_mix]
        in_specs += [row(a.shape[1]), row(b.shape[1]), _const_spec(w_mix.shape)]
    return pl.pallas_call(
        functools.partial(_ffn_prompt_kernel, tiles_per_seq=tps, d_ff=d_ff, final_norm=final_norm,
                          mixer_proj=mixer is not None),
        grid=(m // tm,),
        in_specs=in_specs,
        out_specs=[row(d), pl.BlockSpec((1, SUBLANES, d_ff), lambda i: (i // tps, 0, 0))],
        out_shape=[jax.ShapeDtypeStruct((m, d), F32), jax.ShapeDtypeStruct((n_seq, SUBLANES, d_ff), F32)],
        scratch_shapes=[pltpu.VMEM((tm, d), BF16), pltpu.VMEM((SUBLANES, d_ff), F32),
                        pltpu.VMEM((SUBLANES + tm, FFN_CHUNK), F32)],
        compiler_params=_params(1), name="ffn_prompt",
    )(*operands)


def _ffn_sample_kernel(x_ref, g_ref, win_ref, cw_ref, cb_ref, wout_ref, fg_ref, st_ref, o_ref, tail_out,
                       *, n_seq, d_ff, final_norm):
    rows = x_ref.shape[0]
    x = x_ref[...]
    xn = _rms(x, g_ref[...]).astype(BF16)
    o_ref[...] = x

    def conv_fn(g, sl):
        ext = jnp.concatenate([st_ref[:, sl], g], axis=0)
        tail_out[:, sl] = g[rows - 2 * n_seq:rows, :]
        return ext[0:rows, :], ext[n_seq:n_seq + rows, :]

    _ffn_chunks(xn, win_ref, wout_ref, cw_ref, cb_ref, o_ref, conv_fn, d_ff)
    if final_norm:
        o_ref[...] = _rms(o_ref[...], fg_ref[...])


def _ffn_sample(x, g, win, cw, cb, wout, fg, st, *, n_seq, final_norm):
    d_ff = wout.shape[0]
    return pl.pallas_call(
        functools.partial(_ffn_sample_kernel, n_seq=n_seq, d_ff=d_ff, final_norm=final_norm),
        out_shape=[jax.ShapeDtypeStruct(x.shape, F32), jax.ShapeDtypeStruct(st.shape, F32)],
        compiler_params=pltpu.CompilerParams(vmem_limit_bytes=VMEM_LIMIT), name="ffn_sample",
    )(x, g, win, cw, cb, wout, fg, st)


def _layernorm_stats(v_ref, width, n_groups):
    gw = width // n_groups
    s1 = 0.0
    for gi in range(n_groups):
        s1 = s1 + jnp.sum(v_ref[:, gi * gw:(gi + 1) * gw], axis=-1, keepdims=True)
    mean = s1 / width
    s2 = 0.0
    for gi in range(n_groups):
        xc = v_ref[:, gi * gw:(gi + 1) * gw] - mean
        s2 = s2 + jnp.sum(xc * xc, axis=-1, keepdims=True)
    return mean, lax.rsqrt(s2 / width + EPS)


def _cm_prompt_kernel(x_ref, g_ref, win_ref, lng_ref, lnb_ref, ws_ref, bst_ref, wout_ref, o_ref,
                      xn_ref, v_ref, *, width):
    tm = x_ref.shape[0]
    gw = width // CM_GROUPS
    x = x_ref[...]
    xn_ref[...] = _rms(x, g_ref[...]).astype(BF16)
    o_ref[...] = x
    proj = lambda col0, gi: _mm(xn_ref[...], win_ref[:, col0 + gi * gw:col0 + (gi + 1) * gw])
    nxt = proj(width, 0)
    for gi in range(CM_GROUPS):
        cur, nxt = nxt, (proj(width, gi + 1) if gi + 1 < CM_GROUPS else proj(0, 0))
        v_ref[:, gi * gw:(gi + 1) * gw] = _gelu(cur)
    mean, rstd = _layernorm_stats(v_ref, width, CM_GROUPS)
    rows = lax.broadcasted_iota(jnp.int32, (CM_CHUNK, CM_CHUNK), 0)
    cols = lax.broadcasted_iota(jnp.int32, (CM_CHUNK, CM_CHUNK), 1)
    tril = rows >= cols
    gated = []
    for gi in range(CM_GROUPS):
        sl = slice(gi * gw, (gi + 1) * gw)
        u_pre, nxt = nxt, (proj(0, gi + 1) if gi + 1 < CM_GROUPS else None)
        vn = ((v_ref[:, sl] - mean) * rstd * lng_ref[:, sl] + lnb_ref[:, sl]).astype(BF16)
        wsg = jnp.where(tril, ws_ref[gi], 0.0).astype(BF16)
        bias = bst_ref[:, gi:gi + 1]
        mixed = jnp.concatenate(
            [_mm(wsg, vn[c * CM_CHUNK:(c + 1) * CM_CHUNK, :]) + bias for c in range(tm // CM_CHUNK)], axis=0)
        gated.append((_gelu(u_pre) * mixed).astype(BF16))
        if len(gated) == CM_OUT_GROUPS:
            lo = (gi + 1 - CM_OUT_GROUPS) * gw
            o_ref[...] += _mm(jnp.concatenate(gated, axis=1), wout_ref[lo:(gi + 1) * gw, :])
            gated = []


def _cm_prompt(x, g, win, lng, lnb, ws, bst, wout, *, tm):
    m, d = x.shape
    width = wout.shape[0]
    row = pl.BlockSpec((tm, d), lambda i: (i, 0))
    return pl.pallas_call(
        functools.partial(_cm_prompt_kernel, width=width),
        grid=(m // tm,),
        in_specs=[row] + [_const_spec(a.shape) for a in (g, win, lng, lnb, ws, bst, wout)],
        out_specs=row, out_shape=jax.ShapeDtypeStruct((m, d), F32),
        scratch_shapes=[pltpu.VMEM((tm, d), BF16), pltpu.VMEM((tm, width), F32)],
        compiler_params=_params(1), name="cm_prompt",
    )(x, g, win, lng, lnb, ws, bst, wout)


def _cm_sample_kernel(x_ref, g_ref, win_ref, lng_ref, lnb_ref, wexp_ref, bexp_ref, wout_ref, o_ref, v_out,
                      *, n_seq, n_new, width):
    gw = width // CM_GROUPS
    x = x_ref[...]
    xn = _rms(x, g_ref[...]).astype(BF16)
    for gi in range(CM_GROUPS):
        v_out[:, gi * gw:(gi + 1) * gw] = _gelu(_mm(xn, win_ref[:, width + gi * gw:width + (gi + 1) * gw]))
    mean, rstd = _layernorm_stats(v_out, width, CM_GROUPS)
    acc = x
    for gi in range(CM_GROUPS):
        sl = slice(gi * gw, (gi + 1) * gw)
        vn = (v_out[:, sl] - mean) * rstd * lng_ref[:, sl] + lnb_ref[:, sl]
        v_out[:, sl] = vn
        mixed = []
        for t in range(n_new):
            mt = bexp_ref[t:t + 1, sl]
            for s in range(t + 1):
                mt = mt + wexp_ref[t * n_new + s:t * n_new + s + 1, sl] * vn[s * n_seq:(s + 1) * n_seq, :]
            mixed.append(mt)
        u = _gelu(_mm(xn, win_ref[:, sl]))
        acc = acc + _mm((u * jnp.concatenate(mixed, axis=0)).astype(BF16), wout_ref[sl, :])
    o_ref[...] = acc


def _cm_sample(x, g, win, lng, lnb, wexp, bexp, wout, *, n_seq, n_new):
    width = wout.shape[0]
    return pl.pallas_call(
        functools.partial(_cm_sample_kernel, n_seq=n_seq, n_new=n_new, width=width),
        out_shape=[jax.ShapeDtypeStruct(x.shape, F32), jax.ShapeDtypeStruct((x.shape[0], width), F32)],
        compiler_params=pltpu.CompilerParams(vmem_limit_bytes=VMEM_LIMIT), name="cm_sample",
    )(x, g, win, lng, lnb, wexp, bexp, wout)


def _time_major(a):
    return jnp.swapaxes(a, 0, 1).reshape((a.shape[0] * a.shape[1],) + a.shape[2:])


def _batch_major(a, n_seq):
    return jnp.swapaxes(a.reshape((a.shape[0] // n_seq, n_seq) + a.shape[1:]), 0, 1)


def _pad_lanes(a):
    return jnp.pad(a, [(0, 0)] * (a.ndim - 1) + [(0, LANES - a.shape[-1])])


def kernel(x_prompt, x_sample, cache_k, cache_v, page_table, state_mlstm_c, state_mlstm_n, state_mlstm_m, state_mlstm_conv, state_ffn_conv, norm_mix_g, norm_ffn_g, norm_final_g, w_in_even, ml_conv_w, ml_conv_b, ml_gate_b, ml_head_g, sb_logit_b, w_out_even, w_in_odd, cm_ln_g, cm_ln_b, cm_spatial_w, cm_spatial_b, w_out_odd, ffn_w_in, ffn_conv_w, ffn_conv_b, ffn_w_out):
    n_pb, seq_p, d = x_prompt.shape
    n_sb, seq_s, _ = x_sample.shape
    depth = norm_mix_g.shape[0]
    d_ff = ffn_w_out.shape[1]
    hp = x_prompt.reshape(n_pb * seq_p, d)
    hs = _time_major(x_sample)
    fg = norm_final_g.reshape(1, d)

    kp_l, vp_l, ks_l, vs_l = [], [], [], []
    cp_l, np_l, mp_l, bp_l = [], [], [], []
    cs_l, ns_l, ms_l, bs_l = [], [], [], []
    cmv_l, fp_l, fs_l = [], [], []
    for layer in range(depth):
        mix_g = norm_mix_g[layer].reshape(1, d)
        if layer % 2 == 0:
            e = layer // 2
            w = w_in_even[e]
            gate_lo = _C_O + ML_WIDTH
            gate_hi = gate_lo + 2 * ML_HEADS
            w_all = jnp.concatenate([w[:, :gate_lo], w[:, gate_hi:], _pad_lanes(w[:, gate_lo:gate_hi])],
                                    axis=1).astype(BF16)
            w_kvt = w[:, gate_hi + SB_WIDTH:].T.astype(BF16)
            cw, cb = ml_conv_w[e], ml_conv_b[e].reshape(1, -1)
            gb = _pad_lanes(ml_gate_b[e].reshape(1, -1))
            hg = ml_head_g[e].reshape(1, -1)
            w_out = w_out_even[e].astype(BF16)
            qk, vm, om, gates, qh, kt, vt = _even_in(
                hp, mix_g, w_all, w_kvt, n_seq=n_pb, seq_len=seq_p, tm=EVEN_IN_ROWS, head_major=True)
            h_ml, c_p, n_p, m_p, tail_p = _mlstm_prompt(qk, vm, om, gates, cw, cb, gb, hg, n_seq=n_pb, seq_len=seq_p)
            h_sb = _sb_prompt(sb_logit_b[e], qh, kt, vt, blk=SB_BLOCK, heads_per_step=SB_HEADS_PER_STEP)
            prompt_mixer = (h_ml, h_sb, w_out)
            kp_l.append(jnp.transpose(kt, (0, 3, 1, 2)))
            vp_l.append(jnp.transpose(vt, (0, 3, 1, 2)))
            cp_l.append(c_p)
            np_l.append(n_p[:, :ML_HEADS])
            mp_l.append(m_p[:, :ML_HEADS, 0])
            bp_l.append(tail_p[:, SUBLANES - (ML_CONV - 1):])
            qk, vm, om, gates, sq, sk, sv = _even_in(
                hs, mix_g, w_all, w_kvt, n_seq=n_sb, seq_len=seq_s, tm=n_sb * seq_s, head_major=False)
            h_ml, c_s, n_s, m_s = _mlstm_sample(
                qk, vm, om, gates, _time_major(state_mlstm_conv[e]), cw, cb, gb, hg,
                state_mlstm_c[e], state_mlstm_n[e].reshape(n_sb, ML_WIDTH), _pad_lanes(state_mlstm_m[e]),
                n_seq=n_sb, n_new=seq_s)
            page = cache_k.shape[2]
            pad_keys = lambda a: jnp.pad(_batch_major(a, n_sb), ((0, 0), (0, page - seq_s), (0, 0)))
            q_rows = jnp.broadcast_to(_batch_major(sq, n_sb)[:, :, None, :],
                                      (n_sb, seq_s, SB_HEADS, SB_WIDTH)).reshape(n_sb, seq_s * SB_HEADS, SB_WIDTH)
            bias_rows = jnp.tile(sb_logit_b[e], seq_s).reshape(seq_s * SB_HEADS, 1)
            pool_view = lambda c: jnp.transpose(c, (0, 1, 3, 4, 2))
            h_sb = _sb_sample(page_table, q_rows, bias_rows, pad_keys(sk), pad_keys(sv),
                              pool_view(cache_k), pool_view(cache_v), layer=e,
                              pages_per_step=SB_PAGES_PER_STEP, n_new=seq_s)
            hs = _proj_res(hs, h_ml, _time_major(h_sb), w_out, tm=n_sb * seq_s)
            ks_l.append(_batch_major(sk, n_sb).reshape(n_sb, seq_s, SB_HEADS, SB_DH))
            vs_l.append(_batch_major(sv, n_sb).reshape(n_sb, seq_s, SB_HEADS, SB_DH))
            cs_l.append(c_s)
            ns_l.append(n_s.reshape(n_sb, ML_HEADS, ML_DH))
            ms_l.append(m_s[:, :ML_HEADS])
            bs_l.append(_batch_major(qk[(seq_s - (ML_CONV - 1)) * n_sb:], n_sb))
        else:
            o = layer // 2
            win = w_in_odd[o].astype(BF16)
            wout = w_out_odd[o].astype(BF16)
            lng, lnb = cm_ln_g[o].reshape(1, -1), cm_ln_b[o].reshape(1, -1)
            width = wout.shape[0]
            gw = width // CM_GROUPS
            prompt_mixer = None
            hp = _cm_prompt(hp, mix_g, win, lng, lnb, cm_spatial_w[o], cm_spatial_b[o].T, wout, tm=CM_ROWS)
            wexp = jnp.repeat(cm_spatial_w[o][:, :seq_s, :seq_s].reshape(CM_GROUPS, seq_s * seq_s).T, gw, axis=1)
            bexp = jnp.repeat(cm_spatial_b[o][:, :seq_s].T, gw, axis=1)
            hs, v_rows = _cm_sample(hs, mix_g, win, lng, lnb, wexp, bexp, wout, n_seq=n_sb, n_new=seq_s)
            cmv_l.append(_batch_major(v_rows, n_sb))
        ffn_g = norm_ffn_g[layer].reshape(1, d)
        win = ffn_w_in[layer].astype(BF16)
        wout = ffn_w_out[layer].astype(BF16)
        cw, cb = ffn_conv_w[layer], ffn_conv_b[layer].reshape(1, -1)
        last = layer == depth - 1
        hp, tail_p = _ffn_prompt(hp, ffn_g, win, cw, cb, wout, fg, prompt_mixer, n_seq=n_pb, seq_len=seq_p,
                                 tm=FFN_ROWS, final_norm=last)
        hs, tail_s = _ffn_sample(hs, ffn_g, win, cw, cb, wout, fg, _time_major(state_ffn_conv[layer]),
                                 n_seq=n_sb, final_norm=last)
        fp_l.append(tail_p[:, SUBLANES - (FFN_CONV - 1):])
        fs_l.append(_batch_major(tail_s, n_sb))

    return (hp.reshape(n_pb, seq_p, d), _batch_major(hs, n_sb),
            jnp.stack(kp_l), jnp.stack(vp_l), jnp.stack(ks_l), jnp.stack(vs_l),
            jnp.stack(cp_l), jnp.stack(np_l), jnp.stack(mp_l), jnp.stack(bp_l),
            jnp.stack(cs_l), jnp.stack(ns_l), jnp.stack(ms_l), jnp.stack(bs_l),
            jnp.stack(cmv_l), jnp.stack(fp_l), jnp.stack(fs_l))
```
